```python
import jax
import jax.numpy as jnp
from jax import lax
import numpy as np

D_MODEL = 1024
BATCH = 2
SEQ = 8192
DEPTH = 4
DEC_BATCH = 32
DEC_SEQ = 4
PAST_LEN = 8192
PAGE_SIZE = 128

N_MIXERS = 3
NORM_EPS = 1e-6
D_FF = -(-(8 * D_MODEL) // (3 * 256)) * 256

SSM_D_INNER = 2 * D_MODEL
SSM_HEAD_DIM = 64
SSM_N_HEADS = SSM_D_INNER // SSM_HEAD_DIM
SSM_N_GROUPS = 4
SSM_D_STATE = 128
SSM_CONV = 4
SSM_CHUNK = 128
SSM_CONV_DIM = SSM_D_INNER + 2 * SSM_N_GROUPS * SSM_D_STATE
SSM_IN_DIM = SSM_D_INNER + SSM_CONV_DIM + SSM_N_HEADS

GLA_N_HEADS = 4
GLA_DK = D_MODEL // 2
GLA_DV = D_MODEL
GLA_HEAD_K = GLA_DK // GLA_N_HEADS
GLA_HEAD_V = GLA_DV // GLA_N_HEADS
GLA_GATE_RANK = 16
GLA_GATE_NORM = 16.0
GLA_CHUNK = 32
GLA_IN_DIM = 2 * GLA_DK + 2 * GLA_DV + GLA_GATE_RANK

ATT_GROUPS = ((128, 1), (512, 4), (2048, 16))
ATT_HEADS_PER_GROUP = 4
ATT_HEAD_DIM = 64
ATT_N_HEADS = ATT_HEADS_PER_GROUP * len(ATT_GROUPS)
ATT_ROT_DIM = ATT_HEAD_DIM // 4
ROPE_THETA = 500000.0
ATT_BLOCK = 128

N_SSM_LAYERS = len(range(0, DEPTH, N_MIXERS))
N_GLA_LAYERS = len(range(1, DEPTH, N_MIXERS))
N_ATT_LAYERS = len(range(2, DEPTH, N_MIXERS))

kernel_name = 'hybrid_ssd_gla_dilated_swa_step'


def rmsnorm(x, g):
    xf = x.astype(jnp.float32)
    inv = lax.rsqrt(jnp.mean(xf * xf, axis=-1, keepdims=True) + NORM_EPS)
    return (xf * inv).astype(x.dtype) * g


def swiglu(x, w_gate, w_up, w_down):
    return (jax.nn.silu(x @ w_gate) * (x @ w_up)) @ w_down


def causal_dwconv(x, buf, w, b):
    xp = jnp.concatenate([buf, x], axis=1)
    L = x.shape[1]
    y = b
    for t in range(SSM_CONV):
        y = y + xp[:, t:t + L] * w[t]
    return y, xp[:, -(SSM_CONV - 1):]


def segsum(a):
    T = a.shape[-1]
    cs = jnp.cumsum(a, axis=-1)
    diff = cs[..., :, None] - cs[..., None, :]
    lower = jnp.arange(T)[:, None] >= jnp.arange(T)[None, :]
    return jnp.where(lower, diff, -jnp.inf)


def ssd_scan(x, dt, A, Bm, Cm, h0, chunk):
    b, l, H, P = x.shape
    G, N = Bm.shape[-2:]
    hg = H // G
    c = l // chunk
    xdt = (x * dt[..., None].astype(x.dtype)).reshape(b, c, chunk, G, hg, P)
    a = (A * dt).reshape(b, c, chunk, G, hg).transpose(0, 3, 4, 1, 2)
    Bc = Bm.reshape(b, c, chunk, G, N)
    Cc = Cm.reshape(b, c, chunk, G, N)
    a_cs = jnp.cumsum(a, axis=-1)
    Ldec = jnp.exp(segsum(a)).astype(x.dtype)
    CB = jnp.einsum('bcqgn,bcsgn->bgcqs', Cc, Bc)
    y_diag = jnp.einsum('bgcqs,bghcqs,bcsghp->bcqghp', CB, Ldec, xdt)
    decay_st = jnp.exp(a_cs[..., -1:] - a_cs).astype(x.dtype)
    st = jnp.einsum('bcqgn,bghcq,bcqghp->bcghpn', Bc, decay_st, xdt)
    st = jnp.concatenate([h0.reshape(b, 1, G, hg, P, N).astype(st.dtype), st], axis=1)
    a_last = jnp.pad(a_cs[..., -1], ((0, 0), (0, 0), (0, 0), (1, 0)))
    dchunk = jnp.exp(segsum(a_last)).astype(x.dtype)
    new_st = jnp.einsum('bghzc,bcghpn->bzghpn', dchunk, st)
    st_in, h_final = new_st[:, :-1], new_st[:, -1]
    y_off = jnp.einsum('bcqgn,bcghpn,bghcq->bcqghp', Cc, st_in, jnp.exp(a_cs).astype(x.dtype))
    y = (y_diag + y_off).reshape(b, l, H, P)
    return y, h_final.reshape(b, H, P, N)


def mamba2_mixer(h, conv_buf, ssm_state, w_in, conv_w, conv_b, dt_bias, a_log, d_skip, norm_w, w_out, chunk):
    b, l, _ = h.shape
    z, xbc, dt = jnp.split(h @ w_in, [SSM_D_INNER, SSM_D_INNER + SSM_CONV_DIM], axis=-1)
    xbc, conv_new = causal_dwconv(xbc, conv_buf, conv_w, conv_b)
    xbc = jax.nn.silu(xbc)
    xs, Bm, Cm = jnp.split(xbc, [SSM_D_INNER, SSM_D_INNER + SSM_N_GROUPS * SSM_D_STATE], axis=-1)
    xs = xs.reshape(b, l, SSM_N_HEADS, SSM_HEAD_DIM)
    Bm = Bm.reshape(b, l, SSM_N_GROUPS, SSM_D_STATE)
    Cm = Cm.reshape(b, l, SSM_N_GROUPS, SSM_D_STATE)
    dt = jax.nn.softplus((dt + dt_bias).astype(jnp.float32))
    A = -jnp.exp(a_log.astype(jnp.float32))
    y, ssm_new = ssd_scan(xs, dt, A, Bm, Cm, ssm_state, chunk)
    y = (y + xs * d_skip[:, None]).reshape(b, l, SSM_D_INNER) * jax.nn.silu(z)
    y = rmsnorm(y.reshape(b, l, SSM_N_GROUPS, -1), norm_w.reshape(SSM_N_GROUPS, -1)).reshape(b, l, SSM_D_INNER)
    return y @ w_out, conv_new, ssm_new.astype(ssm_state.dtype)


def gla_chunk_scan(q, k, v, log_a, s0, chunk):
    b, l, H, _ = q.shape
    c = l // chunk
    causal = jnp.arange(chunk)[:, None] >= jnp.arange(chunk)[None, :]

    def to_chunks(t):
        return t.reshape(b, c, chunk, H, t.shape[-1]).swapaxes(0, 1)

    def step(S, inp):
        qc, kc, vc, gc = inp
        bcum = jnp.cumsum(gc, axis=1)
        q_t = qc * jnp.exp(bcum).astype(qc.dtype)
        k_t = kc * jnp.exp(-bcum).astype(kc.dtype)
        att = jnp.where(causal, jnp.einsum('bqhk,bshk->bhqs', q_t, k_t), 0.0)
        o = jnp.einsum('bhqs,bshv->bqhv', att, vc) + jnp.einsum('bqhk,bhkv->bqhv', q_t, S)
        b_last = bcum[:, -1]
        k_dec = kc * jnp.exp(b_last[:, None] - bcum).astype(kc.dtype)
        S_new = S * jnp.exp(b_last)[..., None].astype(S.dtype) + jnp.einsum('bqhk,bqhv->bhkv', k_dec, vc)
        return S_new.astype(S.dtype), o

    S, o = lax.scan(step, s0, (to_chunks(q), to_chunks(k), to_chunks(v), to_chunks(log_a)))
    return o.swapaxes(0, 1).reshape(b, l, H, v.shape[-1]), S


def gla_mixer(h, s0, w_in, w_gate, gate_bias, norm_w, w_out, chunk):
    b, l, _ = h.shape
    q, k, v, r, g_low = jnp.split(h @ w_in, [GLA_DK, 2 * GLA_DK, 2 * GLA_DK + GLA_DV, 2 * GLA_DK + 2 * GLA_DV], axis=-1)
    log_a = jax.nn.log_sigmoid((g_low @ w_gate + gate_bias).astype(jnp.float32)) / GLA_GATE_NORM
    hk = lambda t: t.reshape(b, l, GLA_N_HEADS, GLA_HEAD_K)
    o, s_new = gla_chunk_scan(hk(q) * GLA_HEAD_K ** -0.5, hk(k), v.reshape(b, l, GLA_N_HEADS, GLA_HEAD_V),
                              hk(log_a), s0, chunk)
    o = rmsnorm(o, norm_w).reshape(b, l, GLA_DV) * jax.nn.silu(r)
    return o @ w_out, s_new


def rope_partial(x, pos):
    inv_freq = ROPE_THETA ** (-jnp.arange(0, ATT_ROT_DIM, 2, dtype=jnp.float32) / ATT_ROT_DIM)
    ang = pos.astype(jnp.float32)[:, None] * inv_freq
    cos, sin = jnp.cos(ang)[:, None, :], jnp.sin(ang)[:, None, :]
    half = ATT_ROT_DIM // 2
    x1, x2, rest = x[..., :half], x[..., half:ATT_ROT_DIM], x[..., ATT_ROT_DIM:]
    rot = jnp.concatenate([x1 * cos - x2 * sin, x2 * cos + x1 * sin], axis=-1).astype(x.dtype)
    return jnp.concatenate([rot, rest], axis=-1)


def masked_softmax_stats(s, valid):
    s = jnp.where(valid, s, -jnp.inf)
    mx = jnp.max(s, axis=-1, keepdims=True)
    p = jnp.exp(s - mx)
    den = jnp.sum(p, axis=-1, keepdims=True)
    return p / den, (mx + jnp.log(den))[..., 0]


def dilated_window_prompt(q, kv, window, dilation):
    b, l, hg, dh = q.shape
    m = l // dilation
    span = window // dilation
    N = b * dilation

    def by_residue(t):
        t = t.reshape((b, m, dilation) + t.shape[2:])
        return jnp.moveaxis(t, 2, 1).reshape((N, m) + t.shape[3:])

    def pad_seq(t, front, back):
        return jnp.pad(t, [(0, 0), (front, back)] + [(0, 0)] * (t.ndim - 2))

    qs, kvs = by_residue(q), by_residue(kv)
    nb = -(-m // ATT_BLOCK)
    mp = nb * ATT_BLOCK
    qb = pad_seq(qs, 0, mp - m).reshape(N, nb, ATT_BLOCK, hg, dh)
    kvp = pad_seq(kvs, ATT_BLOCK, mp - m).reshape(N, nb + 1, ATT_BLOCK, 2, hg, dh)
    kvb = jnp.concatenate([kvp[:, :-1], kvp[:, 1:]], axis=2)
    s = jnp.einsum('nbqhd,nbkhd->nbhqk', qb, kvb[:, :, :, 0]).astype(jnp.float32) * ATT_HEAD_DIM ** -0.5
    qi = jnp.arange(ATT_BLOCK)[:, None]
    kj = jnp.arange(2 * ATT_BLOCK)[None, :]
    dist = ATT_BLOCK + qi - kj
    key_pos = jnp.arange(nb)[:, None, None] * ATT_BLOCK - ATT_BLOCK + kj[None]
    valid = (dist >= 0)[None] & (dist <= span)[None] & (key_pos >= 0)
    p, lse = masked_softmax_stats(s, valid[None, :, None])
    o = jnp.einsum('nbhqk,nbkhd->nbqhd', p.astype(q.dtype), kvb[:, :, :, 1])
    o = o.reshape(N, mp, hg, dh)[:, :m].reshape(b, dilation, m, hg, dh)
    o = jnp.moveaxis(o, 1, 2).reshape(b, l, hg, dh)
    lse = lse.transpose(0, 1, 3, 2).reshape(N, mp, hg)[:, :m].reshape(b, dilation, m, hg)
    lse = jnp.moveaxis(lse, 1, 2).reshape(b, l, hg)
    return o, lse


def dilated_window_sample(q, kv_new, kv_buf, window, dilation):
    L = kv_buf.shape[1]
    s_len = q.shape[1]
    kvx = jnp.concatenate([kv_buf, kv_new], axis=1)
    span = window // dilation
    idx = L + jnp.arange(s_len)[:, None] - dilation * jnp.arange(span + 1)[None, :]
    valid = idx >= 0
    g = kvx[:, jnp.maximum(idx, 0)]
    sc = jnp.einsum('bshd,bsjhd->bshj', q, g[:, :, :, 0]).astype(jnp.float32) * ATT_HEAD_DIM ** -0.5
    p, lse = masked_softmax_stats(sc, valid[None, :, None, :])
    o = jnp.einsum('bshj,bsjhd->bshd', p.astype(q.dtype), g[:, :, :, 1])
    keep = min(window, L + s_len)
    return o, lse, kvx[:, -keep:]


def dilated_attn_mixer(h, pos, kv_bufs, w_qkv, w_out):
    b, l, _ = h.shape
    qkv = (h @ w_qkv).reshape(b, l, 3, ATT_N_HEADS, ATT_HEAD_DIM)
    q = rope_partial(qkv[:, :, 0], pos)
    k = rope_partial(qkv[:, :, 1], pos)
    v = qkv[:, :, 2]
    outs, lses, new_kv = [], [], []
    for gi, (window, dilation) in enumerate(ATT_GROUPS):
        hs = slice(gi * ATT_HEADS_PER_GROUP, (gi + 1) * ATT_HEADS_PER_GROUP)
        kv = jnp.stack([k[:, :, hs], v[:, :, hs]], axis=2)
        if kv_bufs is None:
            o, lse = dilated_window_prompt(q[:, :, hs], kv, window, dilation)
            new_kv.append(kv[:, -min(window, l):])
        else:
            o, lse, kv_upd = dilated_window_sample(q[:, :, hs], kv, kv_bufs[gi], window, dilation)
            new_kv.append(kv_upd)
        outs.append(o)
        lses.append(lse)
    alpha = jax.nn.softmax(jnp.stack(lses, axis=0), axis=0)
    mixed = jnp.concatenate([o * alpha[gi][..., None].astype(o.dtype) for gi, o in enumerate(outs)], axis=2)
    return mixed.reshape(b, l, ATT_N_HEADS * ATT_HEAD_DIM) @ w_out, new_kv


def setup_inputs(seed: int = 0) -> dict:
    key = jax.random.key(seed)
    k = jax.random.split(key, 32)
    nrm = lambda kk, shape, scale: jax.random.normal(kk, shape, jnp.float32) * scale
    u = jax.random.uniform(k[16], (N_SSM_LAYERS, SSM_N_HEADS), jnp.float32)
    dt0 = jnp.exp(u * (jnp.log(0.1) - jnp.log(0.001)) + jnp.log(0.001))
    kv_shape = lambda w: (N_ATT_LAYERS, DEC_BATCH, min(w, PAST_LEN), 2, ATT_HEADS_PER_GROUP, ATT_HEAD_DIM)
    return {
        'x_prompt': nrm(k[0], (BATCH, SEQ, D_MODEL), 1.0),
        'x_sample': nrm(k[1], (DEC_BATCH, DEC_SEQ, D_MODEL), 1.0),
        'state_ssm': nrm(k[2], (N_SSM_LAYERS, DEC_BATCH, SSM_N_HEADS, SSM_HEAD_DIM, SSM_D_STATE), 0.1),
        'state_ssm_conv': nrm(k[3], (N_SSM_LAYERS, DEC_BATCH, SSM_CONV - 1, SSM_CONV_DIM), 1.0),
        'state_gla': nrm(k[4], (N_GLA_LAYERS, DEC_BATCH, GLA_N_HEADS, GLA_HEAD_K, GLA_HEAD_V), 0.1),
        'cache_kv_g0': nrm(k[5], kv_shape(ATT_GROUPS[0][0]), 1.0),
        'cache_kv_g1': nrm(k[6], kv_shape(ATT_GROUPS[1][0]), 1.0),
        'cache_kv_g2': nrm(k[7], kv_shape(ATT_GROUPS[2][0]), 1.0),
        'norm_mix': 1.0 + nrm(k[8], (DEPTH, D_MODEL), 0.02),
        'norm_ffn': 1.0 + nrm(k[9], (DEPTH, D_MODEL), 0.02),
        'ffn_gate': nrm(k[10], (DEPTH, D_MODEL, D_FF), D_MODEL ** -0.5),
        'ffn_up': nrm(k[11], (DEPTH, D_MODEL, D_FF), D_MODEL ** -0.5),
        'ffn_down': nrm(k[12], (DEPTH, D_FF, D_MODEL), D_FF ** -0.5),
        'ssm_w_in': nrm(k[13], (N_SSM_LAYERS, D_MODEL, SSM_IN_DIM), D_MODEL ** -0.5),
        'ssm_conv_w': nrm(k[14], (N_SSM_LAYERS, SSM_CONV, SSM_CONV_DIM), SSM_CONV ** -0.5),
        'ssm_conv_b': nrm(k[15], (N_SSM_LAYERS, SSM_CONV_DIM), 0.01),
        'ssm_dt_bias': dt0 + jnp.log(-jnp.expm1(-dt0)),
        'ssm_a_log': jnp.log(jax.random.uniform(k[17], (N_SSM_LAYERS, SSM_N_HEADS), jnp.float32, 1.0, 16.0)),
        'ssm_d': 1.0 + nrm(k[18], (N_SSM_LAYERS, SSM_N_HEADS), 0.1),
        'ssm_norm': 1.0 + nrm(k[19], (N_SSM_LAYERS, SSM_D_INNER), 0.02),
        'ssm_w_out': nrm(k[20], (N_SSM_LAYERS, SSM_D_INNER, D_MODEL), SSM_D_INNER ** -0.5),
        'gla_w_in': nrm(k[21], (N_GLA_LAYERS, D_MODEL, GLA_IN_DIM), D_MODEL ** -0.5),
        'gla_w_gate': nrm(k[22], (N_GLA_LAYERS, GLA_GATE_RANK, GLA_DK), GLA_GATE_RANK ** -0.5),
        'gla_gate_bias': nrm(k[23], (N_GLA_LAYERS, GLA_DK), 0.1),
        'gla_norm': 1.0 + nrm(k[24], (N_GLA_LAYERS, GLA_HEAD_V), 0.02),
        'gla_w_out': nrm(k[25], (N_GLA_LAYERS, GLA_DV, D_MODEL), GLA_DV ** -0.5),
        'att_w_qkv': nrm(k[26], (N_ATT_LAYERS, D_MODEL, 3 * ATT_N_HEADS * ATT_HEAD_DIM), D_MODEL ** -0.5),
        'att_w_out': nrm(k[27], (N_ATT_LAYERS, ATT_N_HEADS * ATT_HEAD_DIM, D_MODEL), (ATT_N_HEADS * ATT_HEAD_DIM) ** -0.5),
        'norm_final': 1.0 + nrm(k[28], (D_MODEL,), 0.02),
    }


def reference(x_prompt, x_sample, state_ssm, state_ssm_conv, state_gla, cache_kv_g0, cache_kv_g1, cache_kv_g2,
              norm_mix, norm_ffn, ffn_gate, ffn_up, ffn_down,
              ssm_w_in, ssm_conv_w, ssm_conv_b, ssm_dt_bias, ssm_a_log, ssm_d, ssm_norm, ssm_w_out,
              gla_w_in, gla_w_gate, gla_gate_bias, gla_norm, gla_w_out,
              att_w_qkv, att_w_out, norm_final):
    bp, lp = x_prompt.shape[:2]
    ls = x_sample.shape[1]
    pos_p = jnp.arange(lp)
    pos_s = PAST_LEN + jnp.arange(ls)
    xp, xs = x_prompt, x_sample
    att_caches = (cache_kv_g0, cache_kv_g1, cache_kv_g2)
    ssm_p, ssm_s, conv_p, conv_s, gla_p, gla_s = [], [], [], [], [], []
    kv_p, kv_s = [[], [], []], [[], [], []]
    for i in range(DEPTH):
        m, j = i % N_MIXERS, i // N_MIXERS
        hp = rmsnorm(xp, norm_mix[i])
        hs = rmsnorm(xs, norm_mix[i])
        if m == 0:
            w = (ssm_w_in[j], ssm_conv_w[j], ssm_conv_b[j], ssm_dt_bias[j], ssm_a_log[j], ssm_d[j], ssm_norm[j], ssm_w_out[j])
            conv0 = jnp.zeros((bp, SSM_CONV - 1, SSM_CONV_DIM), hp.dtype)
            h0 = jnp.zeros((bp, SSM_N_HEADS, SSM_HEAD_DIM, SSM_D_STATE), hp.dtype)
            op, cp, sp = mamba2_mixer(hp, conv0, h0, *w, chunk=min(SSM_CHUNK, lp))
            os_, cs, ss = mamba2_mixer(hs, state_ssm_conv[j], state_ssm[j], *w, chunk=ls)
            conv_p.append(cp); conv_s.append(cs); ssm_p.append(sp); ssm_s.append(ss)
        elif m == 1:
            w = (gla_w_in[j], gla_w_gate[j], gla_gate_bias[j], gla_norm[j], gla_w_out[j])
            s0 = jnp.zeros((bp, GLA_N_HEADS, GLA_HEAD_K, GLA_HEAD_V), hp.dtype)
            op, sp = gla_mixer(hp, s0, *w, chunk=min(GLA_CHUNK, lp))
            os_, ss = gla_mixer(hs, state_gla[j], *w, chunk=ls)
            gla_p.append(sp); gla_s.append(ss)
        else:
            op, nkp = dilated_attn_mixer(hp, pos_p, None, att_w_qkv[j], att_w_out[j])
            os_, nks = dilated_attn_mixer(hs, pos_s, [c[j] for c in att_caches], att_w_qkv[j], att_w_out[j])
            for gi in range(len(ATT_GROUPS)):
                kv_p[gi].append(nkp[gi]); kv_s[gi].append(nks[gi])
        xp = xp + op
        xs = xs + os_
        xp = xp + swiglu(rmsnorm(xp, norm_ffn[i]), ffn_gate[i], ffn_up[i], ffn_down[i])
        xs = xs + swiglu(rmsnorm(xs, norm_ffn[i]), ffn_gate[i], ffn_up[i], ffn_down[i])
    y_prompt = rmsnorm(xp, norm_final)
    y_sample = rmsnorm(xs, norm_final)
    return (y_prompt, y_sample,
            jnp.stack(ssm_p), jnp.stack(ssm_s), jnp.stack(conv_p), jnp.stack(conv_s),
            jnp.stack(gla_p), jnp.stack(gla_s),
            jnp.stack(kv_p[0]), jnp.stack(kv_s[0]), jnp.stack(kv_p[1]), jnp.stack(kv_s[1]),
            jnp.stack(kv_p[2]), jnp.stack(kv_s[2]))
```

```python
import functools

import jax
import jax.numpy as jnp
from jax import lax
from jax.experimental import pallas as pl
from jax.experimental.pallas import tpu as pltpu

F32 = jnp.float32
BF16 = jnp.bfloat16

NORM_EPS = 1e-6
N_MIXERS = 3

SSM_HEAD_DIM = 64
SSM_N_GROUPS = 4
SSM_D_STATE = 128
SSM_CONV = 4
SSM_CHUNK = 128

GLA_N_HEADS = 4
GLA_GATE_NORM = 16.0
GLA_CHUNK = 32

ATT_GROUPS = ((128, 1), (512, 4), (2048, 16))
ATT_HEADS_PER_GROUP = 4
ATT_HEAD_DIM = 64
ATT_ROT_DIM = ATT_HEAD_DIM // 4
ROPE_THETA = 500000.0
ATT_BLOCK = 128
PAST_LEN = 8192

LANES = 128
SUBLANES = 8
SAMPLE_PAD = 16
VMEM_LIMIT = 56 * 1024 * 1024

_NT = (((1,), (1,)), ((), ()))
_TN = (((0,), (0,)), ((), ()))


def _cparams(*sem):
    return pltpu.CompilerParams(dimension_semantics=sem, vmem_limit_bytes=VMEM_LIMIT)


def _resident(shape):
    zeros = (0,) * len(shape)
    return pl.BlockSpec(shape, lambda *_: zeros, pipeline_mode=pl.Buffered(1))


def _rms(x, g):
    inv = lax.rsqrt(jnp.mean(x * x, axis=-1, keepdims=True) + NORM_EPS)
    return x * inv * g


def _sigmoid(x):
    return 1.0 / (1.0 + jnp.exp(-x))


def _softplus(x):
    return jnp.maximum(x, 0.0) + jnp.log1p(jnp.exp(-jnp.abs(x)))


def _split3(v):
    hi = v.astype(BF16)
    r = v - hi.astype(F32)
    mid = r.astype(BF16)
    lo = (r - mid.astype(F32)).astype(BF16)
    return hi, mid, lo


def _dot(a, b):
    return jnp.dot(a, b, preferred_element_type=F32)


def _block_selector(n_blocks, width):
    col = lax.broadcasted_iota(jnp.int32, (n_blocks, n_blocks * width), 1)
    lo = lax.broadcasted_iota(jnp.int32, (n_blocks, n_blocks * width), 0) * width
    return jnp.where((col >= lo) & (col < lo + width), 1.0, 0.0).astype(BF16)


def _sel_right(v, e):
    hi, mid, lo = _split3(v)
    return _dot(hi, e) + _dot(mid, e) + _dot(lo, e)


def _sel_left(e, v):
    hi, mid, lo = _split3(v)
    return _dot(e, hi) + _dot(e, mid) + _dot(e, lo)


def _norm_matmul_kernel(x_ref, g_ref, w_ref, o_ref):
    h = _rms(x_ref[...], g_ref[...]).astype(BF16)
    o_ref[...] = _dot(h, w_ref[...])


def _norm_matmul_rope_kernel(x_ref, g_ref, w_ref, c_ref, s1_ref, s2_ref, o_ref, *, n_rot):
    h = _rms(x_ref[...], g_ref[...]).astype(BF16)
    o_ref[...] = _dot(h, w_ref[...])
    c, s1, s2 = c_ref[...], s1_ref[...], s2_ref[...]
    half = ATT_ROT_DIM // 2
    for j in range(n_rot // LANES):
        sl = slice(j * LANES, (j + 1) * LANES)
        x = o_ref[:, sl]
        o_ref[:, sl] = (x * c + pltpu.roll(x, half, 1) * s1
                        + pltpu.roll(x, LANES - half, 1) * s2)


def _norm_matmul(x, g, w, *, tm, rope=None):
    m, d = x.shape
    n = w.shape[1]
    assert m % tm == 0
    in_specs = [pl.BlockSpec((tm, d), lambda i: (i, 0)), _resident((1, d)), _resident((d, n))]
    args = [x, g.reshape(1, d), w]
    if rope is None:
        body = _norm_matmul_kernel
    else:
        n_rot, tables = rope
        body = functools.partial(_norm_matmul_rope_kernel, n_rot=n_rot)
        in_specs += [pl.BlockSpec((tm, LANES), lambda i: (i, 0))] * 3
        args += list(tables)
    return pl.pallas_call(
        body,
        grid=(m // tm,),
        in_specs=in_specs,
        out_specs=pl.BlockSpec((tm, n), lambda i: (i, 0)),
        out_shape=jax.ShapeDtypeStruct((m, n), F32),
        compiler_params=_cparams("parallel"),
        name="norm_proj" if rope is None else "norm_proj_rope",
    )(*args)


def _proj_res_kernel(y_ref, w_ref, x_ref, o_ref):
    o_ref[...] = x_ref[...] + _dot(y_ref[...], w_ref[...])


def _proj_res(y, w, x, *, tm):
    m, k = y.shape
    d = w.shape[1]
    assert m % tm == 0
    return pl.pallas_call(
        _proj_res_kernel,
        grid=(m // tm,),
        in_specs=[pl.BlockSpec((tm, k), lambda i: (i, 0)), _resident((k, d)),
                  pl.BlockSpec((tm, d), lambda i: (i, 0))],
        out_specs=pl.BlockSpec((tm, d), lambda i: (i, 0)),
        out_shape=jax.ShapeDtypeStruct((m, d), F32),
        compiler_params=_cparams("parallel"),
        name="proj_res",
    )(y, w, x)


def _ffn_kernel(x_ref, g_ref, wg_ref, wu_ref, wd_ref, o_ref):
    x = x_ref[...]
    h = _rms(x, g_ref[...]).astype(BF16)
    gate = _dot(h, wg_ref[...])
    up = _dot(h, wu_ref[...])
    act = (gate * _sigmoid(gate) * up).astype(BF16)
    o_ref[...] = x + _dot(act, wd_ref[...])


def _ffn(x, g, wg, wu, wd, *, tm):
    m, d = x.shape
    f = wg.shape[1]
    assert m % tm == 0
    return pl.pallas_call(
        _ffn_kernel,
        grid=(m // tm,),
        in_specs=[pl.BlockSpec((tm, d), lambda i: (i, 0)), _resident((1, d)),
                  _resident((d, f)), _resident((d, f)), _resident((f, d))],
        out_specs=pl.BlockSpec((tm, d), lambda i: (i, 0)),
        out_shape=jax.ShapeDtypeStruct((m, d), F32),
        compiler_params=_cparams("parallel"),
        name="swiglu",
    )(x, g.reshape(1, d), wg, wu, wd)


def _rmsnorm_kernel(x_ref, g_ref, o_ref):
    o_ref[...] = _rms(x_ref[...], g_ref[...])


def _rmsnorm(x, g, *, tm):
    m, d = x.shape
    return pl.pallas_call(
        _rmsnorm_kernel,
        grid=(m // tm,),
        in_specs=[pl.BlockSpec((tm, d), lambda i: (i, 0)), _resident((1, d))],
        out_specs=pl.BlockSpec((tm, d), lambda i: (i, 0)),
        out_shape=jax.ShapeDtypeStruct((m, d), F32),
        compiler_params=_cparams("parallel"),
        name="final_norm",
    )(x, g.reshape(1, d))


def _ssd_kernel(zx_ref, cbuf_ref, h0_ref, cw_ref, cb_ref, dtb_ref, alog_ref, dsk_ref, nw_ref,
                y_ref, hfin_ref, xpad, state, yacc, *, t, valid, d_inner, n_heads):
    c = pl.program_id(1)
    n_grp = SSM_N_GROUPS
    d_st = SSM_D_STATE
    hp = d_inner // n_grp
    hpg = n_heads // n_grp
    pdim = SSM_HEAD_DIM
    bc_w = n_grp * d_st
    conv_dim = d_inner + 2 * bc_w

    @pl.when(c == 0)
    def _init():
        state[...] = h0_ref[0]
        xpad[0:SUBLANES, :] = cbuf_ref[0]

    xraw = zx_ref[0, :, d_inner:d_inner + conv_dim]
    xpad[SUBLANES:SUBLANES + t, :] = xraw
    cw = cw_ref[...]
    acc = cb_ref[...]
    for k in range(SSM_CONV - 1):
        off = SUBLANES - (SSM_CONV - 1) + k
        acc = acc + xpad[off:off + t, :] * cw[k:k + 1, :]
    acc = acc + xraw * cw[SSM_CONV - 1:SSM_CONV, :]
    xpad[0:SUBLANES, :] = xpad[t:t + SUBLANES, :]
    xbc = acc * _sigmoid(acc)
    xs = xbc[:, :d_inner]
    bm = xbc[:, d_inner:d_inner + bc_w]
    cm = xbc[:, d_inner + bc_w:]

    dt = _softplus(zx_ref[0, :, d_inner + conv_dim:] + dtb_ref[...])
    if valid < t:
        dt = jnp.where(lax.broadcasted_iota(jnp.int32, dt.shape, 0) < valid, dt, 0.0)
    a = dt * (-jnp.exp(alog_ref[...]))

    iq = lax.broadcasted_iota(jnp.int32, (t, t), 0)
    ik = lax.broadcasted_iota(jnp.int32, (t, t), 1)
    low = iq >= ik
    tri = jnp.where(low, 1.0, 0.0).astype(BF16)
    a_cs = _sel_left(tri, a)

    e_hp = _block_selector(n_heads, pdim)
    e_hk = _block_selector(n_heads, t)
    a_last = a_cs[t - 1:t, :]
    stack = jnp.concatenate([dt, jnp.exp(a_cs), jnp.exp(a_last - a_cs)], axis=0)
    ex = _sel_right(stack, e_hp)
    dt_e, ea_e, dst_e = ex[0:t], ex[t:2 * t], ex[2 * t:3 * t]
    xdt = xs * dt_e
    xdt_b = xdt.astype(BF16)
    xdst_b = (xdt * dst_e).astype(BF16)

    a_col = _sel_right(a_cs, e_hk)
    diag = (lax.broadcasted_iota(jnp.int32, (t, n_heads * t), 1) & (t - 1)
            ) == lax.broadcasted_iota(jnp.int32, (t, n_heads * t), 0)
    a_row = jnp.sum(jnp.where(diag, a_col, 0.0), axis=0, keepdims=True)

    for g in range(n_grp):
        bg = bm[:, g * d_st:(g + 1) * d_st].astype(BF16)
        cg = cm[:, g * d_st:(g + 1) * d_st].astype(BF16)
        cb = lax.dot_general(cg, bg, _NT, preferred_element_type=F32)
        sg = state[g]
        gl = slice(g * hp, (g + 1) * hp)
        yacc[:, gl] = _dot(cg, sg.astype(BF16)) * ea_e[:, gl]
        for h in range(hpg):
            hh = g * hpg + h
            ks = slice(hh * t, (hh + 1) * t)
            ps = slice(hh * pdim, (hh + 1) * pdim)
            decay = jnp.exp(jnp.where(low, a_col[:, ks] - a_row[:, ks], -jnp.inf))
            yacc[:, ps] += _dot((cb * decay).astype(BF16), xdt_b[:, ps])
        upd = lax.dot_general(bg, xdst_b[:, gl], _TN, preferred_element_type=F32)
        state[g] = sg * ea_e[t - 1:t, gl] + upd

    z = zx_ref[0, :, :d_inner]
    y = (yacc[...] + xs * dsk_ref[...]) * (z * _sigmoid(z))
    nw = nw_ref[...]
    for g in range(n_grp):
        gl = slice(g * hp, (g + 1) * hp)
        y_ref[0, :, gl] = _rms(y[:, gl], nw[:, gl]).astype(y_ref.dtype)

    @pl.when(c == pl.num_programs(1) - 1)
    def _fin():
        hfin_ref[0] = state[...]


def _ssd(zx, conv_buf, h0, conv_w, conv_b, dt_bias, a_log, d_skip, norm_w, *, t, valid):
    b, l, in_dim = zx.shape
    n_heads = a_log.shape[0]
    d_inner = n_heads * SSM_HEAD_DIM
    conv_dim = conv_w.shape[1]
    n_grp, d_st = SSM_N_GROUPS, SSM_D_STATE
    hp = d_inner // n_grp
    assert l % t == 0 and in_dim == d_inner + conv_dim + n_heads
    cbuf = jnp.pad(conv_buf, ((0, 0), (SUBLANES - (SSM_CONV - 1), 0), (0, 0)))
    h0t = h0.reshape(b, n_grp, n_heads // n_grp, SSM_HEAD_DIM, d_st)
    h0t = h0t.transpose(0, 1, 4, 2, 3).reshape(b, n_grp, d_st, hp)
    body = functools.partial(_ssd_kernel, t=t, valid=valid, d_inner=d_inner, n_heads=n_heads)
    y, hfin = pl.pallas_call(
        body,
        grid=(b, l // t),
        in_specs=[
            pl.BlockSpec((1, t, in_dim), lambda i, c: (i, c, 0)),
            pl.BlockSpec((1, SUBLANES, conv_dim), lambda i, c: (i, 0, 0)),
            pl.BlockSpec((1, n_grp, d_st, hp), lambda i, c: (i, 0, 0, 0)),
            _resident((SSM_CONV, conv_dim)), _resident((1, conv_dim)),
            _resident((1, n_heads)), _resident((1, n_heads)),
            _resident((1, d_inner)), _resident((1, d_inner)),
        ],
        out_specs=[
            pl.BlockSpec((1, t, d_inner), lambda i, c: (i, c, 0)),
            pl.BlockSpec((1, n_grp, d_st, hp), lambda i, c: (i, 0, 0, 0)),
        ],
        out_shape=[
            jax.ShapeDtypeStruct((b, l, d_inner), BF16),
            jax.ShapeDtypeStruct((b, n_grp, d_st, hp), F32),
        ],
        scratch_shapes=[
            pltpu.VMEM((t + SUBLANES, conv_dim), F32),
            pltpu.VMEM((n_grp, d_st, hp), F32),
            pltpu.VMEM((t, d_inner), F32),
        ],
        compiler_params=_cparams("parallel", "arbitrary"),
        name="ssd_scan",
    )(zx, cbuf, h0t, conv_w, conv_b.reshape(1, -1), dt_bias.reshape(1, -1), a_log.reshape(1, -1),
      jnp.repeat(d_skip, SSM_HEAD_DIM).reshape(1, -1), norm_w.reshape(1, -1))
    hfin = hfin.reshape(b, n_grp, d_st, n_heads // n_grp, SSM_HEAD_DIM)
    hfin = hfin.transpose(0, 1, 3, 4, 2).reshape(b, n_heads, SSM_HEAD_DIM, d_st)
    return y, hfin


def _gla_kernel(p_ref, s0_ref, wg_ref, gb_ref, nw_ref, o_ref, sfin_ref, state, oacc,
                *, tb, ch, valid, dk, dv):
    c = pl.program_id(1)
    n_h = GLA_N_HEADS
    hk, hv = dk // n_h, dv // n_h

    @pl.when(c == 0)
    def _init():
        state[...] = s0_ref[0]

    q = p_ref[0, :, 0:dk] * (hk ** -0.5)
    k = p_ref[0, :, dk:2 * dk]
    v = p_ref[0, :, 2 * dk:2 * dk + dv]
    r = p_ref[0, :, 2 * dk + dv:2 * dk + 2 * dv]
    g_low = p_ref[0, :, 2 * dk + 2 * dv:]
    x = _dot(g_low.astype(BF16), wg_ref[...]) + gb_ref[...]
    log_a = -_softplus(-x) / GLA_GATE_NORM
    if valid < tb:
        keep = lax.broadcasted_iota(jnp.int32, (tb, dk), 0) < valid
        log_a = jnp.where(keep, log_a, 0.0)
        k = jnp.where(keep, k, 0.0)

    ir = lax.broadcasted_iota(jnp.int32, (tb, tb), 0)
    ic = lax.broadcasted_iota(jnp.int32, (tb, tb), 1)
    causal = ir >= ic
    blocktri = jnp.where(causal & ((ir & -ch) == (ic & -ch)), 1.0, 0.0).astype(BF16)
    bcum = _sel_left(blocktri, log_a)
    q_t = (q * jnp.exp(bcum)).astype(BF16)
    k_t = (k * jnp.exp(-bcum)).astype(BF16)
    v_b = v.astype(BF16)
    causal_c = causal[0:ch, 0:ch]

    for j in range(tb // ch):
        rows = slice(j * ch, (j + 1) * ch)
        b_last = bcum[(j + 1) * ch - 1:(j + 1) * ch, :]
        k_dec = (k[rows] * jnp.exp(b_last - bcum[rows])).astype(BF16)
        e_last = jnp.exp(b_last)
        for h in range(n_h):
            ks = slice(h * hk, (h + 1) * hk)
            vs = slice(h * hv, (h + 1) * hv)
            qh, kh, vh = q_t[rows, ks], k_t[rows, ks], v_b[rows, vs]
            att = lax.dot_general(qh, kh, _NT, preferred_element_type=F32)
            att = jnp.where(causal_c, att, 0.0).astype(BF16)
            sh = state[h]
            oacc[rows, vs] = _dot(att, vh) + lax.dot_general(qh, sh.astype(BF16), _NT,
                                                           preferred_element_type=F32)
            upd = lax.dot_general(vh, k_dec[:, ks], _TN, preferred_element_type=F32)
            state[h] = sh * e_last[:, ks] + upd

    gate = r * _sigmoid(r)
    nw = nw_ref[...]
    for h in range(n_h):
        vs = slice(h * hv, (h + 1) * hv)
        o_ref[0, :, vs] = (_rms(oacc[:, vs], nw) * gate[:, vs]).astype(o_ref.dtype)

    @pl.when(c == pl.num_programs(1) - 1)
    def _fin():
        sfin_ref[0] = state[...]


def _gla(proj, s0, w_gate, gate_bias, norm_w, *, tb, ch, valid):
    b, l, in_dim = proj.shape
    n_h = GLA_N_HEADS
    rank, dk = w_gate.shape
    hv = norm_w.shape[0]
    dv = hv * n_h
    hk = dk // n_h
    assert l % tb == 0 and tb % ch == 0 and in_dim == 2 * dk + 2 * dv + rank
    body = functools.partial(_gla_kernel, tb=tb, ch=ch, valid=valid, dk=dk, dv=dv)
    o, sfin = pl.pallas_call(
        body,
        grid=(b, l // tb),
        in_specs=[
            pl.BlockSpec((1, tb, in_dim), lambda i, c: (i, c, 0)),
            pl.BlockSpec((1, n_h, hv, hk), lambda i, c: (i, 0, 0, 0)),
            _resident((rank, dk)), _resident((1, dk)), _resident((1, hv)),
        ],
        out_specs=[
            pl.BlockSpec((1, tb, dv), lambda i, c: (i, c, 0)),
            pl.BlockSpec((1, n_h, hv, hk), lambda i, c: (i, 0, 0, 0)),
        ],
        out_shape=[
            jax.ShapeDtypeStruct((b, l, dv), BF16),
            jax.ShapeDtypeStruct((b, n_h, hv, hk), F32),
        ],
        scratch_shapes=[pltpu.VMEM((n_h, hv, hk), F32), pltpu.VMEM((tb, dv), F32)],
        compiler_params=_cparams("parallel", "arbitrary"),
        name="gla_scan",
    )(proj, jnp.swapaxes(s0, -1, -2), w_gate.astype(BF16), gate_bias.reshape(1, -1),
      norm_w.reshape(1, -1))
    return o, jnp.swapaxes(sfin, -1, -2)


def _attn_prompt_kernel(q_ref, kp_ref, kc_ref, vp_ref, vc_ref, o_ref, lse_ref, *, span):
    i = pl.program_id(1)
    blk = ATT_BLOCK
    dh = ATT_HEAD_DIM
    iq = lax.broadcasted_iota(jnp.int32, (blk, blk), 0)
    ik = lax.broadcasted_iota(jnp.int32, (blk, blk), 1)
    ok_prev = (ik >= iq + (blk - span)) & (i > 0)
    ok_cur = ik <= iq
    scale = dh ** -0.5
    for h in range(ATT_HEADS_PER_GROUP):
        hs = slice(h * dh, (h + 1) * dh)
        qh = q_ref[0, :, hs].astype(BF16)
        s_p = lax.dot_general(qh, kp_ref[0, :, hs].astype(BF16), _NT, preferred_element_type=F32) * scale
        s_c = lax.dot_general(qh, kc_ref[0, :, hs].astype(BF16), _NT, preferred_element_type=F32) * scale
        s_p = jnp.where(ok_prev, s_p, -jnp.inf)
        s_c = jnp.where(ok_cur, s_c, -jnp.inf)
        mx = jnp.maximum(jnp.max(s_p, axis=-1, keepdims=True), jnp.max(s_c, axis=-1, keepdims=True))
        p_p = jnp.exp(s_p - mx)
        p_c = jnp.exp(s_c - mx)
        den = jnp.sum(p_p, axis=-1, keepdims=True) + jnp.sum(p_c, axis=-1, keepdims=True)
        o = (_dot((p_p / den).astype(BF16), vp_ref[0, :, hs].astype(BF16))
             + _dot((p_c / den).astype(BF16), vc_ref[0, :, hs].astype(BF16)))
        o_ref[0, :, hs] = o
        lse_ref[0, :, hs] = jnp.broadcast_to(mx + jnp.log(den), (blk, dh))


def _attn_prompt(q, k, v, *, window, dilation):
    b, l, w = q.shape
    m = l // dilation
    span = window // dilation
    blk = ATT_BLOCK
    assert l % dilation == 0 and m % blk == 0 and span <= blk

    def by_residue(t):
        return t.reshape(b, m, dilation, w).transpose(0, 2, 1, 3).reshape(b * dilation, m, w)

    def from_residue(t):
        return t.reshape(b, dilation, m, w).transpose(0, 2, 1, 3).reshape(b, l, w)

    qs, ks, vs = (by_residue(t) for t in (q, k, v)) if dilation > 1 else (q, k, v)
    cur = pl.BlockSpec((1, blk, w), lambda n, i: (n, i, 0))
    prev = pl.BlockSpec((1, blk, w), lambda n, i: (n, jnp.maximum(i - 1, 0), 0))
    o, lse = pl.pallas_call(
        functools.partial(_attn_prompt_kernel, span=span),
        grid=(b * dilation, m // blk),
        in_specs=[cur, prev, cur, prev, cur],
        out_specs=[cur, cur],
        out_shape=[jax.ShapeDtypeStruct(qs.shape, F32)] * 2,
        compiler_params=_cparams("parallel", "parallel"),
        name="attn_prompt",
    )(qs, ks, ks, vs, vs)
    if dilation > 1:
        o, lse = from_residue(o), from_residue(lse)
    return o, lse


def _attn_sample_kernel(q_ref, new_ref, cache_ref, o_ref, lse_ref, kvx, *, window, dilation, n_new):
    dh = ATT_HEAD_DIM
    hg = ATT_HEADS_PER_GROUP
    w = hg * dh
    span = window // dilation
    rows_pad = q_ref.shape[1]
    n_slab = 2 * w // LANES
    for j in range(n_slab):
        kvx[j, 0:window, :] = cache_ref[0, :, j * LANES:(j + 1) * LANES]
        kvx[j, window:window + rows_pad, :] = new_ref[0, :, j * LANES:(j + 1) * LANES]

    def rows_of(row_idx):
        return jnp.concatenate([kvx[j, row_idx, :] for j in range(n_slab)], axis=-1)

    lane = lax.broadcasted_iota(jnp.int32, (SUBLANES, w), 1)
    lane0 = lax.broadcasted_iota(jnp.int32, (SUBLANES, w), 0) * dh
    mine = (lane >= lane0) & (lane < lane0 + dh)
    scale = dh ** -0.5
    o_ref[...] = jnp.zeros_like(o_ref)
    lse_ref[...] = jnp.zeros_like(lse_ref)
    for s in range(n_new):
        kv = rows_of(pl.ds(s, span) if dilation == 1 else pl.ds(s, span, stride=dilation))
        own = rows_of(pl.ds(window + s, 1))
        qm = jnp.where(mine, q_ref[0, s:s + 1, :], 0.0)
        sc = lax.dot_general(qm.astype(BF16), kv[:, :w].astype(BF16), _NT,
                             preferred_element_type=F32) * scale
        sc_own = jnp.sum(qm * own[:, :w], axis=-1, keepdims=True) * scale
        mx = jnp.maximum(jnp.max(sc, axis=-1, keepdims=True), sc_own)
        p = jnp.exp(sc - mx)
        p_own = jnp.exp(sc_own - mx)
        den = jnp.sum(p, axis=-1, keepdims=True) + p_own
        o = (_dot((p / den).astype(BF16), kv[:, w:].astype(BF16))
             + (p_own / den) * own[:, w:])
        lse = jnp.broadcast_to(mx + jnp.log(den), (SUBLANES, w))
        o_ref[0, s:s + 1, :] = jnp.sum(jnp.where(mine, o, 0.0), axis=0, keepdims=True)
        lse_ref[0, s:s + 1, :] = jnp.sum(jnp.where(mine, lse, 0.0), axis=0, keepdims=True)


def _attn_sample(q, kv_new, cache, *, window, dilation, n_new):
    b, rows_pad, w = q.shape
    assert cache.shape[1] == window and window == dilation * (window // dilation)
    assert (n_new - 1) + dilation * (window // dilation - 1) < window + n_new
    return pl.pallas_call(
        functools.partial(_attn_sample_kernel, window=window, dilation=dilation, n_new=n_new),
        grid=(b,),
        in_specs=[pl.BlockSpec((1, rows_pad, w), lambda i: (i, 0, 0)),
                  pl.BlockSpec((1, rows_pad, 2 * w), lambda i: (i, 0, 0)),
                  pl.BlockSpec((1, window, 2 * w), lambda i: (i, 0, 0))],
        out_specs=[pl.BlockSpec((1, rows_pad, w), lambda i: (i, 0, 0))] * 2,
        out_shape=[jax.ShapeDtypeStruct((b, rows_pad, w), F32)] * 2,
        scratch_shapes=[pltpu.VMEM((2 * w // LANES, window + rows_pad, LANES), F32)],
        compiler_params=_cparams("parallel"),
        name="attn_sample",
    )(q, kv_new, cache)


def _attn_out_kernel(o0_ref, o1_ref, o2_ref, l0_ref, l1_ref, l2_ref, w_ref, x_ref, out_ref):
    ls = [l0_ref[...], l1_ref[...], l2_ref[...]]
    mx = jnp.maximum(jnp.maximum(ls[0], ls[1]), ls[2])
    es = [jnp.exp(l - mx) for l in ls]
    den = es[0] + es[1] + es[2]
    mixed = jnp.concatenate([o_ref[...] * (e / den) for o_ref, e in zip((o0_ref, o1_ref, o2_ref), es)],
                            axis=-1).astype(BF16)
    out_ref[...] = x_ref[...] + _dot(mixed, w_ref[...])


def _attn_out(os_, lses, w, x, *, tm):
    m, d = x.shape
    wg = os_[0].shape[1]
    row = pl.BlockSpec((tm, wg), lambda i: (i, 0))
    return pl.pallas_call(
        _attn_out_kernel,
        grid=(m // tm,),
        in_specs=[row] * 6 + [_resident(w.shape), pl.BlockSpec((tm, d), lambda i: (i, 0))],
        out_specs=pl.BlockSpec((tm, d), lambda i: (i, 0)),
        out_shape=jax.ShapeDtypeStruct((m, d), F32),
        compiler_params=_cparams("parallel"),
        name="attn_mix_proj",
    )(*os_, *lses, w, x)


def _rope_tables(pos):
    half = ATT_ROT_DIM // 2
    inv_freq = ROPE_THETA ** (-jnp.arange(0, ATT_ROT_DIM, 2, dtype=F32) / ATT_ROT_DIM)
    ang = pos.astype(F32)[:, None] * inv_freq
    cos, sin = jnp.cos(ang), jnp.sin(ang)
    n = pos.shape[0]
    rest = ATT_HEAD_DIM - ATT_ROT_DIM
    one, zero, zh = jnp.ones((n, rest), F32), jnp.zeros((n, rest), F32), jnp.zeros((n, half), F32)
    reps = LANES // ATT_HEAD_DIM
    c = jnp.tile(jnp.concatenate([cos, cos, one], axis=1), (1, reps))
    s1 = jnp.tile(jnp.concatenate([zh, sin, zero], axis=1), (1, reps))
    s2 = jnp.tile(jnp.concatenate([-sin, zh, zero], axis=1), (1, reps))
    return c, s1, s2


def _pad_rows(t, rows):
    return jnp.pad(t, ((0, 0), (0, rows - t.shape[1]), (0, 0)))


def kernel(x_prompt, x_sample, state_ssm, state_ssm_conv, state_gla, cache_kv_g0, cache_kv_g1, cache_kv_g2,
           norm_mix, norm_ffn, ffn_gate, ffn_up, ffn_down,
           ssm_w_in, ssm_conv_w, ssm_conv_b, ssm_dt_bias, ssm_a_log, ssm_d, ssm_norm, ssm_w_out,
           gla_w_in, gla_w_gate, gla_gate_bias, gla_norm, gla_w_out,
           att_w_qkv, att_w_out, norm_final):
    bp, lp, d = x_prompt.shape
    bs, ls, _ = x_sample.shape
    depth = norm_mix.shape[0]
    mp, ms = bp * lp, bs * ls
    tm_p = 512 if mp % 512 == 0 else mp
    tm_s = ms
    xp = x_prompt.reshape(mp, d)
    xs = x_sample.reshape(ms, d)
    att_caches = (cache_kv_g0, cache_kv_g1, cache_kv_g2)
    hg_w = ATT_HEADS_PER_GROUP * ATT_HEAD_DIM
    n_att = hg_w * len(ATT_GROUPS)
    rope_p = _rope_tables(jnp.tile(jnp.arange(lp), bp))
    rope_s = _rope_tables(jnp.tile(PAST_LEN + jnp.arange(ls), bs))

    ssm_p, ssm_s, conv_p, conv_s, gla_p, gla_s = [], [], [], [], [], []
    kv_p, kv_s = [[], [], []], [[], [], []]
    for i in range(depth):
        m, j = i % N_MIXERS, i // N_MIXERS
        if m == 0:
            w_in = ssm_w_in[j].astype(BF16)
            conv_dim = ssm_conv_w.shape[2]
            d_inner = ssm_w_out.shape[1]
            wts = (ssm_conv_w[j], ssm_conv_b[j], ssm_dt_bias[j], ssm_a_log[j], ssm_d[j], ssm_norm[j])
            zx_p = _norm_matmul(xp, norm_mix[i], w_in, tm=tm_p).reshape(bp, lp, -1)
            zx_s = _norm_matmul(xs, norm_mix[i], w_in, tm=tm_s).reshape(bs, ls, -1)
            t_p = min(SSM_CHUNK, lp)
            y_p, h_p = _ssd(zx_p, jnp.zeros((bp, SSM_CONV - 1, conv_dim), F32),
                            jnp.zeros((bp,) + state_ssm.shape[2:], F32), *wts, t=t_p, valid=t_p)
            y_s, h_s = _ssd(_pad_rows(zx_s, SAMPLE_PAD), state_ssm_conv[j], state_ssm[j], *wts,
                            t=SAMPLE_PAD, valid=ls)
            y_s = y_s[:, :ls]
            xbc_p = zx_p[:, :, d_inner:d_inner + conv_dim]
            xbc_s = zx_s[:, :, d_inner:d_inner + conv_dim]
            conv_p.append(xbc_p[:, -(SSM_CONV - 1):])
            conv_s.append(jnp.concatenate([state_ssm_conv[j], xbc_s], axis=1)[:, -(SSM_CONV - 1):])
            ssm_p.append(h_p)
            ssm_s.append(h_s)
            w_out = ssm_w_out[j].astype(BF16)
            xp = _proj_res(y_p.reshape(mp, -1), w_out, xp, tm=tm_p)
            xs = _proj_res(y_s.reshape(ms, -1), w_out, xs, tm=tm_s)
        elif m == 1:
            w_in = gla_w_in[j].astype(BF16)
            wts = (gla_w_gate[j], gla_gate_bias[j], gla_norm[j])
            pr_p = _norm_matmul(xp, norm_mix[i], w_in, tm=tm_p).reshape(bp, lp, -1)
            pr_s = _norm_matmul(xs, norm_mix[i], w_in, tm=tm_s).reshape(bs, ls, -1)
            ch_p = min(GLA_CHUNK, lp)
            tb_p = 256 if lp % 256 == 0 else ch_p
            o_p, s_p = _gla(pr_p, jnp.zeros((bp,) + state_gla.shape[2:], F32), *wts,
                            tb=tb_p, ch=ch_p, valid=tb_p)
            o_s, s_s = _gla(_pad_rows(pr_s, SAMPLE_PAD), state_gla[j], *wts,
                            tb=SAMPLE_PAD, ch=SAMPLE_PAD, valid=ls)
            o_s = o_s[:, :ls]
            gla_p.append(s_p)
            gla_s.append(s_s)
            w_out = gla_w_out[j].astype(BF16)
            xp = _proj_res(o_p.reshape(mp, -1), w_out, xp, tm=tm_p)
            xs = _proj_res(o_s.reshape(ms, -1), w_out, xs, tm=tm_s)
        else:
            w_qkv = att_w_qkv[j].astype(BF16)
            qkv_p = _norm_matmul(xp, norm_mix[i], w_qkv, tm=tm_p, rope=(2 * n_att, rope_p)).reshape(bp, lp, -1)
            qkv_s = _norm_matmul(xs, norm_mix[i], w_qkv, tm=tm_s, rope=(2 * n_att, rope_s)).reshape(bs, ls, -1)
            os_p, ls_p, os_s, ls_s = [], [], [], []
            for gi, (window, dilation) in enumerate(ATT_GROUPS):
                cols = [slice(part * n_att + gi * hg_w, part * n_att + (gi + 1) * hg_w) for part in range(3)]
                q_p, k_p, v_p = (qkv_p[:, :, cs] for cs in cols)
                o, lse = _attn_prompt(q_p, k_p, v_p, window=window, dilation=dilation)
                os_p.append(o.reshape(mp, hg_w))
                ls_p.append(lse.reshape(mp, hg_w))
                keep = min(window, lp)
                kv_p[gi].append(jnp.stack([k_p[:, -keep:], v_p[:, -keep:]], axis=2)
                                .reshape(bp, keep, 2, ATT_HEADS_PER_GROUP, ATT_HEAD_DIM))
                q_s, k_s, v_s = (qkv_s[:, :, cs] for cs in cols)
                kv_new = jnp.concatenate([k_s, v_s], axis=-1)
                cache = att_caches[gi][j]
                cache2 = cache.reshape(bs, cache.shape[1], 2 * hg_w)
                o, lse = _attn_sample(_pad_rows(q_s, SUBLANES), _pad_rows(kv_new, SUBLANES), cache2,
                                      window=window, dilation=dilation, n_new=ls)
                os_s.append(o[:, :ls].reshape(ms, hg_w))
                ls_s.append(lse[:, :ls].reshape(ms, hg_w))
                kvx = jnp.concatenate([cache2, kv_new], axis=1)
                keep = min(window, kvx.shape[1])
                kv_s[gi].append(kvx[:, -keep:].reshape(bs, keep, 2, ATT_HEADS_PER_GROUP, ATT_HEAD_DIM))
            w_out = att_w_out[j].astype(BF16)
            xp = _attn_out(os_p, ls_p, w_out, xp, tm=tm_p)
            xs = _attn_out(os_s, ls_s, w_out, xs, tm=tm_s)
        wg, wu, wd = ffn_gate[i].astype(BF16), ffn_up[i].astype(BF16), ffn_down[i].astype(BF16)
        xp = _ffn(xp, norm_ffn[i], wg, wu, wd, tm=tm_p)
        xs = _ffn(xs, norm_ffn[i], wg, wu, wd, tm=tm_s)
    y_prompt = _rmsnorm(xp, norm_final, tm=tm_p).reshape(bp, lp, d)
    y_sample = _rmsnorm(xs, norm_final, tm=tm_s).reshape(bs, ls, d)
    return (y_prompt, y_sample,
            jnp.stack(ssm_p), jnp.stack(ssm_s), jnp.stack(conv_p), jnp.stack(conv_s),
            jnp.stack(gla_p), jnp.stack(gla_s),
            jnp.stack(kv_p[0]), jnp.stack(kv_s[0]), jnp.stack(kv_p[1]), jnp.stack(kv_s[1]),
            jnp.stack(kv_p[2]), jnp.stack(kv_s[2]))
```

```python
import functools

import jax
import jax.numpy as jnp
from jax import lax
from jax.experimental import pallas as pl
from jax.experimental.pallas import tpu as pltpu

F32 = jnp.float32
BF16 = jnp.bfloat16

NORM_EPS = 1e-6
N_MIXERS = 3

SSM_HEAD_DIM = 64
SSM_N_GROUPS = 4
SSM_D_STATE = 128
SSM_CONV = 4
SSM_CHUNK = 128

GLA_N_HEADS = 4
GLA_GATE_NORM = 16.0
GLA_CHUNK = 32

ATT_GROUPS = ((128, 1), (512, 4), (2048, 16))
ATT_HEADS_PER_GROUP = 4
ATT_HEAD_DIM = 64
ATT_ROT_DIM = ATT_HEAD_DIM // 4
ROPE_THETA = 500000.0
ATT_BLOCK = 128
PAST_LEN = 8192

LANES = 128
SUBLANES = 8
SAMPLE_PAD = 16
VMEM_LIMIT = 56 * 1024 * 1024

_NT = (((1,), (1,)), ((), ()))
_TN = (((0,), (0,)), ((), ()))


def _cparams(*sem):
    return pltpu.CompilerParams(dimension_semantics=sem, vmem_limit_bytes=VMEM_LIMIT)


def _resident(shape):
    zeros = (0,) * len(shape)
    return pl.BlockSpec(shape, lambda *_: zeros, pipeline_mode=pl.Buffered(1))


def _rms(x, g):
    inv = lax.rsqrt(jnp.mean(x * x, axis=-1, keepdims=True) + NORM_EPS)
    return x * inv * g


def _sigmoid(x):
    return 1.0 / (1.0 + jnp.exp(-x))


def _softplus(x):
    return jnp.maximum(x, 0.0) + jnp.log1p(jnp.exp(-jnp.abs(x)))


def _split3(v):
    hi = v.astype(BF16)
    r = v - hi.astype(F32)
    mid = r.astype(BF16)
    lo = (r - mid.astype(F32)).astype(BF16)
    return hi, mid, lo


def _dot(a, b):
    return jnp.dot(a, b, preferred_element_type=F32)


def _block_selector(n_blocks, width):
    col = lax.broadcasted_iota(jnp.int32, (n_blocks, n_blocks * width), 1)
    lo = lax.broadcasted_iota(jnp.int32, (n_blocks, n_blocks * width), 0) * width
    return jnp.where((col >= lo) & (col < lo + width), 1.0, 0.0).astype(BF16)


def _sel_right(v, e):
    hi, mid, lo = _split3(v)
    return _dot(hi, e) + _dot(mid, e) + _dot(lo, e)


def _sel_left(e, v):
    hi, mid, lo = _split3(v)
    return _dot(e, hi) + _dot(e, mid) + _dot(e, lo)


def _norm_matmul_kernel(x_ref, g_ref, w_ref, o_ref):
    h = _rms(x_ref[...], g_ref[...]).astype(BF16)
    o_ref[...] = _dot(h, w_ref[...])


def _norm_matmul_rope_kernel(x_ref, g_ref, w_ref, c_ref, s1_ref, s2_ref, o_ref, *, n_rot):
    h = _rms(x_ref[...], g_ref[...]).astype(BF16)
    res = _dot(h, w_ref[...])
    c, s1, s2 = c_ref[...], s1_ref[...], s2_ref[...]
    half = ATT_ROT_DIM // 2
    for j in range(o_ref.shape[0]):
        x = res[:, j * LANES:(j + 1) * LANES]
        if j < n_rot // LANES:
            x = x * c + pltpu.roll(x, half, 1) * s1 + pltpu.roll(x, LANES - half, 1) * s2
        o_ref[j] = x


def _norm_matmul(x, g, w, *, tm, rope=None):
    m, d = x.shape
    n = w.shape[1]
    assert m % tm == 0
    in_specs = [pl.BlockSpec((tm, d), lambda i: (i, 0)), _resident((1, d)), _resident((d, n))]
    args = [x, g.reshape(1, d), w]
    if rope is None:
        body = _norm_matmul_kernel
        out_spec = pl.BlockSpec((tm, n), lambda i: (i, 0))
        out_shape = jax.ShapeDtypeStruct((m, n), F32)
    else:
        n_rot, tables = rope
        assert n % LANES == 0 and n_rot % LANES == 0
        body = functools.partial(_norm_matmul_rope_kernel, n_rot=n_rot)
        in_specs += [pl.BlockSpec((tm, LANES), lambda i: (i, 0))] * 3
        args += list(tables)
        out_spec = pl.BlockSpec((n // LANES, tm, LANES), lambda i: (0, i, 0))
        out_shape = jax.ShapeDtypeStruct((n // LANES, m, LANES), F32)
    return pl.pallas_call(
        body,
        grid=(m // tm,),
        in_specs=in_specs,
        out_specs=out_spec,
        out_shape=out_shape,
        compiler_params=_cparams("parallel"),
        name="norm_proj" if rope is None else "norm_proj_rope",
    )(*args)


def _proj_res_kernel(y_ref, w_ref, x_ref, o_ref):
    o_ref[...] = x_ref[...] + _dot(y_ref[...], w_ref[...])


def _proj_res(y, w, x, *, tm):
    m, k = y.shape
    d = w.shape[1]
    assert m % tm == 0
    return pl.pallas_call(
        _proj_res_kernel,
        grid=(m // tm,),
        in_specs=[pl.BlockSpec((tm, k), lambda i: (i, 0)), _resident((k, d)),
                  pl.BlockSpec((tm, d), lambda i: (i, 0))],
        out_specs=pl.BlockSpec((tm, d), lambda i: (i, 0)),
        out_shape=jax.ShapeDtypeStruct((m, d), F32),
        compiler_params=_cparams("parallel"),
        name="proj_res",
    )(y, w, x)


def _ffn_kernel(x_ref, g_ref, wg_ref, wu_ref, wd_ref, gf_ref, o_ref, *, final_norm):
    x = x_ref[...]
    h = _rms(x, g_ref[...]).astype(BF16)
    gate = _dot(h, wg_ref[...])
    up = _dot(h, wu_ref[...])
    act = (gate * _sigmoid(gate) * up).astype(BF16)
    y = x + _dot(act, wd_ref[...])
    o_ref[...] = _rms(y, gf_ref[...]) if final_norm else y


def _ffn(x, g, wg, wu, wd, *, tm, final_g=None):
    m, d = x.shape
    f = wg.shape[1]
    assert m % tm == 0
    gf = g if final_g is None else final_g
    return pl.pallas_call(
        functools.partial(_ffn_kernel, final_norm=final_g is not None),
        grid=(m // tm,),
        in_specs=[pl.BlockSpec((tm, d), lambda i: (i, 0)), _resident((1, d)),
                  _resident((d, f)), _resident((d, f)), _resident((f, d)), _resident((1, d))],
        out_specs=pl.BlockSpec((tm, d), lambda i: (i, 0)),
        out_shape=jax.ShapeDtypeStruct((m, d), F32),
        compiler_params=_cparams("parallel"),
        name="swiglu",
    )(x, g.reshape(1, d), wg, wu, wd, gf.reshape(1, d))


def _ssd_kernel(zx_ref, cbuf_ref, h0_ref, cw_ref, cb_ref, dtb_ref, alog_ref, dsk_ref, nw_ref,
                y_ref, hfin_ref, xpad, state, yacc, *, t, valid, d_inner, n_heads):
    c = pl.program_id(1)
    n_grp = SSM_N_GROUPS
    d_st = SSM_D_STATE
    hp = d_inner // n_grp
    hpg = n_heads // n_grp
    pdim = SSM_HEAD_DIM
    bc_w = n_grp * d_st
    conv_dim = d_inner + 2 * bc_w

    @pl.when(c == 0)
    def _init():
        state[...] = h0_ref[0]
        xpad[0:SUBLANES, :] = cbuf_ref[0]

    xraw = zx_ref[0, :, d_inner:d_inner + conv_dim]
    xpad[SUBLANES:SUBLANES + t, :] = xraw
    cw = cw_ref[...]
    acc = cb_ref[...]
    for k in range(SSM_CONV - 1):
        off = SUBLANES - (SSM_CONV - 1) + k
        acc = acc + xpad[off:off + t, :] * cw[k:k + 1, :]
    acc = acc + xraw * cw[SSM_CONV - 1:SSM_CONV, :]
    xpad[0:SUBLANES, :] = xpad[t:t + SUBLANES, :]
    xbc = acc * _sigmoid(acc)
    xs = xbc[:, :d_inner]
    bm = xbc[:, d_inner:d_inner + bc_w]
    cm = xbc[:, d_inner + bc_w:]

    dt = _softplus(zx_ref[0, :, d_inner + conv_dim:] + dtb_ref[...])
    if valid < t:
        dt = jnp.where(lax.broadcasted_iota(jnp.int32, dt.shape, 0) < valid, dt, 0.0)
    a = dt * (-jnp.exp(alog_ref[...]))

    iq = lax.broadcasted_iota(jnp.int32, (t, t), 0)
    ik = lax.broadcasted_iota(jnp.int32, (t, t), 1)
    low = iq >= ik
    tri = jnp.where(low, 1.0, 0.0).astype(BF16)
    a_cs = _sel_left(tri, a)

    e_hp = _block_selector(n_heads, pdim)
    e_hk = _block_selector(n_heads, t)
    a_last = a_cs[t - 1:t, :]
    stack = jnp.concatenate([dt, jnp.exp(a_cs), jnp.exp(a_last - a_cs)], axis=0)
    ex = _sel_right(stack, e_hp)
    dt_e, ea_e, dst_e = ex[0:t], ex[t:2 * t], ex[2 * t:3 * t]
    xdt = xs * dt_e
    xdt_b = xdt.astype(BF16)
    xdst_b = (xdt * dst_e).astype(BF16)

    a_col = _sel_right(a_cs, e_hk)
    diag = (lax.broadcasted_iota(jnp.int32, (t, n_heads * t), 1) & (t - 1)
            ) == lax.broadcasted_iota(jnp.int32, (t, n_heads * t), 0)
    a_row = jnp.sum(jnp.where(diag, a_col, 0.0), axis=0, keepdims=True)

    for g in range(n_grp):
        bg = bm[:, g * d_st:(g + 1) * d_st].astype(BF16)
        cg = cm[:, g * d_st:(g + 1) * d_st].astype(BF16)
        cb = lax.dot_general(cg, bg, _NT, preferred_element_type=F32)
        sg = state[g]
        gl = slice(g * hp, (g + 1) * hp)
        yacc[:, gl] = _dot(cg, sg.astype(BF16)) * ea_e[:, gl]
        for h in range(hpg):
            hh = g * hpg + h
            ks = slice(hh * t, (hh + 1) * t)
            ps = slice(hh * pdim, (hh + 1) * pdim)
            decay = jnp.exp(jnp.where(low, a_col[:, ks] - a_row[:, ks], -jnp.inf))
            yacc[:, ps] += _dot((cb * decay).astype(BF16), xdt_b[:, ps])
        upd = lax.dot_general(bg, xdst_b[:, gl], _TN, preferred_element_type=F32)
        state[g] = sg * ea_e[t - 1:t, gl] + upd

    z = zx_ref[0, :, :d_inner]
    y = (yacc[...] + xs * dsk_ref[...]) * (z * _sigmoid(z))
    nw = nw_ref[...]
    for g in range(n_grp):
        gl = slice(g * hp, (g + 1) * hp)
        y_ref[0, :, gl] = _rms(y[:, gl], nw[:, gl]).astype(y_ref.dtype)

    @pl.when(c == pl.num_programs(1) - 1)
    def _fin():
        hfin_ref[0] = state[...]


def _ssd(zx, conv_buf, h0, conv_w, conv_b, dt_bias, a_log, d_skip, norm_w, *, t, valid):
    b, l, in_dim = zx.shape
    n_heads = a_log.shape[0]
    d_inner = n_heads * SSM_HEAD_DIM
    conv_dim = conv_w.shape[1]
    n_grp, d_st = SSM_N_GROUPS, SSM_D_STATE
    hp = d_inner // n_grp
    assert l % t == 0 and in_dim == d_inner + conv_dim + n_heads
    cbuf = jnp.pad(conv_buf, ((0, 0), (SUBLANES - (SSM_CONV - 1), 0), (0, 0)))
    h0t = h0.reshape(b, n_grp, n_heads // n_grp, SSM_HEAD_DIM, d_st)
    h0t = h0t.transpose(0, 1, 4, 2, 3).reshape(b, n_grp, d_st, hp)
    body = functools.partial(_ssd_kernel, t=t, valid=valid, d_inner=d_inner, n_heads=n_heads)
    y, hfin = pl.pallas_call(
        body,
        grid=(b, l // t),
        in_specs=[
            pl.BlockSpec((1, t, in_dim), lambda i, c: (i, c, 0)),
            pl.BlockSpec((1, SUBLANES, conv_dim), lambda i, c: (i, 0, 0)),
            pl.BlockSpec((1, n_grp, d_st, hp), lambda i, c: (i, 0, 0, 0)),
            _resident((SSM_CONV, conv_dim)), _resident((1, conv_dim)),
            _resident((1, n_heads)), _resident((1, n_heads)),
            _resident((1, d_inner)), _resident((1, d_inner)),
        ],
        out_specs=[
            pl.BlockSpec((1, t, d_inner), lambda i, c: (i, c, 0)),
            pl.BlockSpec((1, n_grp, d_st, hp), lambda i, c: (i, 0, 0, 0)),
        ],
        out_shape=[
            jax.ShapeDtypeStruct((b, l, d_inner), BF16),
            jax.ShapeDtypeStruct((b, n_grp, d_st, hp), F32),
        ],
        scratch_shapes=[
            pltpu.VMEM((t + SUBLANES, conv_dim), F32),
            pltpu.VMEM((n_grp, d_st, hp), F32),
            pltpu.VMEM((t, d_inner), F32),
        ],
        compiler_params=_cparams("parallel", "arbitrary"),
        name="ssd_scan",
    )(zx, cbuf, h0t, conv_w, conv_b.reshape(1, -1), dt_bias.reshape(1, -1), a_log.reshape(1, -1),
      jnp.repeat(d_skip, SSM_HEAD_DIM).reshape(1, -1), norm_w.reshape(1, -1))
    hfin = hfin.reshape(b, n_grp, d_st, n_heads // n_grp, SSM_HEAD_DIM)
    hfin = hfin.transpose(0, 1, 3, 4, 2).reshape(b, n_heads, SSM_HEAD_DIM, d_st)
    return y, hfin


def _gla_kernel(p_ref, s0_ref, wg_ref, gb_ref, nw_ref, o_ref, sfin_ref, state, oacc,
                *, tb, ch, valid, dk, dv):
    c = pl.program_id(1)
    n_h = GLA_N_HEADS
    hk, hv = dk // n_h, dv // n_h

    @pl.when(c == 0)
    def _init():
        state[...] = s0_ref[0]

    q = p_ref[0, :, 0:dk] * (hk ** -0.5)
    k = p_ref[0, :, dk:2 * dk]
    v = p_ref[0, :, 2 * dk:2 * dk + dv]
    r = p_ref[0, :, 2 * dk + dv:2 * dk + 2 * dv]
    g_low = p_ref[0, :, 2 * dk + 2 * dv:]
    x = _dot(g_low.astype(BF16), wg_ref[...]) + gb_ref[...]
    log_a = -_softplus(-x) / GLA_GATE_NORM
    if valid < tb:
        keep = lax.broadcasted_iota(jnp.int32, (tb, dk), 0) < valid
        log_a = jnp.where(keep, log_a, 0.0)
        k = jnp.where(keep, k, 0.0)

    ir = lax.broadcasted_iota(jnp.int32, (tb, tb), 0)
    ic = lax.broadcasted_iota(jnp.int32, (tb, tb), 1)
    causal = ir >= ic
    blocktri = jnp.where(causal & ((ir & -ch) == (ic & -ch)), 1.0, 0.0).astype(BF16)
    bcum = _sel_left(blocktri, log_a)
    q_t = (q * jnp.exp(bcum)).astype(BF16)
    k_t = (k * jnp.exp(-bcum)).astype(BF16)
    v_b = v.astype(BF16)
    causal_c = causal[0:ch, 0:ch]

    for j in range(tb // ch):
        rows = slice(j * ch, (j + 1) * ch)
        b_last = bcum[(j + 1) * ch - 1:(j + 1) * ch, :]
        k_dec = (k[rows] * jnp.exp(b_last - bcum[rows])).astype(BF16)
        e_last = jnp.exp(b_last)
        for h in range(n_h):
            ks = slice(h * hk, (h + 1) * hk)
            vs = slice(h * hv, (h + 1) * hv)
            qh, kh, vh = q_t[rows, ks], k_t[rows, ks], v_b[rows, vs]
            att = lax.dot_general(qh, kh, _NT, preferred_element_type=F32)
            att = jnp.where(causal_c, att, 0.0).astype(BF16)
            sh = state[h]
            oacc[rows, vs] = _dot(att, vh) + lax.dot_general(qh, sh.astype(BF16), _NT,
                                                           preferred_element_type=F32)
            upd = lax.dot_general(vh, k_dec[:, ks], _TN, preferred_element_type=F32)
            state[h] = sh * e_last[:, ks] + upd

    gate = r * _sigmoid(r)
    nw = nw_ref[...]
    for h in range(n_h):
        vs = slice(h * hv, (h + 1) * hv)
        o_ref[0, :, vs] = (_rms(oacc[:, vs], nw) * gate[:, vs]).astype(o_ref.dtype)

    @pl.when(c == pl.num_programs(1) - 1)
    def _fin():
        sfin_ref[0] = state[...]


def _gla(proj, s0, w_gate, gate_bias, norm_w, *, tb, ch, valid):
    b, l, in_dim = proj.shape
    n_h = GLA_N_HEADS
    rank, dk = w_gate.shape
    hv = norm_w.shape[0]
    dv = hv * n_h
    hk = dk // n_h
    assert l % tb == 0 and tb % ch == 0 and in_dim == 2 * dk + 2 * dv + rank
    body = functools.partial(_gla_kernel, tb=tb, ch=ch, valid=valid, dk=dk, dv=dv)
    o, sfin = pl.pallas_call(
        body,
        grid=(b, l // tb),
        in_specs=[
            pl.BlockSpec((1, tb, in_dim), lambda i, c: (i, c, 0)),
            pl.BlockSpec((1, n_h, hv, hk), lambda i, c: (i, 0, 0, 0)),
            _resident((rank, dk)), _resident((1, dk)), _resident((1, hv)),
        ],
        out_specs=[
            pl.BlockSpec((1, tb, dv), lambda i, c: (i, c, 0)),
            pl.BlockSpec((1, n_h, hv, hk), lambda i, c: (i, 0, 0, 0)),
        ],
        out_shape=[
            jax.ShapeDtypeStruct((b, l, dv), BF16),
            jax.ShapeDtypeStruct((b, n_h, hv, hk), F32),
        ],
        scratch_shapes=[pltpu.VMEM((n_h, hv, hk), F32), pltpu.VMEM((tb, dv), F32)],
        compiler_params=_cparams("parallel", "arbitrary"),
        name="gla_scan",
    )(proj, jnp.swapaxes(s0, -1, -2), w_gate.astype(BF16), gate_bias.reshape(1, -1),
      norm_w.reshape(1, -1))
    return o, jnp.swapaxes(sfin, -1, -2)


def _attn_prompt_kernel(q_ref, kp_ref, kc_ref, vp_ref, vc_ref, o_ref, lse_ref, *, dilation, nb, span):
    first = pl.program_id(2) == 0
    blk = ATT_BLOCK
    dh = ATT_HEAD_DIM
    iq = lax.broadcasted_iota(jnp.int32, (blk, blk), 0)
    ik = lax.broadcasted_iota(jnp.int32, (blk, blk), 1)
    in_band = ik >= iq + (blk - span)
    ok_cur = ik <= iq
    scale = dh ** -0.5

    def rows(r, jb):
        start = r + dilation * jb * blk
        return pl.ds(start, blk) if dilation == 1 else pl.ds(start, blk, stride=dilation)

    for r in range(dilation):
        for jb in range(nb):
            cur = rows(r, jb)
            if jb == 0:
                k_prev, v_prev = kp_ref[0, rows(r, nb - 1), :], vp_ref[0, rows(r, nb - 1), :]
                ok_prev = in_band & jnp.logical_not(first)
            else:
                k_prev, v_prev = kc_ref[0, rows(r, jb - 1), :], vc_ref[0, rows(r, jb - 1), :]
                ok_prev = in_band
            q2, k_cur, v_cur = q_ref[0, cur, :], kc_ref[0, cur, :], vc_ref[0, cur, :]
            outs, lses = [], []
            for hh in range(LANES // dh):
                hs = slice(hh * dh, (hh + 1) * dh)
                qh = q2[:, hs].astype(BF16)
                s_p = lax.dot_general(qh, k_prev[:, hs].astype(BF16), _NT, preferred_element_type=F32) * scale
                s_c = lax.dot_general(qh, k_cur[:, hs].astype(BF16), _NT, preferred_element_type=F32) * scale
                s_p = jnp.where(ok_prev, s_p, -jnp.inf)
                s_c = jnp.where(ok_cur, s_c, -jnp.inf)
                mx = jnp.max(jnp.maximum(s_p, s_c), axis=-1, keepdims=True)
                p_p = jnp.exp(s_p - mx)
                p_c = jnp.exp(s_c - mx)
                den = jnp.sum(p_p + p_c, axis=-1, keepdims=True)
                outs.append(_dot((p_p / den).astype(BF16), v_prev[:, hs].astype(BF16))
                            + _dot((p_c / den).astype(BF16), v_cur[:, hs].astype(BF16)))
                lses.append(jnp.broadcast_to(mx + jnp.log(den), (blk, dh)))
            o_ref[0, cur, :] = jnp.concatenate(outs, axis=-1)
            lse_ref[0, cur, :] = jnp.concatenate(lses, axis=-1)


def _attn_prompt(qkv, gi, *, b, l, window, dilation):
    n_slab = qkv.shape[0] // 3
    spg = ATT_HEADS_PER_GROUP * ATT_HEAD_DIM // LANES
    span = window // dilation
    blk = ATT_BLOCK
    nb = max(1, 512 // (dilation * blk))
    r_rows = dilation * nb * blk
    n_sup = l // r_rows
    assert l % r_rows == 0 and span <= blk and qkv.shape[1] == b * l

    def spec(part, prev):
        def index(bi, sp, i):
            i = jnp.maximum(i - 1, 0) if prev else i
            return (part * n_slab + gi * spg + sp, bi * n_sup + i, 0)
        return pl.BlockSpec((1, r_rows, LANES), index)

    out_spec = pl.BlockSpec((1, r_rows, LANES), lambda bi, sp, i: (sp, bi * n_sup + i, 0))
    return pl.pallas_call(
        functools.partial(_attn_prompt_kernel, dilation=dilation, nb=nb, span=span),
        grid=(b, spg, n_sup),
        in_specs=[spec(0, False), spec(1, True), spec(1, False), spec(2, True), spec(2, False)],
        out_specs=[out_spec, out_spec],
        out_shape=[jax.ShapeDtypeStruct((spg, b * l, LANES), F32)] * 2,
        compiler_params=_cparams("parallel", "parallel", "parallel"),
        name="attn_prompt",
    )(qkv, qkv, qkv, qkv, qkv)


def _attn_sample_kernel(q_ref, kn_ref, vn_ref, cache_ref, o_ref, lse_ref, cout_ref,
                        *, window, dilation, n_new, seqs):
    dh = ATT_HEAD_DIM
    rows = seqs * n_new
    scale = dh ** -0.5
    row = lax.broadcasted_iota(jnp.int32, (rows, window), 0)
    col = lax.broadcasted_iota(jnp.int32, (rows, window), 1)
    sq = row & (n_new - 1)
    ok_cache = (col >= sq) & (((col - sq) & (dilation - 1)) == 0)
    row_n = lax.broadcasted_iota(jnp.int32, (rows, rows), 0)
    col_n = lax.broadcasted_iota(jnp.int32, (rows, rows), 1)
    back = (row_n & (n_new - 1)) - (col_n & (n_new - 1))
    same_res = (back >= 0) & ((back & (dilation - 1)) == 0)
    lane_t = lax.broadcasted_iota(jnp.int32, (2 * rows, LANES), 1)
    tok_t = lax.broadcasted_iota(jnp.int32, (2 * rows, LANES), 0)
    lane_o = lax.broadcasted_iota(jnp.int32, (dh, LANES), 1)
    zpad = jnp.zeros((rows, dh), F32)
    for bb in range(seqs):
        r0 = bb * n_new
        ok_new = same_res & (col_n >= r0) & (col_n < r0 + n_new)
        place = jnp.where(lane_t - (LANES - n_new) == tok_t - r0, 1.0, 0.0).astype(BF16)
        for h in range(ATT_HEADS_PER_GROUP):
            sl = h * dh // LANES
            hs = slice(h * dh % LANES, h * dh % LANES + dh)
            q8 = q_ref[sl][:, hs].astype(BF16)
            k_new, v_new = kn_ref[sl][:, hs], vn_ref[sl][:, hs]
            k_t, v_t = cache_ref[bb, 0, h], cache_ref[bb, 1, h]
            s_c = jnp.where(ok_cache, _dot(q8, k_t.astype(BF16)) * scale, -jnp.inf)
            s_n = lax.dot_general(q8, k_new.astype(BF16), _NT, preferred_element_type=F32) * scale
            s_n = jnp.where(ok_new, s_n, -jnp.inf)
            mx = jnp.maximum(jnp.max(s_c, axis=-1, keepdims=True), jnp.max(s_n, axis=-1, keepdims=True))
            p_c = jnp.exp(s_c - mx)
            p_n = jnp.exp(s_n - mx)
            den = jnp.sum(p_c, axis=-1, keepdims=True) + jnp.sum(p_n, axis=-1, keepdims=True)
            o = (lax.dot_general((p_c / den).astype(BF16), v_t.astype(BF16), _NT, preferred_element_type=F32)
                 + _dot((p_n / den).astype(BF16), v_new.astype(BF16)))
            o_ref[sl, r0:r0 + n_new, hs] = o[r0:r0 + n_new]
            lse_ref[sl, r0:r0 + n_new, hs] = jnp.broadcast_to(mx + jnp.log(den), (rows, dh))[r0:r0 + n_new]
            for kv, old, new in ((0, k_t, k_new), (1, v_t, v_new)):
                hi, mid, lo = _split3(jnp.concatenate([new, zpad], axis=0))
                tail = sum(lax.dot_general(part, place, _TN, preferred_element_type=F32)
                           for part in (hi, mid, lo))
                shifted = pltpu.roll(old, window - n_new, 1)
                last = jnp.where(lane_o >= LANES - n_new, tail, shifted[:, window - LANES:])
                if window > LANES:
                    cout_ref[bb, kv, h, :, :window - LANES] = shifted[:, :window - LANES]
                cout_ref[bb, kv, h, :, window - LANES:] = last


def _attn_sample(qkv, cache_t, gi, *, window, dilation, n_new):
    n_slab = qkv.shape[0] // 3
    spg = ATT_HEADS_PER_GROUP * ATT_HEAD_DIM // LANES
    b = cache_t.shape[0]
    seqs = SUBLANES // n_new
    rows = seqs * n_new
    assert rows == SUBLANES and b % seqs == 0 and qkv.shape[1] == b * n_new
    assert cache_t.shape[-1] == window and window % dilation == 0 and window % LANES == 0
    assert dilation & (dilation - 1) == 0 and n_new & (n_new - 1) == 0

    def slabs(part):
        return pl.BlockSpec((spg, rows, LANES), lambda i: (part * n_slab // spg + gi, i, 0))

    cache_spec = pl.BlockSpec((seqs,) + cache_t.shape[1:], lambda i: (i, 0, 0, 0, 0))
    out_spec = pl.BlockSpec((spg, rows, LANES), lambda i: (0, i, 0))
    return pl.pallas_call(
        functools.partial(_attn_sample_kernel, window=window, dilation=dilation, n_new=n_new, seqs=seqs),
        grid=(b // seqs,),
        in_specs=[slabs(0), slabs(1), slabs(2), cache_spec],
        out_specs=[out_spec, out_spec, cache_spec],
        out_shape=[jax.ShapeDtypeStruct((spg, b * n_new, LANES), F32)] * 2
        + [jax.ShapeDtypeStruct(cache_t.shape, F32)],
        compiler_params=_cparams("parallel"),
        name="attn_sample",
    )(qkv, qkv, qkv, cache_t)


def _attn_out_kernel(o0_ref, o1_ref, o2_ref, l0_ref, l1_ref, l2_ref, w_ref, x_ref, out_ref):
    o_refs, l_refs = (o0_ref, o1_ref, o2_ref), (l0_ref, l1_ref, l2_ref)
    spg = o0_ref.shape[0]
    pieces = [[None] * spg for _ in o_refs]
    for s in range(spg):
        ls = [l_ref[s] for l_ref in l_refs]
        mx = jnp.maximum(jnp.maximum(ls[0], ls[1]), ls[2])
        es = [jnp.exp(l - mx) for l in ls]
        den = es[0] + es[1] + es[2]
        for g, (o_ref, e) in enumerate(zip(o_refs, es)):
            pieces[g][s] = o_ref[s] * (e / den)
    mixed = jnp.concatenate([p for grp in pieces for p in grp], axis=-1).astype(BF16)
    out_ref[...] = x_ref[...] + _dot(mixed, w_ref[...])


def _attn_out(os_, lses, w, x, *, tm):
    m, d = x.shape
    spg = os_[0].shape[0]
    row = pl.BlockSpec((spg, tm, LANES), lambda i: (0, i, 0))
    return pl.pallas_call(
        _attn_out_kernel,
        grid=(m // tm,),
        in_specs=[row] * 6 + [_resident(w.shape), pl.BlockSpec((tm, d), lambda i: (i, 0))],
        out_specs=pl.BlockSpec((tm, d), lambda i: (i, 0)),
        out_shape=jax.ShapeDtypeStruct((m, d), F32),
        compiler_params=_cparams("parallel"),
        name="attn_mix_proj",
    )(*os_, *lses, w, x)


def _rope_tables(pos):
    half = ATT_ROT_DIM // 2
    inv_freq = ROPE_THETA ** (-jnp.arange(0, ATT_ROT_DIM, 2, dtype=F32) / ATT_ROT_DIM)
    ang = pos.astype(F32)[:, None] * inv_freq
    cos, sin = jnp.cos(ang), jnp.sin(ang)
    n = pos.shape[0]
    rest = ATT_HEAD_DIM - ATT_ROT_DIM
    one, zero, zh = jnp.ones((n, rest), F32), jnp.zeros((n, rest), F32), jnp.zeros((n, half), F32)
    reps = LANES // ATT_HEAD_DIM
    c = jnp.tile(jnp.concatenate([cos, cos, one], axis=1), (1, reps))
    s1 = jnp.tile(jnp.concatenate([zh, sin, zero], axis=1), (1, reps))
    s2 = jnp.tile(jnp.concatenate([-sin, zh, zero], axis=1), (1, reps))
    return c, s1, s2


def _pad_rows(t, rows):
    return jnp.pad(t, ((0, 0), (0, rows - t.shape[1]), (0, 0)))


def kernel(x_prompt, x_sample, state_ssm, state_ssm_conv, state_gla, cache_kv_g0, cache_kv_g1, cache_kv_g2,
           norm_mix, norm_ffn, ffn_gate, ffn_up, ffn_down,
           ssm_w_in, ssm_conv_w, ssm_conv_b, ssm_dt_bias, ssm_a_log, ssm_d, ssm_norm, ssm_w_out,
           gla_w_in, gla_w_gate, gla_gate_bias, gla_norm, gla_w_out,
           att_w_qkv, att_w_out, norm_final):
    bp, lp, d = x_prompt.shape
    bs, ls, _ = x_sample.shape
    depth = norm_mix.shape[0]
    mp, ms = bp * lp, bs * ls
    tm_p = 512 if mp % 512 == 0 else mp
    tm_s = ms
    xp = x_prompt.reshape(mp, d)
    xs = x_sample.reshape(ms, d)
    att_caches = (cache_kv_g0, cache_kv_g1, cache_kv_g2)
    hg_w = ATT_HEADS_PER_GROUP * ATT_HEAD_DIM
    n_att = hg_w * len(ATT_GROUPS)
    rope_p = _rope_tables(jnp.tile(jnp.arange(lp), bp))
    rope_s = _rope_tables(jnp.tile(PAST_LEN + jnp.arange(ls), bs))

    ssm_p, ssm_s, conv_p, conv_s, gla_p, gla_s = [], [], [], [], [], []
    kv_p, kv_s = [[], [], []], [[], [], []]
    for i in range(depth):
        m, j = i % N_MIXERS, i // N_MIXERS
        if m == 0:
            w_in = ssm_w_in[j].astype(BF16)
            conv_dim = ssm_conv_w.shape[2]
            d_inner = ssm_w_out.shape[1]
            wts = (ssm_conv_w[j], ssm_conv_b[j], ssm_dt_bias[j], ssm_a_log[j], ssm_d[j], ssm_norm[j])
            zx_p = _norm_matmul(xp, norm_mix[i], w_in, tm=tm_p).reshape(bp, lp, -1)
            zx_s = _norm_matmul(xs, norm_mix[i], w_in, tm=tm_s).reshape(bs, ls, -1)
            t_p = min(SSM_CHUNK, lp)
            y_p, h_p = _ssd(zx_p, jnp.zeros((bp, SSM_CONV - 1, conv_dim), F32),
                            jnp.zeros((bp,) + state_ssm.shape[2:], F32), *wts, t=t_p, valid=t_p)
            y_s, h_s = _ssd(_pad_rows(zx_s, SAMPLE_PAD), state_ssm_conv[j], state_ssm[j], *wts,
                            t=SAMPLE_PAD, valid=ls)
            y_s = y_s[:, :ls]
            xbc_p = zx_p[:, :, d_inner:d_inner + conv_dim]
            xbc_s = zx_s[:, :, d_inner:d_inner + conv_dim]
            conv_p.append(xbc_p[:, -(SSM_CONV - 1):])
            conv_s.append(jnp.concatenate([state_ssm_conv[j], xbc_s], axis=1)[:, -(SSM_CONV - 1):])
            ssm_p.append(h_p)
            ssm_s.append(h_s)
            w_out = ssm_w_out[j].astype(BF16)
            xp = _proj_res(y_p.reshape(mp, -1), w_out, xp, tm=tm_p)
            xs = _proj_res(y_s.reshape(ms, -1), w_out, xs, tm=tm_s)
        elif m == 1:
            w_in = gla_w_in[j].astype(BF16)
            wts = (gla_w_gate[j], gla_gate_bias[j], gla_norm[j])
            pr_p = _norm_matmul(xp, norm_mix[i], w_in, tm=tm_p).reshape(bp, lp, -1)
            pr_s = _norm_matmul(xs, norm_mix[i], w_in, tm=tm_s).reshape(bs, ls, -1)
            ch_p = min(GLA_CHUNK, lp)
            tb_p = 256 if lp % 256 == 0 else ch_p
            o_p, s_p = _gla(pr_p, jnp.zeros((bp,) + state_gla.shape[2:], F32), *wts,
                            tb=tb_p, ch=ch_p, valid=tb_p)
            o_s, s_s = _gla(_pad_rows(pr_s, SAMPLE_PAD), state_gla[j], *wts,
                            tb=SAMPLE_PAD, ch=SAMPLE_PAD, valid=ls)
            o_s = o_s[:, :ls]
            gla_p.append(s_p)
            gla_s.append(s_s)
            w_out = gla_w_out[j].astype(BF16)
            xp = _proj_res(o_p.reshape(mp, -1), w_out, xp, tm=tm_p)
            xs = _proj_res(o_s.reshape(ms, -1), w_out, xs, tm=tm_s)
        else:
            w_qkv = att_w_qkv[j].astype(BF16)
            qkv_p = _norm_matmul(xp, norm_mix[i], w_qkv, tm=tm_p, rope=(2 * n_att, rope_p))
            qkv_s = _norm_matmul(xs, norm_mix[i], w_qkv, tm=tm_s, rope=(2 * n_att, rope_s))
            n_slab = n_att // LANES
            spg = hg_w // LANES
            os_p, ls_p, os_s, ls_s = [], [], [], []
            for gi, (window, dilation) in enumerate(ATT_GROUPS):
                o, lse = _attn_prompt(qkv_p, gi, b=bp, l=lp, window=window, dilation=dilation)
                os_p.append(o)
                ls_p.append(lse)
                keep = min(window, lp)
                kv = qkv_p.reshape(3, n_slab, bp, lp, LANES // ATT_HEAD_DIM, ATT_HEAD_DIM)
                kv = kv[1:, gi * spg:(gi + 1) * spg, :, lp - keep:]
                kv_p[gi].append(kv.transpose(2, 3, 0, 1, 4, 5)
                                .reshape(bp, keep, 2, ATT_HEADS_PER_GROUP, ATT_HEAD_DIM))
                cache_t = att_caches[gi][j].transpose(0, 2, 3, 4, 1)
                o, lse, cache_new = _attn_sample(qkv_s, cache_t, gi, window=window, dilation=dilation, n_new=ls)
                os_s.append(o)
                ls_s.append(lse)
                kv_s[gi].append(cache_new.transpose(0, 4, 1, 2, 3))
            w_out = att_w_out[j].astype(BF16)
            xp = _attn_out(os_p, ls_p, w_out, xp, tm=tm_p)
            xs = _attn_out(os_s, ls_s, w_out, xs, tm=tm_s)
        wg, wu, wd = ffn_gate[i].astype(BF16), ffn_up[i].astype(BF16), ffn_down[i].astype(BF16)
        final_g = norm_final if i == depth - 1 else None
        xp = _ffn(xp, norm_ffn[i], wg, wu, wd, tm=tm_p, final_g=final_g)
        xs = _ffn(xs, norm_ffn[i], wg, wu, wd, tm=tm_s, final_g=final_g)
    y_prompt = xp.reshape(bp, lp, d)
    y_sample = xs.reshape(bs, ls, d)
    return (y_prompt, y_sample,
            jnp.stack(ssm_p), jnp.stack(ssm_s), jnp.stack(conv_p), jnp.stack(conv_s),
            jnp.stack(gla_p), jnp.stack(gla_s),
            jnp.stack(kv_p[0]), jnp.stack(kv_s[0]), jnp.stack(kv_p[1]), jnp.stack(kv_s[1]),
            jnp.stack(kv_p[2]), jnp.stack(kv_s[2]))
```

```python
import functools

import jax
import jax.numpy as jnp
from jax import lax
from jax.experimental import pallas as pl
from jax.experimental.pallas import tpu as pltpu

F32 = jnp.float32
BF16 = jnp.bfloat16

NORM_EPS = 1e-6
N_MIXERS = 3

SSM_HEAD_DIM = 64
SSM_N_GROUPS = 4
SSM_D_STATE = 128
SSM_CONV = 4
SSM_CHUNK = 128

GLA_N_HEADS = 4
GLA_GATE_NORM = 16.0
GLA_CHUNK = 32

ATT_GROUPS = ((128, 1), (512, 4), (2048, 16))
ATT_HEADS_PER_GROUP = 4
ATT_HEAD_DIM = 64
ATT_ROT_DIM = ATT_HEAD_DIM // 4
ROPE_THETA = 500000.0
ATT_BLOCK = 128
ATT_SUPER_ROWS = 2048
PAST_LEN = 8192

LANES = 128
SUBLANES = 8
SAMPLE_PAD = 16
VMEM_LIMIT = 56 * 1024 * 1024

_NT = (((1,), (1,)), ((), ()))
_TN = (((0,), (0,)), ((), ()))


def _cparams(*sem):
    return pltpu.CompilerParams(dimension_semantics=sem, vmem_limit_bytes=VMEM_LIMIT)


def _resident(shape):
    zeros = (0,) * len(shape)
    return pl.BlockSpec(shape, lambda *_: zeros, pipeline_mode=pl.Buffered(1))


def _rms(x, g):
    inv = lax.rsqrt(jnp.mean(x * x, axis=-1, keepdims=True) + NORM_EPS)
    return x * inv * g


def _sigmoid(x):
    return 1.0 / (1.0 + jnp.exp(-x))


def _softplus(x):
    return jnp.maximum(x, 0.0) + jnp.log1p(jnp.exp(-jnp.abs(x)))


def _split3(v):
    hi = v.astype(BF16)
    r = v - hi.astype(F32)
    mid = r.astype(BF16)
    lo = (r - mid.astype(F32)).astype(BF16)
    return hi, mid, lo


def _dot(a, b):
    return jnp.dot(a, b, preferred_element_type=F32)


def _spread_matrix(n_blocks, width):
    k_dim = -(-3 * n_blocks // LANES) * LANES
    row = lax.broadcasted_iota(jnp.int32, (k_dim, n_blocks * width), 0)
    col = lax.broadcasted_iota(jnp.int32, (k_dim, n_blocks * width), 1)
    blk = row - jnp.where(row >= 2 * n_blocks, 2 * n_blocks, jnp.where(row >= n_blocks, n_blocks, 0))
    sel = (row < 3 * n_blocks) & (col >= blk * width) & (col < blk * width + width)
    return jnp.where(sel, 1.0, 0.0).astype(BF16)


def _spread(v, sel):
    rows, n_blocks = v.shape
    terms = [p.astype(F32) for p in _split3(v)]
    pad = jnp.zeros((rows, sel.shape[0] - 3 * n_blocks), F32)
    return _dot(jnp.concatenate(terms + [pad], axis=1).astype(BF16), sel)


def _sel_left(e, v):
    hi, mid, lo = _split3(v)
    return _dot(e, hi) + _dot(e, mid) + _dot(e, lo)


def _norm_matmul_kernel(x_ref, g_ref, w_ref, o_ref):
    h = _rms(x_ref[...], g_ref[...]).astype(BF16)
    o_ref[...] = _dot(h, w_ref[...])


def _norm_matmul_rope_kernel(x_ref, g_ref, w_ref, c_ref, s1_ref, s2_ref, o_ref, *, n_rot):
    h = _rms(x_ref[...], g_ref[...]).astype(BF16)
    res = _dot(h, w_ref[...])
    c, s1, s2 = c_ref[...], s1_ref[...], s2_ref[...]
    half = ATT_ROT_DIM // 2
    for j in range(o_ref.shape[0]):
        x = res[:, j * LANES:(j + 1) * LANES]
        if j < n_rot // LANES:
            x = x * c + pltpu.roll(x, half, 1) * s1 + pltpu.roll(x, LANES - half, 1) * s2
        o_ref[j] = x


def _norm_matmul(x, g, w, *, tm, rope=None):
    m, d = x.shape
    n = w.shape[1]
    assert m % tm == 0
    in_specs = [pl.BlockSpec((tm, d), lambda i: (i, 0)), _resident((1, d)), _resident((d, n))]
    args = [x, g.reshape(1, d), w]
    if rope is None:
        body = _norm_matmul_kernel
        out_spec = pl.BlockSpec((tm, n), lambda i: (i, 0))
        out_shape = jax.ShapeDtypeStruct((m, n), F32)
    else:
        n_rot, tables = rope
        assert n % LANES == 0 and n_rot % LANES == 0
        body = functools.partial(_norm_matmul_rope_kernel, n_rot=n_rot)
        in_specs += [pl.BlockSpec((tm, LANES), lambda i: (i, 0))] * 3
        args += list(tables)
        out_spec = pl.BlockSpec((n // LANES, tm, LANES), lambda i: (0, i, 0))
        out_shape = jax.ShapeDtypeStruct((n // LANES, m, LANES), F32)
    return pl.pallas_call(
        body,
        grid=(m // tm,),
        in_specs=in_specs,
        out_specs=out_spec,
        out_shape=out_shape,
        compiler_params=_cparams("parallel"),
        name="norm_proj" if rope is None else "norm_proj_rope",
    )(*args)


def _proj_res_kernel(y_ref, w_ref, x_ref, o_ref):
    o_ref[...] = x_ref[...] + _dot(y_ref[...], w_ref[...])


def _proj_res(y, w, x, *, tm):
    m, k = y.shape
    d = w.shape[1]
    assert m % tm == 0
    return pl.pallas_call(
        _proj_res_kernel,
        grid=(m // tm,),
        in_specs=[pl.BlockSpec((tm, k), lambda i: (i, 0)), _resident((k, d)),
                  pl.BlockSpec((tm, d), lambda i: (i, 0))],
        out_specs=pl.BlockSpec((tm, d), lambda i: (i, 0)),
        out_shape=jax.ShapeDtypeStruct((m, d), F32),
        compiler_params=_cparams("parallel"),
        name="proj_res",
    )(y, w, x)


def _ffn_kernel(x_ref, g_ref, wg_ref, wu_ref, wd_ref, gf_ref, o_ref, *, final_norm):
    x = x_ref[...]
    h = _rms(x, g_ref[...]).astype(BF16)
    gate = _dot(h, wg_ref[...])
    up = _dot(h, wu_ref[...])
    act = (gate * _sigmoid(gate) * up).astype(BF16)
    y = x + _dot(act, wd_ref[...])
    o_ref[...] = _rms(y, gf_ref[...]) if final_norm else y


def _ffn(x, g, wg, wu, wd, *, tm, final_g=None):
    m, d = x.shape
    f = wg.shape[1]
    assert m % tm == 0
    gf = g if final_g is None else final_g
    return pl.pallas_call(
        functools.partial(_ffn_kernel, final_norm=final_g is not None),
        grid=(m // tm,),
        in_specs=[pl.BlockSpec((tm, d), lambda i: (i, 0)), _resident((1, d)),
                  _resident((d, f)), _resident((d, f)), _resident((f, d)), _resident((1, d))],
        out_specs=pl.BlockSpec((tm, d), lambda i: (i, 0)),
        out_shape=jax.ShapeDtypeStruct((m, d), F32),
        compiler_params=_cparams("parallel"),
        name="swiglu",
    )(x, g.reshape(1, d), wg, wu, wd, gf.reshape(1, d))


def _ssd_kernel(zx_ref, cbuf_ref, h0_ref, cw_ref, cb_ref, dtb_ref, alog_ref, dsk_ref, nw_ref,
                tri_ref, sel_hp_ref, sel_hk_ref, sel_hn_ref, diag_ref,
                y_ref, hfin_ref, tail, state, yacc, *, t, valid, d_inner, n_heads):
    c = pl.program_id(1)
    n_grp = SSM_N_GROUPS
    d_st = SSM_D_STATE
    hp = d_inner // n_grp
    hpg = n_heads // n_grp
    pdim = SSM_HEAD_DIM
    bc_w = n_grp * d_st
    conv_dim = d_inner + 2 * bc_w

    @pl.when(c == 0)
    def _init():
        state[...] = h0_ref[0]
        tail[...] = cbuf_ref[0]

    xraw = zx_ref[0, :, d_inner:d_inner + conv_dim]
    groups = [tail[...]] + [xraw[r:r + SUBLANES] for r in range(0, t, SUBLANES)]
    cw = cw_ref[...]
    acc = cb_ref[...]
    row8 = lax.broadcasted_iota(jnp.int32, (SUBLANES, conv_dim), 0)
    for k in range(SSM_CONV - 1, 0, -1):
        rolled = [pltpu.roll(grp, k, 0) for grp in groups]
        shifted = jnp.concatenate([jnp.where(row8 < k, before, here)
                                   for before, here in zip(rolled[:-1], rolled[1:])], axis=0)
        acc = acc + shifted * cw[SSM_CONV - 1 - k:SSM_CONV - k, :]
    acc = acc + xraw * cw[SSM_CONV - 1:SSM_CONV, :]
    tail[...] = groups[-1]
    xbc = acc * _sigmoid(acc)
    xs = xbc[:, :d_inner]
    bm = xbc[:, d_inner:d_inner + bc_w]
    cm = xbc[:, d_inner + bc_w:]

    dt = _softplus(zx_ref[0, :, d_inner + conv_dim:] + dtb_ref[...])
    if valid < t:
        dt = jnp.where(lax.broadcasted_iota(jnp.int32, dt.shape, 0) < valid, dt, 0.0)
    a = dt * (-jnp.exp(alog_ref[...]))

    iq = lax.broadcasted_iota(jnp.int32, (t, t), 0)
    ik = lax.broadcasted_iota(jnp.int32, (t, t), 1)
    low = iq >= ik
    a_cs = _sel_left(tri_ref[...], a)

    a_last = a_cs[t - 1:t, :]
    stack = jnp.concatenate([dt, dt * jnp.exp(a_last - a_cs),
                             jnp.broadcast_to(jnp.exp(a_last), (SUBLANES, n_heads))], axis=0)
    ex = _spread(stack, sel_hp_ref[...])
    xdt_b = (xs * ex[0:t]).astype(BF16)
    xdst_b = (xs * ex[t:2 * t]).astype(BF16)
    chunk_decay = ex[2 * t:2 * t + 1]

    a_col = _spread(a_cs, sel_hk_ref[...])
    a_row = jnp.sum(a_col * diag_ref[...], axis=0, keepdims=True)
    ea_col = jnp.exp(a_col if t == d_st else _spread(a_cs, sel_hn_ref[...]))

    pair = LANES // pdim
    for g in range(n_grp):
        bg = bm[:, g * d_st:(g + 1) * d_st].astype(BF16)
        cg_f = cm[:, g * d_st:(g + 1) * d_st]
        cb = lax.dot_general(cg_f.astype(BF16), bg, _NT, preferred_element_type=F32)
        sg = state[g]
        sg_b = sg.astype(BF16)
        gl = slice(g * hp, (g + 1) * hp)
        for h0 in range(0, hpg, pair):
            outs = []
            for hh in range(g * hpg + h0, g * hpg + h0 + pair):
                ks = slice(hh * t, (hh + 1) * t)
                ns = slice(hh * d_st, (hh + 1) * d_st)
                ps = slice(hh * pdim, (hh + 1) * pdim)
                ps_g = slice(hh * pdim - g * hp, (hh + 1) * pdim - g * hp)
                decay = jnp.exp(jnp.where(low, a_col[:, ks] - a_row[:, ks], -jnp.inf))
                intra = (cb * decay).astype(BF16)
                carried = (cg_f * ea_col[:, ns]).astype(BF16)
                if t % LANES == 0:
                    outs.append(_dot(jnp.concatenate([intra, carried], axis=1),
                                     jnp.concatenate([xdt_b[:, ps], sg_b[:, ps_g]], axis=0)))
                else:
                    outs.append(_dot(intra, xdt_b[:, ps]) + _dot(carried, sg_b[:, ps_g]))
            lo = (g * hpg + h0) * pdim
            yacc[:, lo:lo + pair * pdim] = jnp.concatenate(outs, axis=1)
        upd = lax.dot_general(bg, xdst_b[:, gl], _TN, preferred_element_type=F32)
        state[g] = sg * chunk_decay[:, gl] + upd

    z = zx_ref[0, :, :d_inner]
    y = (yacc[...] + xs * dsk_ref[...]) * (z * _sigmoid(z))
    nw = nw_ref[...]
    for g in range(n_grp):
        gl = slice(g * hp, (g + 1) * hp)
        y_ref[0, :, gl] = _rms(y[:, gl], nw[:, gl]).astype(y_ref.dtype)

    @pl.when(c == pl.num_programs(1) - 1)
    def _fin():
        hfin_ref[0] = state[...]


def _ssd(zx, conv_buf, h0, conv_w, conv_b, dt_bias, a_log, d_skip, norm_w, *, t, valid):
    b, l, in_dim = zx.shape
    n_heads = a_log.shape[0]
    d_inner = n_heads * SSM_HEAD_DIM
    conv_dim = conv_w.shape[1]
    n_grp, d_st = SSM_N_GROUPS, SSM_D_STATE
    hp = d_inner // n_grp
    assert l % t == 0 and in_dim == d_inner + conv_dim + n_heads
    cbuf = jnp.pad(conv_buf, ((0, 0), (SUBLANES - (SSM_CONV - 1), 0), (0, 0)))
    h0t = h0.reshape(b, n_grp, n_heads // n_grp, SSM_HEAD_DIM, d_st)
    h0t = h0t.transpose(0, 1, 4, 2, 3).reshape(b, n_grp, d_st, hp)
    body = functools.partial(_ssd_kernel, t=t, valid=valid, d_inner=d_inner, n_heads=n_heads)
    tri = jnp.tril(jnp.ones((t, t), BF16))
    sel_hp = _spread_matrix(n_heads, SSM_HEAD_DIM)
    sel_hk = _spread_matrix(n_heads, t)
    sel_hn = _spread_matrix(n_heads, d_st)
    diag = jnp.tile(jnp.eye(t, dtype=F32), (1, n_heads))
    consts = (tri, sel_hp, sel_hk, sel_hn, diag)
    y, hfin = pl.pallas_call(
        body,
        grid=(b, l // t),
        in_specs=[
            pl.BlockSpec((1, t, in_dim), lambda i, c: (i, c, 0)),
            pl.BlockSpec((1, SUBLANES, conv_dim), lambda i, c: (i, 0, 0)),
            pl.BlockSpec((1, n_grp, d_st, hp), lambda i, c: (i, 0, 0, 0)),
            _resident((SSM_CONV, conv_dim)), _resident((1, conv_dim)),
            _resident((1, n_heads)), _resident((1, n_heads)),
            _resident((1, d_inner)), _resident((1, d_inner)),
        ] + [_resident(cst.shape) for cst in consts],
        out_specs=[
            pl.BlockSpec((1, t, d_inner), lambda i, c: (i, c, 0)),
            pl.BlockSpec((1, n_grp, d_st, hp), lambda i, c: (i, 0, 0, 0)),
        ],
        out_shape=[
            jax.ShapeDtypeStruct((b, l, d_inner), BF16),
            jax.ShapeDtypeStruct((b, n_grp, d_st, hp), F32),
        ],
        scratch_shapes=[
            pltpu.VMEM((SUBLANES, conv_dim), F32),
            pltpu.VMEM((n_grp, d_st, hp), F32),
            pltpu.VMEM((t, d_inner), F32),
        ],
        compiler_params=_cparams("parallel", "arbitrary"),
        name="ssd_scan",
    )(zx, cbuf, h0t, conv_w, conv_b.reshape(1, -1), dt_bias.reshape(1, -1), a_log.reshape(1, -1),
      jnp.repeat(d_skip, SSM_HEAD_DIM).reshape(1, -1), norm_w.reshape(1, -1), *consts)
    hfin = hfin.reshape(b, n_grp, d_st, n_heads // n_grp, SSM_HEAD_DIM)
    hfin = hfin.transpose(0, 1, 3, 4, 2).reshape(b, n_heads, SSM_HEAD_DIM, d_st)
    return y, hfin


def _gla_kernel(p_ref, s0_ref, wg_ref, gb_ref, nw_ref, o_ref, sfin_ref, state, oacc,
                *, tb, ch, valid, dk, dv):
    c = pl.program_id(1)
    n_h = GLA_N_HEADS
    hk, hv = dk // n_h, dv // n_h

    @pl.when(c == 0)
    def _init():
        state[...] = s0_ref[0]

    q = p_ref[0, :, 0:dk] * (hk ** -0.5)
    k = p_ref[0, :, dk:2 * dk]
    v = p_ref[0, :, 2 * dk:2 * dk + dv]
    r = p_ref[0, :, 2 * dk + dv:2 * dk + 2 * dv]
    g_low = p_ref[0, :, 2 * dk + 2 * dv:]
    x = _dot(g_low.astype(BF16), wg_ref[...]) + gb_ref[...]
    log_a = -_softplus(-x) / GLA_GATE_NORM
    if valid < tb:
        keep = lax.broadcasted_iota(jnp.int32, (tb, dk), 0) < valid
        log_a = jnp.where(keep, log_a, 0.0)
        k = jnp.where(keep, k, 0.0)

    ir = lax.broadcasted_iota(jnp.int32, (tb, tb), 0)
    ic = lax.broadcasted_iota(jnp.int32, (tb, tb), 1)
    causal = ir >= ic
    blocktri = jnp.where(causal & ((ir & -ch) == (ic & -ch)), 1.0, 0.0).astype(BF16)
    bcum = _sel_left(blocktri, log_a)
    q_t = (q * jnp.exp(bcum)).astype(BF16)
    k_t = (k * jnp.exp(-bcum)).astype(BF16)
    v_b = v.astype(BF16)
    causal_c = causal[0:ch, 0:ch]

    for j in range(tb // ch):
        rows = slice(j * ch, (j + 1) * ch)
        b_last = bcum[(j + 1) * ch - 1:(j + 1) * ch, :]
        k_dec = (k[rows] * jnp.exp(b_last - bcum[rows])).astype(BF16)
        e_last = jnp.exp(b_last)
        for h in range(n_h):
            ks = slice(h * hk, (h + 1) * hk)
            vs = slice(h * hv, (h + 1) * hv)
            qh, kh, vh = q_t[rows, ks], k_t[rows, ks], v_b[rows, vs]
            att = lax.dot_general(qh, kh, _NT, preferred_element_type=F32)
            att = jnp.where(causal_c, att, 0.0).astype(BF16)
            sh = state[h]
            oacc[rows, vs] = _dot(att, vh) + lax.dot_general(qh, sh.astype(BF16), _NT,
                                                           preferred_element_type=F32)
            upd = lax.dot_general(vh, k_dec[:, ks], _TN, preferred_element_type=F32)
            state[h] = sh * e_last[:, ks] + upd

    gate = r * _sigmoid(r)
    nw = nw_ref[...]
    for h in range(n_h):
        vs = slice(h * hv, (h + 1) * hv)
        o_ref[0, :, vs] = (_rms(oacc[:, vs], nw) * gate[:, vs]).astype(o_ref.dtype)

    @pl.when(c == pl.num_programs(1) - 1)
    def _fin():
        sfin_ref[0] = state[...]


def _gla(proj, s0, w_gate, gate_bias, norm_w, *, tb, ch, valid):
    b, l, in_dim = proj.shape
    n_h = GLA_N_HEADS
    rank, dk = w_gate.shape
    hv = norm_w.shape[0]
    dv = hv * n_h
    hk = dk // n_h
    assert l % tb == 0 and tb % ch == 0 and in_dim == 2 * dk + 2 * dv + rank
    body = functools.partial(_gla_kernel, tb=tb, ch=ch, valid=valid, dk=dk, dv=dv)
    o, sfin = pl.pallas_call(
        body,
        grid=(b, l // tb),
        in_specs=[
            pl.BlockSpec((1, tb, in_dim), lambda i, c: (i, c, 0)),
            pl.BlockSpec((1, n_h, hv, hk), lambda i, c: (i, 0, 0, 0)),
            _resident((rank, dk)), _resident((1, dk)), _resident((1, hv)),
        ],
        out_specs=[
            pl.BlockSpec((1, tb, dv), lambda i, c: (i, c, 0)),
            pl.BlockSpec((1, n_h, hv, hk), lambda i, c: (i, 0, 0, 0)),
        ],
        out_shape=[
            jax.ShapeDtypeStruct((b, l, dv), BF16),
            jax.ShapeDtypeStruct((b, n_h, hv, hk), F32),
        ],
        scratch_shapes=[pltpu.VMEM((n_h, hv, hk), F32), pltpu.VMEM((tb, dv), F32)],
        compiler_params=_cparams("parallel", "arbitrary"),
        name="gla_scan",
    )(proj, jnp.swapaxes(s0, -1, -2), w_gate.astype(BF16), gate_bias.reshape(1, -1),
      norm_w.reshape(1, -1))
    return o, jnp.swapaxes(sfin, -1, -2)


def _attn_prompt_kernel(q_ref, kp_ref, kc_ref, vp_ref, vc_ref, o_ref, lse_ref, *, dilation, nb, span):
    not_first = pl.program_id(2) > 0
    blk = ATT_BLOCK
    dh = ATT_HEAD_DIM
    units = [(r, jb) for r in range(dilation) for jb in range(nb)]
    n_u = len(units)
    iu = lax.broadcasted_iota(jnp.int32, (n_u, blk, blk), 0)
    iq = lax.broadcasted_iota(jnp.int32, (n_u, blk, blk), 1)
    ik = lax.broadcasted_iota(jnp.int32, (n_u, blk, blk), 2)
    ok_prev = (ik >= iq + (blk - span)) & (((iu & (nb - 1)) != 0) | not_first)
    ok_cur = ik <= iq
    scale = dh ** -0.5

    def rows(r, jb):
        start = r + dilation * jb * blk
        return pl.ds(start, blk) if dilation == 1 else pl.ds(start, blk, stride=dilation)

    def gather(cur_ref, prev_ref):
        cur = jnp.stack([cur_ref[0, rows(r, jb), :] for r, jb in units])
        prev = jnp.stack([prev_ref[0, rows(r, nb - 1), :] if jb == 0 else cur_ref[0, rows(r, jb - 1), :]
                          for r, jb in units])
        return cur.astype(BF16), prev.astype(BF16)

    q = jnp.stack([q_ref[0, rows(r, jb), :] for r, jb in units]).astype(BF16)
    k_cur, k_prev = gather(kc_ref, kp_ref)
    v_cur, v_prev = gather(vc_ref, vp_ref)
    outs, lses = [], []
    for hh in range(LANES // dh):
        hs = slice(hh * dh, (hh + 1) * dh)
        qh = q[:, :, hs]
        s_p = jnp.einsum("uqd,ukd->uqk", qh, k_prev[:, :, hs], preferred_element_type=F32) * scale
        s_c = jnp.einsum("uqd,ukd->uqk", qh, k_cur[:, :, hs], preferred_element_type=F32) * scale
        s_p = jnp.where(ok_prev, s_p, -jnp.inf)
        s_c = jnp.where(ok_cur, s_c, -jnp.inf)
        mx = jnp.max(jnp.maximum(s_p, s_c), axis=-1, keepdims=True)
        p_p = jnp.exp(s_p - mx)
        p_c = jnp.exp(s_c - mx)
        den = jnp.sum(p_p + p_c, axis=-1, keepdims=True)
        outs.append(jnp.einsum("uqk,ukd->uqd", (p_p / den).astype(BF16), v_prev[:, :, hs],
                               preferred_element_type=F32)
                    + jnp.einsum("uqk,ukd->uqd", (p_c / den).astype(BF16), v_cur[:, :, hs],
                                 preferred_element_type=F32))
        lses.append(jnp.broadcast_to(mx + jnp.log(den), (n_u, blk, dh)))
    o = jnp.concatenate(outs, axis=-1)
    lse = jnp.concatenate(lses, axis=-1)
    for u, (r, jb) in enumerate(units):
        o_ref[0, rows(r, jb), :] = o[u]
        lse_ref[0, rows(r, jb), :] = lse[u]


def _attn_prompt(qkv, gi, *, b, l, window, dilation):
    n_slab = qkv.shape[0] // 3
    spg = ATT_HEADS_PER_GROUP * ATT_HEAD_DIM // LANES
    span = window // dilation
    blk = ATT_BLOCK
    nb = max(1, min(l, ATT_SUPER_ROWS) // (dilation * blk))
    r_rows = dilation * nb * blk
    n_sup = l // r_rows
    assert l % r_rows == 0 and span <= blk and qkv.shape[1] == b * l and nb & (nb - 1) == 0

    def spec(part, prev):
        def index(bi, sp, i):
            i = jnp.maximum(i - 1, 0) if prev else i
            return (part * n_slab + gi * spg + sp, bi * n_sup + i, 0)
        return pl.BlockSpec((1, r_rows, LANES), index)

    out_spec = pl.BlockSpec((1, r_rows, LANES), lambda bi, sp, i: (sp, bi * n_sup + i, 0))
    return pl.pallas_call(
        functools.partial(_attn_prompt_kernel, dilation=dilation, nb=nb, span=span),
        grid=(b, spg, n_sup),
        in_specs=[spec(0, False), spec(1, True), spec(1, False), spec(2, True), spec(2, False)],
        out_specs=[out_spec, out_spec],
        out_shape=[jax.ShapeDtypeStruct((spg, b * l, LANES), F32)] * 2,
        compiler_params=_cparams("parallel", "parallel", "parallel"),
        name="attn_prompt",
    )(qkv, qkv, qkv, qkv, qkv)


def _attn_sample_kernel(q_ref, kn_ref, vn_ref, cache_ref, o_ref, lse_ref, cout_ref,
                        *, window, dilation, n_new, seqs):
    dh = ATT_HEAD_DIM
    rows = seqs * n_new
    scale = dh ** -0.5
    row = lax.broadcasted_iota(jnp.int32, (rows, window), 0)
    col = lax.broadcasted_iota(jnp.int32, (rows, window), 1)
    sq = row & (n_new - 1)
    ok_cache = (col >= sq) & (((col - sq) & (dilation - 1)) == 0)
    row_n = lax.broadcasted_iota(jnp.int32, (rows, rows), 0)
    col_n = lax.broadcasted_iota(jnp.int32, (rows, rows), 1)
    back = (row_n & (n_new - 1)) - (col_n & (n_new - 1))
    same_res = (back >= 0) & ((back & (dilation - 1)) == 0)
    lane_t = lax.broadcasted_iota(jnp.int32, (2 * rows, LANES), 1)
    tok_t = lax.broadcasted_iota(jnp.int32, (2 * rows, LANES), 0)
    lane_o = lax.broadcasted_iota(jnp.int32, (dh, LANES), 1)
    zpad = jnp.zeros((rows, dh), F32)
    for bb in range(seqs):
        r0 = bb * n_new
        ok_new = same_res & (col_n >= r0) & (col_n < r0 + n_new)
        place = jnp.where(lane_t - (LANES - n_new) == tok_t - r0, 1.0, 0.0).astype(BF16)
        for h in range(ATT_HEADS_PER_GROUP):
            sl = h * dh // LANES
            hs = slice(h * dh % LANES, h * dh % LANES + dh)
            q8 = q_ref[sl][:, hs].astype(BF16)
            k_new, v_new = kn_ref[sl][:, hs], vn_ref[sl][:, hs]
            k_t, v_t = cache_ref[bb, 0, h], cache_ref[bb, 1, h]
            s_c = jnp.where(ok_cache, _dot(q8, k_t.astype(BF16)) * scale, -jnp.inf)
            s_n = lax.dot_general(q8, k_new.astype(BF16), _NT, preferred_element_type=F32) * scale
            s_n = jnp.where(ok_new, s_n, -jnp.inf)
            mx = jnp.maximum(jnp.max(s_c, axis=-1, keepdims=True), jnp.max(s_n, axis=-1, keepdims=True))
            p_c = jnp.exp(s_c - mx)
            p_n = jnp.exp(s_n - mx)
            den = jnp.sum(p_c, axis=-1, keepdims=True) + jnp.sum(p_n, axis=-1, keepdims=True)
            o = (lax.dot_general((p_c / den).astype(BF16), v_t.astype(BF16), _NT, preferred_element_type=F32)
                 + _dot((p_n / den).astype(BF16), v_new.astype(BF16)))
            o_ref[sl, r0:r0 + n_new, hs] = o[r0:r0 + n_new]
            lse_ref[sl, r0:r0 + n_new, hs] = jnp.broadcast_to(mx + jnp.log(den), (rows, dh))[r0:r0 + n_new]
            for kv, old, new in ((0, k_t, k_new), (1, v_t, v_new)):
                hi, mid, lo = _split3(jnp.concatenate([new, zpad], axis=0))
                tail = sum(lax.dot_general(part, place, _TN, preferred_element_type=F32)
                           for part in (hi, mid, lo))
                shifted = pltpu.roll(old, window - n_new, 1)
                last = jnp.where(lane_o >= LANES - n_new, tail, shifted[:, window - LANES:])
                if window > LANES:
                    cout_ref[bb, kv, h, :, :window - LANES] = shifted[:, :window - LANES]
                cout_ref[bb, kv, h, :, window - LANES:] = last


def _attn_sample(qkv, cache_t, gi, *, window, dilation, n_new):
    n_slab = qkv.shape[0] // 3
    spg = ATT_HEADS_PER_GROUP * ATT_HEAD_DIM // LANES
    b = cache_t.shape[0]
    seqs = SUBLANES // n_new
    rows = seqs * n_new
    assert rows == SUBLANES and b % seqs == 0 and qkv.shape[1] == b * n_new
    assert cache_t.shape[-1] == window and window % dilation == 0 and window % LANES == 0
    assert dilation & (dilation - 1) == 0 and n_new & (n_new - 1) == 0

    def slabs(part):
        return pl.BlockSpec((spg, rows, LANES), lambda i: (part * n_slab // spg + gi, i, 0))

    cache_spec = pl.BlockSpec((seqs,) + cache_t.shape[1:], lambda i: (i, 0, 0, 0, 0))
    out_spec = pl.BlockSpec((spg, rows, LANES), lambda i: (0, i, 0))
    return pl.pallas_call(
        functools.partial(_attn_sample_kernel, window=window, dilation=dilation, n_new=n_new, seqs=seqs),
        grid=(b // seqs,),
        in_specs=[slabs(0), slabs(1), slabs(2), cache_spec],
        out_specs=[out_spec, out_spec, cache_spec],
        out_shape=[jax.ShapeDtypeStruct((spg, b * n_new, LANES), F32)] * 2
        + [jax.ShapeDtypeStruct(cache_t.shape, F32)],
        compiler_params=_cparams("parallel"),
        name="attn_sample",
    )(qkv, qkv, qkv, cache_t)


def _attn_out_kernel(o0_ref, o1_ref, o2_ref, l0_ref, l1_ref, l2_ref, w_ref, x_ref, out_ref):
    o_refs, l_refs = (o0_ref, o1_ref, o2_ref), (l0_ref, l1_ref, l2_ref)
    spg = o0_ref.shape[0]
    pieces = [[None] * spg for _ in o_refs]
    for s in range(spg):
        ls = [l_ref[s] for l_ref in l_refs]
        mx = jnp.maximum(jnp.maximum(ls[0], ls[1]), ls[2])
        es = [jnp.exp(l - mx) for l in ls]
        den = es[0] + es[1] + es[2]
        for g, (o_ref, e) in enumerate(zip(o_refs, es)):
            pieces[g][s] = o_ref[s] * (e / den)
    mixed = jnp.concatenate([p for grp in pieces for p in grp], axis=-1).astype(BF16)
    out_ref[...] = x_ref[...] + _dot(mixed, w_ref[...])


def _attn_out(os_, lses, w, x, *, tm):
    m, d = x.shape
    spg = os_[0].shape[0]
    row = pl.BlockSpec((spg, tm, LANES), lambda i: (0, i, 0))
    return pl.pallas_call(
        _attn_out_kernel,
        grid=(m // tm,),
        in_specs=[row] * 6 + [_resident(w.shape), pl.BlockSpec((tm, d), lambda i: (i, 0))],
        out_specs=pl.BlockSpec((tm, d), lambda i: (i, 0)),
        out_shape=jax.ShapeDtypeStruct((m, d), F32),
        compiler_params=_cparams("parallel"),
        name="attn_mix_proj",
    )(*os_, *lses, w, x)


def _rope_tables(pos):
    half = ATT_ROT_DIM // 2
    inv_freq = ROPE_THETA ** (-jnp.arange(0, ATT_ROT_DIM, 2, dtype=F32) / ATT_ROT_DIM)
    ang = pos.astype(F32)[:, None] * inv_freq
    cos, sin = jnp.cos(ang), jnp.sin(ang)
    n = pos.shape[0]
    rest = ATT_HEAD_DIM - ATT_ROT_DIM
    one, zero, zh = jnp.ones((n, rest), F32), jnp.zeros((n, rest), F32), jnp.zeros((n, half), F32)
    reps = LANES // ATT_HEAD_DIM
    c = jnp.tile(jnp.concatenate([cos, cos, one], axis=1), (1, reps))
    s1 = jnp.tile(jnp.concatenate([zh, sin, zero], axis=1), (1, reps))
    s2 = jnp.tile(jnp.concatenate([-sin, zh, zero], axis=1), (1, reps))
    return c, s1, s2


def _pad_rows(t, rows):
    return jnp.pad(t, ((0, 0), (0, rows - t.shape[1]), (0, 0)))


def kernel(x_prompt, x_sample, state_ssm, state_ssm_conv, state_gla, cache_kv_g0, cache_kv_g1, cache_kv_g2,
           norm_mix, norm_ffn, ffn_gate, ffn_up, ffn_down,
           ssm_w_in, ssm_conv_w, ssm_conv_b, ssm_dt_bias, ssm_a_log, ssm_d, ssm_norm, ssm_w_out,
           gla_w_in, gla_w_gate, gla_gate_bias, gla_norm, gla_w_out,
           att_w_qkv, att_w_out, norm_final):
    bp, lp, d = x_prompt.shape
    bs, ls, _ = x_sample.shape
    depth = norm_mix.shape[0]
    mp, ms = bp * lp, bs * ls
    tm_p = 512 if mp % 512 == 0 else mp
    tm_s = ms
    xp = x_prompt.reshape(mp, d)
    xs = x_sample.reshape(ms, d)
    att_caches = (cache_kv_g0, cache_kv_g1, cache_kv_g2)
    hg_w = ATT_HEADS_PER_GROUP * ATT_HEAD_DIM
    n_att = hg_w * len(ATT_GROUPS)
    rope_p = _rope_tables(jnp.tile(jnp.arange(lp), bp))
    rope_s = _rope_tables(jnp.tile(PAST_LEN + jnp.arange(ls), bs))

    ssm_p, ssm_s, conv_p, conv_s, gla_p, gla_s = [], [], [], [], [], []
    kv_p, kv_s = [[], [], []], [[], [], []]
    for i in range(depth):
        m, j = i % N_MIXERS, i // N_MIXERS
        if m == 0:
            w_in = ssm_w_in[j].astype(BF16)
            conv_dim = ssm_conv_w.shape[2]
            d_inner = ssm_w_out.shape[1]
            wts = (ssm_conv_w[j], ssm_conv_b[j], ssm_dt_bias[j], ssm_a_log[j], ssm_d[j], ssm_norm[j])
            zx_p = _norm_matmul(xp, norm_mix[i], w_in, tm=tm_p).reshape(bp, lp, -1)
            zx_s = _norm_matmul(xs, norm_mix[i], w_in, tm=tm_s).reshape(bs, ls, -1)
            t_p = min(SSM_CHUNK, lp)
            y_p, h_p = _ssd(zx_p, jnp.zeros((bp, SSM_CONV - 1, conv_dim), F32),
                            jnp.zeros((bp,) + state_ssm.shape[2:], F32), *wts, t=t_p, valid=t_p)
            y_s, h_s = _ssd(_pad_rows(zx_s, SAMPLE_PAD), state_ssm_conv[j], state_ssm[j], *wts,
                            t=SAMPLE_PAD, valid=ls)
            y_s = y_s[:, :ls]
            xbc_p = zx_p[:, :, d_inner:d_inner + conv_dim]
            xbc_s = zx_s[:, :, d_inner:d_inner + conv_dim]
            conv_p.append(xbc_p[:, -(SSM_CONV - 1):])
            conv_s.append(jnp.concatenate([state_ssm_conv[j], xbc_s], axis=1)[:, -(SSM_CONV - 1):])
            ssm_p.append(h_p)
            ssm_s.append(h_s)
            w_out = ssm_w_out[j].astype(BF16)
            xp = _proj_res(y_p.reshape(mp, -1), w_out, xp, tm=tm_p)
            xs = _proj_res(y_s.reshape(ms, -1), w_out, xs, tm=tm_s)
        elif m == 1:
            w_in = gla_w_in[j].astype(BF16)
            wts = (gla_w_gate[j], gla_gate_bias[j], gla_norm[j])
            pr_p = _norm_matmul(xp, norm_mix[i], w_in, tm=tm_p).reshape(bp, lp, -1)
            pr_s = _norm_matmul(xs, norm_mix[i], w_in, tm=tm_s).reshape(bs, ls, -1)
            ch_p = min(GLA_CHUNK, lp)
            tb_p = 256 if lp % 256 == 0 else ch_p
            o_p, s_p = _gla(pr_p, jnp.zeros((bp,) + state_gla.shape[2:], F32), *wts,
                            tb=tb_p, ch=ch_p, valid=tb_p)
            o_s, s_s = _gla(_pad_rows(pr_s, SAMPLE_PAD), state_gla[j], *wts,
                            tb=SAMPLE_PAD, ch=SAMPLE_PAD, valid=ls)
            o_s = o_s[:, :ls]
            gla_p.append(s_p)
            gla_s.append(s_s)
            w_out = gla_w_out[j].astype(BF16)
            xp = _proj_res(o_p.reshape(mp, -1), w_out, xp, tm=tm_p)
            xs = _proj_res(o_s.reshape(ms, -1), w_out, xs, tm=tm_s)
        else:
            w_qkv = att_w_qkv[j].astype(BF16)
            qkv_p = _norm_matmul(xp, norm_mix[i], w_qkv, tm=tm_p, rope=(2 * n_att, rope_p))
            qkv_s = _norm_matmul(xs, norm_mix[i], w_qkv, tm=tm_s, rope=(2 * n_att, rope_s))
            n_slab = n_att // LANES
            spg = hg_w // LANES
            os_p, ls_p, os_s, ls_s = [], [], [], []
            for gi, (window, dilation) in enumerate(ATT_GROUPS):
                o, lse = _attn_prompt(qkv_p, gi, b=bp, l=lp, window=window, dilation=dilation)
                os_p.append(o)
                ls_p.append(lse)
                keep = min(window, lp)
                kv = jnp.stack([qkv_p[part * n_slab + gi * spg:part * n_slab + (gi + 1) * spg]
                                .reshape(spg, bp, lp, LANES)[:, :, lp - keep:] for part in (1, 2)])
                kv = kv.reshape(2, spg, bp, keep, LANES // ATT_HEAD_DIM, ATT_HEAD_DIM)
                kv_p[gi].append(kv.transpose(2, 3, 0, 1, 4, 5)
                                .reshape(bp, keep, 2, ATT_HEADS_PER_GROUP, ATT_HEAD_DIM))
                cache_t = att_caches[gi][j].transpose(0, 2, 3, 4, 1)
                o, lse, cache_new = _attn_sample(qkv_s, cache_t, gi, window=window, dilation=dilation, n_new=ls)
                os_s.append(o)
                ls_s.append(lse)
                kv_s[gi].append(cache_new.transpose(0, 4, 1, 2, 3))
            w_out = att_w_out[j].astype(BF16)
            xp = _attn_out(os_p, ls_p, w_out, xp, tm=tm_p)
            xs = _attn_out(os_s, ls_s, w_out, xs, tm=tm_s)
        wg, wu, wd = ffn_gate[i].astype(BF16), ffn_up[i].astype(BF16), ffn_down[i].astype(BF16)
        final_g = norm_final if i == depth - 1 else None
        xp = _ffn(xp, norm_ffn[i], wg, wu, wd, tm=tm_p, final_g=final_g)
        xs = _ffn(xs, norm_ffn[i], wg, wu, wd, tm=tm_s, final_g=final_g)
    y_prompt = xp.reshape(bp, lp, d)
    y_sample = xs.reshape(bs, ls, d)
    return (y_prompt, y_sample,
            jnp.stack(ssm_p), jnp.stack(ssm_s), jnp.stack(conv_p), jnp.stack(conv_s),
            jnp.stack(gla_p), jnp.stack(gla_s),
            jnp.stack(kv_p[0]), jnp.stack(kv_s[0]), jnp.stack(kv_p[1]), jnp.stack(kv_s[1]),
            jnp.stack(kv_p[2]), jnp.stack(kv_s[2]))
```

```python
import functools

import jax
import jax.numpy as jnp
from jax import lax
from jax.experimental import pallas as pl
from jax.experimental.pallas import tpu as pltpu

F32 = jnp.float32
BF16 = jnp.bfloat16

NORM_EPS = 1e-6
N_MIXERS = 3

SSM_HEAD_DIM = 64
SSM_N_GROUPS = 4
SSM_D_STATE = 128
SSM_CONV = 4
SSM_CHUNK = 128

GLA_N_HEADS = 4
GLA_GATE_NORM = 16.0
GLA_CHUNK = 32

ATT_GROUPS = ((128, 1), (512, 4), (2048, 16))
ATT_HEADS_PER_GROUP = 4
ATT_HEAD_DIM = 64
ATT_ROT_DIM = ATT_HEAD_DIM // 4
ROPE_THETA = 500000.0
ATT_BLOCK = 128
ATT_SUPER_ROWS = 2048
PAST_LEN = 8192

LANES = 128
SUBLANES = 8
SAMPLE_PAD = 16
VMEM_LIMIT = 56 * 1024 * 1024

_NT = (((1,), (1,)), ((), ()))
_TN = (((0,), (0,)), ((), ()))


def _cparams(*sem):
    return pltpu.CompilerParams(dimension_semantics=sem, vmem_limit_bytes=VMEM_LIMIT)


def _resident(shape):
    zeros = (0,) * len(shape)
    return pl.BlockSpec(shape, lambda *_: zeros, pipeline_mode=pl.Buffered(1))


def _layer(wl):
    w, layer = wl
    return pl.BlockSpec((None,) + w.shape[1:], lambda *_: (layer, 0, 0), pipeline_mode=pl.Buffered(1))


def _rms(x, g):
    inv = lax.rsqrt(jnp.mean(x * x, axis=-1, keepdims=True) + NORM_EPS)
    return x * inv * g


def _sigmoid(x):
    return 1.0 / (1.0 + jnp.exp(-x))


def _softplus(x):
    return jnp.maximum(x, 0.0) + jnp.log1p(jnp.exp(-jnp.abs(x)))


def _split3(v):
    hi = v.astype(BF16)
    r = v - hi.astype(F32)
    mid = r.astype(BF16)
    lo = (r - mid.astype(F32)).astype(BF16)
    return hi, mid, lo


def _dot(a, b):
    return jnp.dot(a, b, preferred_element_type=F32)


def _spread_matrix(n_blocks, width):
    k_dim = -(-3 * n_blocks // LANES) * LANES
    row = lax.broadcasted_iota(jnp.int32, (k_dim, n_blocks * width), 0)
    col = lax.broadcasted_iota(jnp.int32, (k_dim, n_blocks * width), 1)
    blk = row - jnp.where(row >= 2 * n_blocks, 2 * n_blocks, jnp.where(row >= n_blocks, n_blocks, 0))
    sel = (row < 3 * n_blocks) & (col >= blk * width) & (col < blk * width + width)
    return jnp.where(sel, 1.0, 0.0).astype(BF16)


def _spread(v, sel):
    rows, n_blocks = v.shape
    terms = [p.astype(F32) for p in _split3(v)]
    pad = jnp.zeros((rows, sel.shape[0] - 3 * n_blocks), F32)
    return _dot(jnp.concatenate(terms + [pad], axis=1).astype(BF16), sel)


def _sel_left(e, v):
    hi, mid, lo = _split3(v)
    return _dot(e, hi) + _dot(e, mid) + _dot(e, lo)


def _norm_matmul_kernel(x_ref, g_ref, w_ref, o_ref):
    h = _rms(x_ref[...], g_ref[...]).astype(BF16)
    o_ref[...] = _dot(h, w_ref[...])


def _norm_matmul_rope_kernel(x_ref, g_ref, w_ref, c_ref, s1_ref, s2_ref, o_ref, *, n_rot):
    h = _rms(x_ref[...], g_ref[...]).astype(BF16)
    res = _dot(h, w_ref[...])
    c, s1, s2 = c_ref[...], s1_ref[...], s2_ref[...]
    half = ATT_ROT_DIM // 2
    for j in range(o_ref.shape[0]):
        x = res[:, j * LANES:(j + 1) * LANES]
        if j < n_rot // LANES:
            x = x * c + pltpu.roll(x, half, 1) * s1 + pltpu.roll(x, LANES - half, 1) * s2
        o_ref[j] = x


def _norm_matmul(x, g, wl, *, tm, rope=None):
    m, d = x.shape
    n = wl[0].shape[2]
    assert m % tm == 0
    in_specs = [pl.BlockSpec((tm, d), lambda i: (i, 0)), _resident((1, d)), _layer(wl)]
    args = [x, g.reshape(1, d), wl[0]]
    if rope is None:
        body = _norm_matmul_kernel
        out_spec = pl.BlockSpec((tm, n), lambda i: (i, 0))
        out_shape = jax.ShapeDtypeStruct((m, n), F32)
    else:
        n_rot, tables = rope
        assert n % LANES == 0 and n_rot % LANES == 0
        body = functools.partial(_norm_matmul_rope_kernel, n_rot=n_rot)
        period = tables[0].shape[0] // tm
        assert tables[0].shape[0] % tm == 0 and (m // tm) % period == 0
        in_specs += [pl.BlockSpec((tm, LANES), lambda i: (i % period, 0))] * 3
        args += list(tables)
        out_spec = pl.BlockSpec((n // LANES, tm, LANES), lambda i: (0, i, 0))
        out_shape = jax.ShapeDtypeStruct((n // LANES, m, LANES), F32)
    return pl.pallas_call(
        body,
        grid=(m // tm,),
        in_specs=in_specs,
        out_specs=out_spec,
        out_shape=out_shape,
        compiler_params=_cparams("parallel"),
        name="norm_proj" if rope is None else "norm_proj_rope",
    )(*args)


def _proj_res_kernel(y_ref, w_ref, x_ref, o_ref):
    o_ref[...] = x_ref[...] + _dot(y_ref[...], w_ref[...])


def _proj_res(y, wl, x, *, tm):
    m, k = y.shape
    w = wl[0]
    d = w.shape[2]
    assert m % tm == 0
    return pl.pallas_call(
        _proj_res_kernel,
        grid=(m // tm,),
        in_specs=[pl.BlockSpec((tm, k), lambda i: (i, 0)), _layer(wl),
                  pl.BlockSpec((tm, d), lambda i: (i, 0))],
        out_specs=pl.BlockSpec((tm, d), lambda i: (i, 0)),
        out_shape=jax.ShapeDtypeStruct((m, d), F32),
        compiler_params=_cparams("parallel"),
        name="proj_res",
    )(y, w, x)


def _ffn_kernel(x_ref, g_ref, wg_ref, wu_ref, wd_ref, gf_ref, o_ref, *, final_norm):
    x = x_ref[...]
    h = _rms(x, g_ref[...]).astype(BF16)
    gate = _dot(h, wg_ref[...])
    up = _dot(h, wu_ref[...])
    act = (gate * _sigmoid(gate) * up).astype(BF16)
    y = x + _dot(act, wd_ref[...])
    o_ref[...] = _rms(y, gf_ref[...]) if final_norm else y


def _ffn(x, g, wg, wu, wd, layer, *, tm, final_g=None):
    m, d = x.shape
    f = wg.shape[2]
    assert m % tm == 0
    gf = g if final_g is None else final_g
    return pl.pallas_call(
        functools.partial(_ffn_kernel, final_norm=final_g is not None),
        grid=(m // tm,),
        in_specs=[pl.BlockSpec((tm, d), lambda i: (i, 0)), _resident((1, d)),
                  _layer((wg, layer)), _layer((wu, layer)), _layer((wd, layer)), _resident((1, d))],
        out_specs=pl.BlockSpec((tm, d), lambda i: (i, 0)),
        out_shape=jax.ShapeDtypeStruct((m, d), F32),
        compiler_params=_cparams("parallel"),
        name="swiglu",
    )(x, g.reshape(1, d), wg, wu, wd, gf.reshape(1, d))


def _ssd_kernel(zx_ref, cbuf_ref, h0_ref, cw_ref, cb_ref, dtb_ref, alog_ref, dsk_ref, nw_ref,
                tri_ref, sel_hp_ref, sel_hk_ref, sel_hn_ref, diag_ref,
                y_ref, hfin_ref, tail, state, yacc, *, t, valid, d_inner, n_heads):
    c = pl.program_id(1)
    n_grp = SSM_N_GROUPS
    d_st = SSM_D_STATE
    hp = d_inner // n_grp
    hpg = n_heads // n_grp
    pdim = SSM_HEAD_DIM
    bc_w = n_grp * d_st
    conv_dim = d_inner + 2 * bc_w

    @pl.when(c == 0)
    def _init():
        state[...] = h0_ref[0]
        tail[...] = cbuf_ref[0]

    xraw = zx_ref[0, :, d_inner:d_inner + conv_dim]
    groups = [tail[...]] + [xraw[r:r + SUBLANES] for r in range(0, t, SUBLANES)]
    cw = cw_ref[...]
    acc = cb_ref[...]
    row8 = lax.broadcasted_iota(jnp.int32, (SUBLANES, conv_dim), 0)
    for k in range(SSM_CONV - 1, 0, -1):
        rolled = [pltpu.roll(grp, k, 0) for grp in groups]
        shifted = jnp.concatenate([jnp.where(row8 < k, before, here)
                                   for before, here in zip(rolled[:-1], rolled[1:])], axis=0)
        acc = acc + shifted * cw[SSM_CONV - 1 - k:SSM_CONV - k, :]
    acc = acc + xraw * cw[SSM_CONV - 1:SSM_CONV, :]
    tail[...] = groups[-1]
    xbc = acc * _sigmoid(acc)
    xs = xbc[:, :d_inner]
    bm = xbc[:, d_inner:d_inner + bc_w]
    cm = xbc[:, d_inner + bc_w:]

    dt = _softplus(zx_ref[0, :, d_inner + conv_dim:] + dtb_ref[...])
    if valid < t:
        dt = jnp.where(lax.broadcasted_iota(jnp.int32, dt.shape, 0) < valid, dt, 0.0)
    a = dt * (-jnp.exp(alog_ref[...]))

    iq = lax.broadcasted_iota(jnp.int32, (t, t), 0)
    ik = lax.broadcasted_iota(jnp.int32, (t, t), 1)
    low = iq >= ik
    a_cs = _sel_left(tri_ref[...], a)

    a_last = a_cs[t - 1:t, :]
    stack = jnp.concatenate([dt, dt * jnp.exp(a_last - a_cs),
                             jnp.broadcast_to(jnp.exp(a_last), (SUBLANES, n_heads))], axis=0)
    ex = _spread(stack, sel_hp_ref[...])
    xdt_b = (xs * ex[0:t]).astype(BF16)
    xdst_b = (xs * ex[t:2 * t]).astype(BF16)
    chunk_decay = ex[2 * t:2 * t + 1]

    a_col = _spread(a_cs, sel_hk_ref[...])
    a_row = jnp.sum(a_col * diag_ref[...], axis=0, keepdims=True)
    ea_col = jnp.exp(a_col if t == d_st else _spread(a_cs, sel_hn_ref[...]))

    pair = LANES // pdim
    for g in range(n_grp):
        bg = bm[:, g * d_st:(g + 1) * d_st].astype(BF16)
        cg_f = cm[:, g * d_st:(g + 1) * d_st]
        cb = lax.dot_general(cg_f.astype(BF16), bg, _NT, preferred_element_type=F32)
        sg = state[g]
        sg_b = sg.astype(BF16)
        gl = slice(g * hp, (g + 1) * hp)
        for h0 in range(0, hpg, pair):
            outs = []
            for hh in range(g * hpg + h0, g * hpg + h0 + pair):
                ks = slice(hh * t, (hh + 1) * t)
                ns = slice(hh * d_st, (hh + 1) * d_st)
                ps = slice(hh * pdim, (hh + 1) * pdim)
                ps_g = slice(hh * pdim - g * hp, (hh + 1) * pdim - g * hp)
                decay = jnp.exp(jnp.where(low, a_col[:, ks] - a_row[:, ks], -jnp.inf))
                intra = (cb * decay).astype(BF16)
                carried = (cg_f * ea_col[:, ns]).astype(BF16)
                if t % LANES == 0:
                    outs.append(_dot(jnp.concatenate([intra, carried], axis=1),
                                     jnp.concatenate([xdt_b[:, ps], sg_b[:, ps_g]], axis=0)))
                else:
                    outs.append(_dot(intra, xdt_b[:, ps]) + _dot(carried, sg_b[:, ps_g]))
            lo = (g * hpg + h0) * pdim
            yacc[:, lo:lo + pair * pdim] = jnp.concatenate(outs, axis=1)
        upd = lax.dot_general(bg, xdst_b[:, gl], _TN, preferred_element_type=F32)
        state[g] = sg * chunk_decay[:, gl] + upd

    z = zx_ref[0, :, :d_inner]
    y = (yacc[...] + xs * dsk_ref[...]) * (z * _sigmoid(z))
    nw = nw_ref[...]
    for g in range(n_grp):
        gl = slice(g * hp, (g + 1) * hp)
        y_ref[0, :, gl] = _rms(y[:, gl], nw[:, gl]).astype(y_ref.dtype)

    @pl.when(c == pl.num_programs(1) - 1)
    def _fin():
        hfin_ref[0] = state[...]


def _ssd(zx, conv_buf, h0, conv_w, conv_b, dt_bias, a_log, d_skip, norm_w, *, t, valid):
    b, l, in_dim = zx.shape
    n_heads = a_log.shape[0]
    d_inner = n_heads * SSM_HEAD_DIM
    conv_dim = conv_w.shape[1]
    n_grp, d_st = SSM_N_GROUPS, SSM_D_STATE
    hp = d_inner // n_grp
    assert l % t == 0 and in_dim == d_inner + conv_dim + n_heads
    cbuf = jnp.pad(conv_buf, ((0, 0), (SUBLANES - (SSM_CONV - 1), 0), (0, 0)))
    h0t = h0.reshape(b, n_grp, n_heads // n_grp, SSM_HEAD_DIM, d_st)
    h0t = h0t.transpose(0, 1, 4, 2, 3).reshape(b, n_grp, d_st, hp)
    body = functools.partial(_ssd_kernel, t=t, valid=valid, d_inner=d_inner, n_heads=n_heads)
    tri = jnp.tril(jnp.ones((t, t), BF16))
    sel_hp = _spread_matrix(n_heads, SSM_HEAD_DIM)
    sel_hk = _spread_matrix(n_heads, t)
    sel_hn = _spread_matrix(n_heads, d_st)
    diag = jnp.tile(jnp.eye(t, dtype=F32), (1, n_heads))
    consts = (tri, sel_hp, sel_hk, sel_hn, diag)
    y, hfin = pl.pallas_call(
        body,
        grid=(b, l // t),
        in_specs=[
            pl.BlockSpec((1, t, in_dim), lambda i, c: (i, c, 0)),
            pl.BlockSpec((1, SUBLANES, conv_dim), lambda i, c: (i, 0, 0)),
            pl.BlockSpec((1, n_grp, d_st, hp), lambda i, c: (i, 0, 0, 0)),
            _resident((SSM_CONV, conv_dim)), _resident((1, conv_dim)),
            _resident((1, n_heads)), _resident((1, n_heads)),
            _resident((1, d_inner)), _resident((1, d_inner)),
        ] + [_resident(cst.shape) for cst in consts],
        out_specs=[
            pl.BlockSpec((1, t, d_inner), lambda i, c: (i, c, 0)),
            pl.BlockSpec((1, n_grp, d_st, hp), lambda i, c: (i, 0, 0, 0)),
        ],
        out_shape=[
            jax.ShapeDtypeStruct((b, l, d_inner), BF16),
            jax.ShapeDtypeStruct((b, n_grp, d_st, hp), F32),
        ],
        scratch_shapes=[
            pltpu.VMEM((SUBLANES, conv_dim), F32),
            pltpu.VMEM((n_grp, d_st, hp), F32),
            pltpu.VMEM((t, d_inner), F32),
        ],
        compiler_params=_cparams("parallel", "arbitrary"),
        name="ssd_scan",
    )(zx, cbuf, h0t, conv_w, conv_b.reshape(1, -1), dt_bias.reshape(1, -1), a_log.reshape(1, -1),
      jnp.repeat(d_skip, SSM_HEAD_DIM).reshape(1, -1), norm_w.reshape(1, -1), *consts)
    hfin = hfin.reshape(b, n_grp, d_st, n_heads // n_grp, SSM_HEAD_DIM)
    hfin = hfin.transpose(0, 1, 3, 4, 2).reshape(b, n_heads, SSM_HEAD_DIM, d_st)
    return y, hfin


def _gla_kernel(p_ref, s0_ref, wg_ref, gb_ref, nw_ref, tri_ref, o_ref, sfin_ref, state,
                *, tb, ch, valid, dk, dv):
    c = pl.program_id(1)
    n_h = GLA_N_HEADS
    hk, hv = dk // n_h, dv // n_h

    @pl.when(c == 0)
    def _init():
        state[...] = s0_ref[0]

    q = p_ref[0, :, 0:dk] * (hk ** -0.5)
    k = p_ref[0, :, dk:2 * dk]
    v = p_ref[0, :, 2 * dk:2 * dk + dv]
    r = p_ref[0, :, 2 * dk + dv:2 * dk + 2 * dv]
    g_low = p_ref[0, :, 2 * dk + 2 * dv:]
    x = _dot(g_low.astype(BF16), wg_ref[...]) + gb_ref[...]
    log_a = -_softplus(-x) / GLA_GATE_NORM
    if valid < tb:
        keep = lax.broadcasted_iota(jnp.int32, (tb, dk), 0) < valid
        log_a = jnp.where(keep, log_a, 0.0)
        k = jnp.where(keep, k, 0.0)

    bcum = _sel_left(tri_ref[...], log_a)
    nch = tb // ch

    def chunks(a):
        return a.reshape(nch, ch, a.shape[-1])

    bc3 = chunks(bcum)
    b_last = bc3[:, ch - 1:ch, :]
    q_t = chunks(q * jnp.exp(bcum)).astype(BF16)
    k_t = chunks(k * jnp.exp(-bcum)).astype(BF16)
    k_dec = (chunks(k) * jnp.exp(b_last - bc3)).astype(BF16)
    e_last = jnp.exp(b_last)
    v_b = chunks(v).astype(BF16)
    causal = (lax.broadcasted_iota(jnp.int32, (nch, ch, ch), 1)
              >= lax.broadcasted_iota(jnp.int32, (nch, ch, ch), 2))

    gate = r * _sigmoid(r)
    nw = nw_ref[...]
    for h in range(n_h):
        ks = slice(h * hk, (h + 1) * hk)
        vs = slice(h * hv, (h + 1) * hv)
        qh, kh, vh = q_t[:, :, ks], k_t[:, :, ks], v_b[:, :, vs]
        upd = jnp.einsum("jkv,jkd->jvd", vh, k_dec[:, :, ks], preferred_element_type=F32)
        s_run = state[h]
        entering = []
        for j in range(nch):
            entering.append(s_run.astype(BF16))
            s_run = s_run * e_last[j, :, ks] + upd[j]
        state[h] = s_run
        att = jnp.einsum("jqd,jkd->jqk", qh, kh, preferred_element_type=F32)
        att = jnp.where(causal, att, 0.0).astype(BF16)
        o = (jnp.einsum("jqk,jkv->jqv", att, vh, preferred_element_type=F32)
             + jnp.einsum("jqd,jvd->jqv", qh, jnp.stack(entering), preferred_element_type=F32))
        o_ref[0, :, vs] = (_rms(o.reshape(tb, hv), nw) * gate[:, vs]).astype(o_ref.dtype)

    @pl.when(c == pl.num_programs(1) - 1)
    def _fin():
        sfin_ref[0] = state[...]


def _gla(proj, s0, w_gate, gate_bias, norm_w, *, tb, ch, valid):
    b, l, in_dim = proj.shape
    n_h = GLA_N_HEADS
    rank, dk = w_gate.shape
    hv = norm_w.shape[0]
    dv = hv * n_h
    hk = dk // n_h
    assert l % tb == 0 and tb % ch == 0 and in_dim == 2 * dk + 2 * dv + rank
    body = functools.partial(_gla_kernel, tb=tb, ch=ch, valid=valid, dk=dk, dv=dv)
    blocktri = jnp.kron(jnp.eye(tb // ch, dtype=F32), jnp.tril(jnp.ones((ch, ch), F32))).astype(BF16)
    o, sfin = pl.pallas_call(
        body,
        grid=(b, l // tb),
        in_specs=[
            pl.BlockSpec((1, tb, in_dim), lambda i, c: (i, c, 0)),
            pl.BlockSpec((1, n_h, hv, hk), lambda i, c: (i, 0, 0, 0)),
            _resident((rank, dk)), _resident((1, dk)), _resident((1, hv)), _resident((tb, tb)),
        ],
        out_specs=[
            pl.BlockSpec((1, tb, dv), lambda i, c: (i, c, 0)),
            pl.BlockSpec((1, n_h, hv, hk), lambda i, c: (i, 0, 0, 0)),
        ],
        out_shape=[
            jax.ShapeDtypeStruct((b, l, dv), BF16),
            jax.ShapeDtypeStruct((b, n_h, hv, hk), F32),
        ],
        scratch_shapes=[pltpu.VMEM((n_h, hv, hk), F32)],
        compiler_params=_cparams("parallel", "arbitrary"),
        name="gla_scan",
    )(proj, jnp.swapaxes(s0, -1, -2), w_gate.astype(BF16), gate_bias.reshape(1, -1),
      norm_w.reshape(1, -1), blocktri)
    return o, jnp.swapaxes(sfin, -1, -2)


def _attn_prompt_kernel(q_ref, kp_ref, kc_ref, vp_ref, vc_ref, o_ref, lse_ref, *, dilation, nb, span):
    not_first = pl.program_id(2) > 0
    blk = ATT_BLOCK
    dh = ATT_HEAD_DIM
    units = [(r, jb) for r in range(dilation) for jb in range(nb)]
    n_u = len(units)
    iu = lax.broadcasted_iota(jnp.int32, (n_u, blk, blk), 0)
    iq = lax.broadcasted_iota(jnp.int32, (n_u, blk, blk), 1)
    ik = lax.broadcasted_iota(jnp.int32, (n_u, blk, blk), 2)
    ok_prev = (ik >= iq + (blk - span)) & (((iu & (nb - 1)) != 0) | not_first)
    ok_cur = ik <= iq
    scale = dh ** -0.5

    def rows(r, jb):
        start = r + dilation * jb * blk
        return pl.ds(start, blk) if dilation == 1 else pl.ds(start, blk, stride=dilation)

    def gather(cur_ref, prev_ref):
        cur = jnp.stack([cur_ref[0, rows(r, jb), :] for r, jb in units])
        prev = jnp.stack([prev_ref[0, rows(r, nb - 1), :] if jb == 0 else cur_ref[0, rows(r, jb - 1), :]
                          for r, jb in units])
        return cur.astype(BF16), prev.astype(BF16)

    q = jnp.stack([q_ref[0, rows(r, jb), :] for r, jb in units]).astype(BF16)
    k_cur, k_prev = gather(kc_ref, kp_ref)
    v_cur, v_prev = gather(vc_ref, vp_ref)
    outs, lses = [], []
    for hh in range(LANES // dh):
        hs = slice(hh * dh, (hh + 1) * dh)
        qh = q[:, :, hs]
        s_p = jnp.einsum("uqd,ukd->uqk", qh, k_prev[:, :, hs], preferred_element_type=F32) * scale
        s_c = jnp.einsum("uqd,ukd->uqk", qh, k_cur[:, :, hs], preferred_element_type=F32) * scale
        s_p = jnp.where(ok_prev, s_p, -jnp.inf)
        s_c = jnp.where(ok_cur, s_c, -jnp.inf)
        mx = jnp.max(jnp.maximum(s_p, s_c), axis=-1, keepdims=True)
        p_p = jnp.exp(s_p - mx)
        p_c = jnp.exp(s_c - mx)
        den = jnp.sum(p_p + p_c, axis=-1, keepdims=True)
        outs.append(jnp.einsum("uqk,ukd->uqd", (p_p / den).astype(BF16), v_prev[:, :, hs],
                               preferred_element_type=F32)
                    + jnp.einsum("uqk,ukd->uqd", (p_c / den).astype(BF16), v_cur[:, :, hs],
                                 preferred_element_type=F32))
        lses.append(jnp.broadcast_to(mx + jnp.log(den), (n_u, blk, dh)))
    o = jnp.concatenate(outs, axis=-1)
    lse = jnp.concatenate(lses, axis=-1)
    for u, (r, jb) in enumerate(units):
        o_ref[0, rows(r, jb), :] = o[u]
        lse_ref[0, rows(r, jb), :] = lse[u]


def _attn_prompt(qkv, gi, *, b, l, window, dilation):
    n_slab = qkv.shape[0] // 3
    spg = ATT_HEADS_PER_GROUP * ATT_HEAD_DIM // LANES
    span = window // dilation
    blk = ATT_BLOCK
    nb = max(1, min(l, ATT_SUPER_ROWS) // (dilation * blk))
    r_rows = dilation * nb * blk
    n_sup = l // r_rows
    assert l % r_rows == 0 and span <= blk and qkv.shape[1] == b * l and nb & (nb - 1) == 0

    def spec(part, prev):
        def index(bi, sp, i):
            i = jnp.maximum(i - 1, 0) if prev else i
            return (part * n_slab + gi * spg + sp, bi * n_sup + i, 0)
        return pl.BlockSpec((1, r_rows, LANES), index)

    out_spec = pl.BlockSpec((1, r_rows, LANES), lambda bi, sp, i: (sp, bi * n_sup + i, 0))
    return pl.pallas_call(
        functools.partial(_attn_prompt_kernel, dilation=dilation, nb=nb, span=span),
        grid=(b, spg, n_sup),
        in_specs=[spec(0, False), spec(1, True), spec(1, False), spec(2, True), spec(2, False)],
        out_specs=[out_spec, out_spec],
        out_shape=[jax.ShapeDtypeStruct((spg, b * l, LANES), F32)] * 2,
        compiler_params=_cparams("parallel", "parallel", "parallel"),
        name="attn_prompt",
    )(qkv, qkv, qkv, qkv, qkv)


def _attn_sample_kernel(q_ref, kn_ref, vn_ref, cache_ref, o_ref, lse_ref, cout_ref,
                        *, window, dilation, n_new, seqs):
    dh = ATT_HEAD_DIM
    rows = seqs * n_new
    scale = dh ** -0.5
    row = lax.broadcasted_iota(jnp.int32, (rows, window), 0)
    col = lax.broadcasted_iota(jnp.int32, (rows, window), 1)
    sq = row & (n_new - 1)
    ok_cache = (col >= sq) & (((col - sq) & (dilation - 1)) == 0)
    row_n = lax.broadcasted_iota(jnp.int32, (rows, rows), 0)
    col_n = lax.broadcasted_iota(jnp.int32, (rows, rows), 1)
    back = (row_n & (n_new - 1)) - (col_n & (n_new - 1))
    same_res = (back >= 0) & ((back & (dilation - 1)) == 0)
    lane_t = lax.broadcasted_iota(jnp.int32, (2 * rows, LANES), 1)
    tok_t = lax.broadcasted_iota(jnp.int32, (2 * rows, LANES), 0)
    lane_o = lax.broadcasted_iota(jnp.int32, (dh, LANES), 1)
    zpad = jnp.zeros((rows, dh), F32)
    for bb in range(seqs):
        r0 = bb * n_new
        ok_new = same_res & (col_n >= r0) & (col_n < r0 + n_new)
        place = jnp.where(lane_t - (LANES - n_new) == tok_t - r0, 1.0, 0.0).astype(BF16)
        for h in range(ATT_HEADS_PER_GROUP):
            sl = h * dh // LANES
            hs = slice(h * dh % LANES, h * dh % LANES + dh)
            q8 = q_ref[sl][:, hs].astype(BF16)
            k_new, v_new = kn_ref[sl][:, hs], vn_ref[sl][:, hs]
            k_t, v_t = cache_ref[bb, 0, h], cache_ref[bb, 1, h]
            s_c = jnp.where(ok_cache, _dot(q8, k_t.astype(BF16)) * scale, -jnp.inf)
            s_n = lax.dot_general(q8, k_new.astype(BF16), _NT, preferred_element_type=F32) * scale
            s_n = jnp.where(ok_new, s_n, -jnp.inf)
            mx = jnp.maximum(jnp.max(s_c, axis=-1, keepdims=True), jnp.max(s_n, axis=-1, keepdims=True))
            p_c = jnp.exp(s_c - mx)
            p_n = jnp.exp(s_n - mx)
            den = jnp.sum(p_c, axis=-1, keepdims=True) + jnp.sum(p_n, axis=-1, keepdims=True)
            o = (lax.dot_general((p_c / den).astype(BF16), v_t.astype(BF16), _NT, preferred_element_type=F32)
                 + _dot((p_n / den).astype(BF16), v_new.astype(BF16)))
            o_ref[sl, r0:r0 + n_new, hs] = o[r0:r0 + n_new]
            lse_ref[sl, r0:r0 + n_new, hs] = jnp.broadcast_to(mx + jnp.log(den), (rows, dh))[r0:r0 + n_new]
            for kv, old, new in ((0, k_t, k_new), (1, v_t, v_new)):
                hi, mid, lo = _split3(jnp.concatenate([new, zpad], axis=0))
                tail = sum(lax.dot_general(part, place, _TN, preferred_element_type=F32)
                           for part in (hi, mid, lo))
                shifted = pltpu.roll(old, window - n_new, 1)
                last = jnp.where(lane_o >= LANES - n_new, tail, shifted[:, window - LANES:])
                if window > LANES:
                    cout_ref[bb, kv, h, :, :window - LANES] = shifted[:, :window - LANES]
                cout_ref[bb, kv, h, :, window - LANES:] = last


def _attn_sample(qkv, cache_t, gi, *, window, dilation, n_new):
    n_slab = qkv.shape[0] // 3
    spg = ATT_HEADS_PER_GROUP * ATT_HEAD_DIM // LANES
    b = cache_t.shape[0]
    seqs = SUBLANES // n_new
    rows = seqs * n_new
    assert rows == SUBLANES and b % seqs == 0 and qkv.shape[1] == b * n_new
    assert cache_t.shape[-1] == window and window % dilation == 0 and window % LANES == 0
    assert dilation & (dilation - 1) == 0 and n_new & (n_new - 1) == 0

    def slabs(part):
        return pl.BlockSpec((spg, rows, LANES), lambda i: (part * n_slab // spg + gi, i, 0))

    cache_spec = pl.BlockSpec((seqs,) + cache_t.shape[1:], lambda i: (i, 0, 0, 0, 0))
    out_spec = pl.BlockSpec((spg, rows, LANES), lambda i: (0, i, 0))
    return pl.pallas_call(
        functools.partial(_attn_sample_kernel, window=window, dilation=dilation, n_new=n_new, seqs=seqs),
        grid=(b // seqs,),
        in_specs=[slabs(0), slabs(1), slabs(2), cache_spec],
        out_specs=[out_spec, out_spec, cache_spec],
        out_shape=[jax.ShapeDtypeStruct((spg, b * n_new, LANES), F32)] * 2
        + [jax.ShapeDtypeStruct(cache_t.shape, F32)],
        compiler_params=_cparams("parallel"),
        name="attn_sample",
    )(qkv, qkv, qkv, cache_t)


def _attn_out_kernel(o0_ref, o1_ref, o2_ref, l0_ref, l1_ref, l2_ref, w_ref, x_ref, out_ref):
    o_refs, l_refs = (o0_ref, o1_ref, o2_ref), (l0_ref, l1_ref, l2_ref)
    spg = o0_ref.shape[0]
    pieces = [[None] * spg for _ in o_refs]
    for s in range(spg):
        ls = [l_ref[s] for l_ref in l_refs]
        mx = jnp.maximum(jnp.maximum(ls[0], ls[1]), ls[2])
        es = [jnp.exp(l - mx) for l in ls]
        den = es[0] + es[1] + es[2]
        for g, (o_ref, e) in enumerate(zip(o_refs, es)):
            pieces[g][s] = o_ref[s] * (e / den)
    mixed = jnp.concatenate([p for grp in pieces for p in grp], axis=-1).astype(BF16)
    out_ref[...] = x_ref[...] + _dot(mixed, w_ref[...])


def _attn_out(os_, lses, wl, x, *, tm):
    m, d = x.shape
    w = wl[0]
    spg = os_[0].shape[0]
    row = pl.BlockSpec((spg, tm, LANES), lambda i: (0, i, 0))
    return pl.pallas_call(
        _attn_out_kernel,
        grid=(m // tm,),
        in_specs=[row] * 6 + [_layer(wl), pl.BlockSpec((tm, d), lambda i: (i, 0))],
        out_specs=pl.BlockSpec((tm, d), lambda i: (i, 0)),
        out_shape=jax.ShapeDtypeStruct((m, d), F32),
        compiler_params=_cparams("parallel"),
        name="attn_mix_proj",
    )(*os_, *lses, w, x)


def _rope_tables(pos):
    half = ATT_ROT_DIM // 2
    lane = lax.broadcasted_iota(jnp.int32, (pos.shape[0], LANES), 1) % ATT_HEAD_DIM
    x1, rot = lane < half, lane < ATT_ROT_DIM
    freq = jnp.where(x1, lane, lane - half)
    inv_freq = ROPE_THETA ** (-(2 * freq).astype(F32) / ATT_ROT_DIM)
    ang = pos.astype(F32)[:, None] * inv_freq
    cos, sin = jnp.cos(ang), jnp.sin(ang)
    c = jnp.where(rot, cos, 1.0)
    s1 = jnp.where(rot & jnp.logical_not(x1), sin, 0.0)
    s2 = jnp.where(x1, -sin, 0.0)
    return c, s1, s2


def _pad_rows(t, rows):
    return jnp.pad(t, ((0, 0), (0, rows - t.shape[1]), (0, 0)))


def kernel(x_prompt, x_sample, state_ssm, state_ssm_conv, state_gla, cache_kv_g0, cache_kv_g1, cache_kv_g2,
           norm_mix, norm_ffn, ffn_gate, ffn_up, ffn_down,
           ssm_w_in, ssm_conv_w, ssm_conv_b, ssm_dt_bias, ssm_a_log, ssm_d, ssm_norm, ssm_w_out,
           gla_w_in, gla_w_gate, gla_gate_bias, gla_norm, gla_w_out,
           att_w_qkv, att_w_out, norm_final):
    bp, lp, d = x_prompt.shape
    bs, ls, _ = x_sample.shape
    depth = norm_mix.shape[0]
    mp, ms = bp * lp, bs * ls
    tm_p = 512 if mp % 512 == 0 else mp
    tm_s = ms
    xp = x_prompt.reshape(mp, d)
    xs = x_sample.reshape(ms, d)
    att_caches = (cache_kv_g0, cache_kv_g1, cache_kv_g2)
    hg_w = ATT_HEADS_PER_GROUP * ATT_HEAD_DIM
    n_att = hg_w * len(ATT_GROUPS)
    rope_p = _rope_tables(jnp.arange(lp))
    rope_s = _rope_tables(jnp.tile(PAST_LEN + jnp.arange(ls), bs))
    ffn_w = tuple(w.astype(BF16) for w in (ffn_gate, ffn_up, ffn_down))
    ssm_w_in, ssm_w_out, gla_w_in, gla_w_out, att_w_qkv, att_w_out = (
        w.astype(BF16) for w in (ssm_w_in, ssm_w_out, gla_w_in, gla_w_out, att_w_qkv, att_w_out))

    ssm_p, ssm_s, conv_p, conv_s, gla_p, gla_s = [], [], [], [], [], []
    kv_p, kv_s = [[], [], []], [[], [], []]
    for i in range(depth):
        m, j = i % N_MIXERS, i // N_MIXERS
        if m == 0:
            w_in = (ssm_w_in, j)
            conv_dim = ssm_conv_w.shape[2]
            d_inner = ssm_w_out.shape[1]
            wts = (ssm_conv_w[j], ssm_conv_b[j], ssm_dt_bias[j], ssm_a_log[j], ssm_d[j], ssm_norm[j])
            zx_p = _norm_matmul(xp, norm_mix[i], w_in, tm=tm_p).reshape(bp, lp, -1)
            zx_s = _norm_matmul(xs, norm_mix[i], w_in, tm=tm_s).reshape(bs, ls, -1)
            t_p = min(SSM_CHUNK, lp)
            y_p, h_p = _ssd(zx_p, jnp.zeros((bp, SSM_CONV - 1, conv_dim), F32),
                            jnp.zeros((bp,) + state_ssm.shape[2:], F32), *wts, t=t_p, valid=t_p)
            y_s, h_s = _ssd(_pad_rows(zx_s, SAMPLE_PAD), state_ssm_conv[j], state_ssm[j], *wts,
                            t=SAMPLE_PAD, valid=ls)
            y_s = y_s[:, :ls]
            xbc_p = zx_p[:, :, d_inner:d_inner + conv_dim]
            xbc_s = zx_s[:, :, d_inner:d_inner + conv_dim]
            conv_p.append(xbc_p[:, -(SSM_CONV - 1):])
            conv_s.append(jnp.concatenate([state_ssm_conv[j], xbc_s], axis=1)[:, -(SSM_CONV - 1):])
            ssm_p.append(h_p)
            ssm_s.append(h_s)
            w_out = (ssm_w_out, j)
            xp = _proj_res(y_p.reshape(mp, -1), w_out, xp, tm=tm_p)
            xs = _proj_res(y_s.reshape(ms, -1), w_out, xs, tm=tm_s)
        elif m == 1:
            w_in = (gla_w_in, j)
            wts = (gla_w_gate[j], gla_gate_bias[j], gla_norm[j])
            pr_p = _norm_matmul(xp, norm_mix[i], w_in, tm=tm_p).reshape(bp, lp, -1)
            pr_s = _norm_matmul(xs, norm_mix[i], w_in, tm=tm_s).reshape(bs, ls, -1)
            ch_p = min(GLA_CHUNK, lp)
            tb_p = 256 if lp % 256 == 0 else ch_p
            o_p, s_p = _gla(pr_p, jnp.zeros((bp,) + state_gla.shape[2:], F32), *wts,
                            tb=tb_p, ch=ch_p, valid=tb_p)
            o_s, s_s = _gla(_pad_rows(pr_s, SAMPLE_PAD), state_gla[j], *wts,
                            tb=SAMPLE_PAD, ch=SAMPLE_PAD, valid=ls)
            o_s = o_s[:, :ls]
            gla_p.append(s_p)
            gla_s.append(s_s)
            w_out = (gla_w_out, j)
            xp = _proj_res(o_p.reshape(mp, -1), w_out, xp, tm=tm_p)
            xs = _proj_res(o_s.reshape(ms, -1), w_out, xs, tm=tm_s)
        else:
            w_qkv = (att_w_qkv, j)
            qkv_p = _norm_matmul(xp, norm_mix[i], w_qkv, tm=tm_p, rope=(2 * n_att, rope_p))
            qkv_s = _norm_matmul(xs, norm_mix[i], w_qkv, tm=tm_s, rope=(2 * n_att, rope_s))
            n_slab = n_att // LANES
            spg = hg_w // LANES
            qkv_p4 = qkv_p.reshape(3 * n_slab, bp, lp, LANES)
            os_p, ls_p, os_s, ls_s = [], [], [], []
            for gi, (window, dilation) in enumerate(ATT_GROUPS):
                o, lse = _attn_prompt(qkv_p, gi, b=bp, l=lp, window=window, dilation=dilation)
                os_p.append(o)
                ls_p.append(lse)
                keep = min(window, lp)
                kv = jnp.stack([qkv_p4[part * n_slab + gi * spg:part * n_slab + (gi + 1) * spg, :, lp - keep:]
                                for part in (1, 2)])
                kv = kv.reshape(2, spg, bp, keep, LANES // ATT_HEAD_DIM, ATT_HEAD_DIM)
                kv_p[gi].append(kv.transpose(2, 3, 0, 1, 4, 5)
                                .reshape(bp, keep, 2, ATT_HEADS_PER_GROUP, ATT_HEAD_DIM))
                cache_t = att_caches[gi][j].transpose(0, 2, 3, 4, 1)
                o, lse, cache_new = _attn_sample(qkv_s, cache_t, gi, window=window, dilation=dilation, n_new=ls)
                os_s.append(o)
                ls_s.append(lse)
                kv_s[gi].append(cache_new.transpose(0, 4, 1, 2, 3))
            w_out = (att_w_out, j)
            xp = _attn_out(os_p, ls_p, w_out, xp, tm=tm_p)
            xs = _attn_out(os_s, ls_s, w_out, xs, tm=tm_s)
        final_g = norm_final if i == depth - 1 else None
        xp = _ffn(xp, norm_ffn[i], *ffn_w, i, tm=tm_p, final_g=final_g)
        xs = _ffn(xs, norm_ffn[i], *ffn_w, i, tm=tm_s, final_g=final_g)
    y_prompt = xp.reshape(bp, lp, d)
    y_sample = xs.reshape(bs, ls, d)
    return (y_prompt, y_sample,
            jnp.stack(ssm_p), jnp.stack(ssm_s), jnp.stack(conv_p), jnp.stack(conv_s),
            jnp.stack(gla_p), jnp.stack(gla_s),
            jnp.stack(kv_p[0]), jnp.stack(kv_s[0]), jnp.stack(kv_p[1]), jnp.stack(kv_s[1]),
            jnp.stack(kv_p[2]), jnp.stack(kv_s[2]))
```

```python
import functools

import jax
import jax.numpy as jnp
from jax import lax
from jax.experimental import pallas as pl
from jax.experimental.pallas import tpu as pltpu

F32 = jnp.float32
BF16 = jnp.bfloat16

NORM_EPS = 1e-6
N_MIXERS = 3

SSM_HEAD_DIM = 64
SSM_N_GROUPS = 4
SSM_D_STATE = 128
SSM_CONV = 4
SSM_CHUNK = 128
SSM_IN_PROJ_CHUNKS = 4

GLA_N_HEADS = 4
GLA_GATE_NORM = 16.0
GLA_CHUNK = 32

ATT_GROUPS = ((128, 1), (512, 4), (2048, 16))
ATT_HEADS_PER_GROUP = 4
ATT_HEAD_DIM = 64
ATT_ROT_DIM = ATT_HEAD_DIM // 4
ROPE_THETA = 500000.0
ATT_BLOCK = 128
ATT_SUPER_ROWS = 2048
PAST_LEN = 8192

LANES = 128
SUBLANES = 8
SAMPLE_PAD = 16
VMEM_LIMIT = 56 * 1024 * 1024

_NT = (((1,), (1,)), ((), ()))
_TN = (((0,), (0,)), ((), ()))


def _cparams(*sem):
    return pltpu.CompilerParams(dimension_semantics=sem, vmem_limit_bytes=VMEM_LIMIT)


def _resident(shape):
    zeros = (0,) * len(shape)
    return pl.BlockSpec(shape, lambda *_: zeros, pipeline_mode=pl.Buffered(1))


def _layer(wl):
    w, layer = wl
    return pl.BlockSpec((None,) + w.shape[1:], lambda *_: (layer, 0, 0), pipeline_mode=pl.Buffered(1))


def _rms(x, g):
    inv = lax.rsqrt(jnp.mean(x * x, axis=-1, keepdims=True) + NORM_EPS)
    return x * inv * g


def _sigmoid(x):
    return 1.0 / (1.0 + jnp.exp(-x))


def _softplus(x):
    return jnp.maximum(x, 0.0) + jnp.log1p(jnp.exp(-jnp.abs(x)))


def _split3(v):
    hi = v.astype(BF16)
    r = v - hi.astype(F32)
    mid = r.astype(BF16)
    lo = (r - mid.astype(F32)).astype(BF16)
    return hi, mid, lo


def _dot(a, b):
    return jnp.dot(a, b, preferred_element_type=F32)


def _spread_matrix(n_blocks, width):
    k_dim = -(-3 * n_blocks // LANES) * LANES
    row = lax.broadcasted_iota(jnp.int32, (k_dim, n_blocks * width), 0)
    col = lax.broadcasted_iota(jnp.int32, (k_dim, n_blocks * width), 1)
    blk = row - jnp.where(row >= 2 * n_blocks, 2 * n_blocks, jnp.where(row >= n_blocks, n_blocks, 0))
    sel = (row < 3 * n_blocks) & (col >= blk * width) & (col < blk * width + width)
    return jnp.where(sel, 1.0, 0.0).astype(BF16)


def _spread(v, sel):
    rows, n_blocks = v.shape
    terms = [p.astype(F32) for p in _split3(v)]
    pad = jnp.zeros((rows, sel.shape[0] - 3 * n_blocks), F32)
    return _dot(jnp.concatenate(terms + [pad], axis=1).astype(BF16), sel)


def _sel_left(e, v):
    hi, mid, lo = _split3(v)
    return _dot(e, hi) + _dot(e, mid) + _dot(e, lo)


def _norm_matmul_kernel(x_ref, g_ref, w_ref, o_ref):
    h = _rms(x_ref[...], g_ref[...]).astype(BF16)
    o_ref[...] = _dot(h, w_ref[...])


def _norm_matmul_rope_kernel(x_ref, g_ref, w_ref, c_ref, s1_ref, s2_ref, o_ref, *, n_rot):
    h = _rms(x_ref[...], g_ref[...]).astype(BF16)
    res = _dot(h, w_ref[...])
    c, s1, s2 = c_ref[...], s1_ref[...], s2_ref[...]
    half = ATT_ROT_DIM // 2
    for j in range(o_ref.shape[0]):
        x = res[:, j * LANES:(j + 1) * LANES]
        if j < n_rot // LANES:
            x = x * c + pltpu.roll(x, half, 1) * s1 + pltpu.roll(x, LANES - half, 1) * s2
        o_ref[j] = x


def _norm_matmul(x, g, wl, *, tm, rope=None):
    m, d = x.shape
    n = wl[0].shape[2]
    assert m % tm == 0
    in_specs = [pl.BlockSpec((tm, d), lambda i: (i, 0)), _resident((1, d)), _layer(wl)]
    args = [x, g.reshape(1, d), wl[0]]
    if rope is None:
        body = _norm_matmul_kernel
        out_spec = pl.BlockSpec((tm, n), lambda i: (i, 0))
        out_shape = jax.ShapeDtypeStruct((m, n), F32)
    else:
        n_rot, tables = rope
        assert n % LANES == 0 and n_rot % LANES == 0
        body = functools.partial(_norm_matmul_rope_kernel, n_rot=n_rot)
        period = tables[0].shape[0] // tm
        assert tables[0].shape[0] % tm == 0 and (m // tm) % period == 0
        in_specs += [pl.BlockSpec((tm, LANES), lambda i: (i % period, 0))] * 3
        args += list(tables)
        out_spec = pl.BlockSpec((n // LANES, tm, LANES), lambda i: (0, i, 0))
        out_shape = jax.ShapeDtypeStruct((n // LANES, m, LANES), F32)
    return pl.pallas_call(
        body,
        grid=(m // tm,),
        in_specs=in_specs,
        out_specs=out_spec,
        out_shape=out_shape,
        compiler_params=_cparams("parallel"),
        name="norm_proj" if rope is None else "norm_proj_rope",
    )(*args)


def _conv_silu(raw, before, cw, cb):
    t, c = raw.shape
    groups = [before] + [raw[r:r + SUBLANES] for r in range(0, t, SUBLANES)]
    row8 = lax.broadcasted_iota(jnp.int32, (SUBLANES, c), 0)
    acc = cb
    for k in range(SSM_CONV - 1, 0, -1):
        rolled = [pltpu.roll(grp, k, 0) for grp in groups]
        shifted = jnp.concatenate([jnp.where(row8 < k, prev, here)
                                   for prev, here in zip(rolled[:-1], rolled[1:])], axis=0)
        acc = acc + shifted * cw[SSM_CONV - 1 - k:SSM_CONV - k, :]
    acc = acc + raw * cw[SSM_CONV - 1:SSM_CONV, :]
    return acc * _sigmoid(acc), groups[-1]


def _ssd_in_proj_kernel(x_ref, g_ref, w_ref, cbuf_ref, cw_ref, cb_ref, o_ref, tail_ref, tail,
                        *, d_inner, conv_dim, tiles_per_seq, col_chunk):
    @pl.when(pl.program_id(0) % tiles_per_seq == 0)
    def _start_of_sequence():
        tail[...] = cbuf_ref[0]

    h = _rms(x_ref[...], g_ref[...]).astype(BF16)
    n_chunks = conv_dim // col_chunk
    z_chunk = d_inner // n_chunks
    for j in range(n_chunks):
        c0 = j * col_chunk
        cs = slice(c0, c0 + col_chunk)
        raw = _dot(h, w_ref[:, d_inner + c0:d_inner + c0 + col_chunk])
        zs = slice(j * z_chunk, (j + 1) * z_chunk)
        o_ref[:, zs] = _dot(h, w_ref[:, zs])
        act, last = _conv_silu(raw, tail[:, cs], cw_ref[:, cs], cb_ref[:, cs])
        o_ref[:, d_inner + c0:d_inner + c0 + col_chunk] = act
        tail[:, cs] = last
        tail_ref[0, :, cs] = last
    o_ref[:, d_inner + conv_dim:] = _dot(h, w_ref[:, d_inner + conv_dim:])


def _ssd_in_proj(x, g, wl, conv_buf, conv_w, conv_b, *, tm, seq_len, d_inner):
    m, d = x.shape
    n = wl[0].shape[2]
    conv_dim = conv_w.shape[1]
    assert m % tm == 0 and seq_len % tm == 0
    cbuf = jnp.pad(conv_buf, ((0, 0), (SUBLANES - (SSM_CONV - 1), 0), (0, 0)))
    tiles_per_seq = seq_len // tm
    body = functools.partial(_ssd_in_proj_kernel, d_inner=d_inner, conv_dim=conv_dim,
                             tiles_per_seq=tiles_per_seq, col_chunk=conv_dim // SSM_IN_PROJ_CHUNKS)
    assert conv_dim % (SSM_IN_PROJ_CHUNKS * LANES) == 0 and d_inner % (SSM_IN_PROJ_CHUNKS * LANES) == 0
    return pl.pallas_call(
        body,
        grid=(m // tm,),
        in_specs=[pl.BlockSpec((tm, d), lambda i: (i, 0)), _resident((1, d)), _layer(wl),
                  pl.BlockSpec((1, SUBLANES, conv_dim), lambda i: (i // tiles_per_seq, 0, 0)),
                  _resident((SSM_CONV, conv_dim)), _resident((1, conv_dim))],
        out_specs=[pl.BlockSpec((tm, n), lambda i: (i, 0)),
                   pl.BlockSpec((1, SUBLANES, conv_dim), lambda i: (i, 0, 0))],
        out_shape=[jax.ShapeDtypeStruct((m, n), F32),
                   jax.ShapeDtypeStruct((m // tm, SUBLANES, conv_dim), F32)],
        scratch_shapes=[pltpu.VMEM((SUBLANES, conv_dim), F32)],
        compiler_params=_cparams("arbitrary"),
        name="ssd_in_proj",
    )(x, g.reshape(1, d), wl[0], cbuf, conv_w, conv_b.reshape(1, -1))


def _proj_res_kernel(y_ref, w_ref, x_ref, o_ref):
    o_ref[...] = x_ref[...] + _dot(y_ref[...], w_ref[...])


def _proj_res(y, wl, x, *, tm):
    m, k = y.shape
    w = wl[0]
    d = w.shape[2]
    assert m % tm == 0
    return pl.pallas_call(
        _proj_res_kernel,
        grid=(m // tm,),
        in_specs=[pl.BlockSpec((tm, k), lambda i: (i, 0)), _layer(wl),
                  pl.BlockSpec((tm, d), lambda i: (i, 0))],
        out_specs=pl.BlockSpec((tm, d), lambda i: (i, 0)),
        out_shape=jax.ShapeDtypeStruct((m, d), F32),
        compiler_params=_cparams("parallel"),
        name="proj_res",
    )(y, w, x)


def _ffn_kernel(x_ref, g_ref, wg_ref, wu_ref, wd_ref, gf_ref, o_ref, *, final_norm):
    x = x_ref[...]
    h = _rms(x, g_ref[...]).astype(BF16)
    gate = _dot(h, wg_ref[...])
    up = _dot(h, wu_ref[...])
    act = (gate * _sigmoid(gate) * up).astype(BF16)
    y = x + _dot(act, wd_ref[...])
    o_ref[...] = _rms(y, gf_ref[...]) if final_norm else y


def _ffn(x, g, wg, wu, wd, layer, *, tm, final_g=None):
    m, d = x.shape
    f = wg.shape[2]
    assert m % tm == 0
    gf = g if final_g is None else final_g
    return pl.pallas_call(
        functools.partial(_ffn_kernel, final_norm=final_g is not None),
        grid=(m // tm,),
        in_specs=[pl.BlockSpec((tm, d), lambda i: (i, 0)), _resident((1, d)),
                  _layer((wg, layer)), _layer((wu, layer)), _layer((wd, layer)), _resident((1, d))],
        out_specs=pl.BlockSpec((tm, d), lambda i: (i, 0)),
        out_shape=jax.ShapeDtypeStruct((m, d), F32),
        compiler_params=_cparams("parallel"),
        name="swiglu",
    )(x, g.reshape(1, d), wg, wu, wd, gf.reshape(1, d))


def _ssd_kernel(zx_ref, cbuf_ref, h0_ref, cw_ref, cb_ref, dtb_ref, alog_ref, dsk_ref, nw_ref,
                tri_ref, sel_hp_ref, sel_hk_ref, sel_hn_ref, diag_ref,
                y_ref, hfin_ref, tail, state, yacc, *, t, valid, d_inner, n_heads, conv_done):
    c = pl.program_id(1)
    n_grp = SSM_N_GROUPS
    d_st = SSM_D_STATE
    hp = d_inner // n_grp
    hpg = n_heads // n_grp
    pdim = SSM_HEAD_DIM
    bc_w = n_grp * d_st
    conv_dim = d_inner + 2 * bc_w

    @pl.when(c == 0)
    def _init():
        state[...] = h0_ref[0]
        tail[...] = cbuf_ref[0]

    xbc = zx_ref[0, :, d_inner:d_inner + conv_dim]
    if not conv_done:
        xbc, tail[...] = _conv_silu(xbc, tail[...], cw_ref[...], cb_ref[...])
    xs = xbc[:, :d_inner]
    bm = xbc[:, d_inner:d_inner + bc_w]
    cm = xbc[:, d_inner + bc_w:]

    dt = _softplus(zx_ref[0, :, d_inner + conv_dim:] + dtb_ref[...])
    if valid < t:
        dt = jnp.where(lax.broadcasted_iota(jnp.int32, dt.shape, 0) < valid, dt, 0.0)
    a = dt * (-jnp.exp(alog_ref[...]))

    iq = lax.broadcasted_iota(jnp.int32, (t, t), 0)
    ik = lax.broadcasted_iota(jnp.int32, (t, t), 1)
    low = iq >= ik
    a_cs = _sel_left(tri_ref[...], a)

    a_last = a_cs[t - 1:t, :]
    stack = jnp.concatenate([dt, dt * jnp.exp(a_last - a_cs),
                             jnp.broadcast_to(jnp.exp(a_last), (SUBLANES, n_heads))], axis=0)
    ex = _spread(stack, sel_hp_ref[...])
    xdt_b = (xs * ex[0:t]).astype(BF16)
    xdst_b = (xs * ex[t:2 * t]).astype(BF16)
    chunk_decay = ex[2 * t:2 * t + 1]

    a_col = _spread(a_cs, sel_hk_ref[...])
    a_row = jnp.sum(a_col * diag_ref[...], axis=0, keepdims=True)
    ea_col = jnp.exp(a_col if t == d_st else _spread(a_cs, sel_hn_ref[...]))

    pair = LANES // pdim
    for g in range(n_grp):
        bg = bm[:, g * d_st:(g + 1) * d_st].astype(BF16)
        cg_f = cm[:, g * d_st:(g + 1) * d_st]
        cb = lax.dot_general(cg_f.astype(BF16), bg, _NT, preferred_element_type=F32)
        sg = state[g]
        sg_b = sg.astype(BF16)
        gl = slice(g * hp, (g + 1) * hp)
        for h0 in range(0, hpg, pair):
            outs = []
            for hh in range(g * hpg + h0, g * hpg + h0 + pair):
                ks = slice(hh * t, (hh + 1) * t)
                ns = slice(hh * d_st, (hh + 1) * d_st)
                ps = slice(hh * pdim, (hh + 1) * pdim)
                ps_g = slice(hh * pdim - g * hp, (hh + 1) * pdim - g * hp)
                decay = jnp.exp(jnp.where(low, a_col[:, ks] - a_row[:, ks], -jnp.inf))
                intra = (cb * decay).astype(BF16)
                carried = (cg_f * ea_col[:, ns]).astype(BF16)
                if t % LANES == 0:
                    outs.append(_dot(jnp.concatenate([intra, carried], axis=1),
                                     jnp.concatenate([xdt_b[:, ps], sg_b[:, ps_g]], axis=0)))
                else:
                    outs.append(_dot(intra, xdt_b[:, ps]) + _dot(carried, sg_b[:, ps_g]))
            lo = (g * hpg + h0) * pdim
            yacc[:, lo:lo + pair * pdim] = jnp.concatenate(outs, axis=1)
        upd = lax.dot_general(bg, xdst_b[:, gl], _TN, preferred_element_type=F32)
        state[g] = sg * chunk_decay[:, gl] + upd

    z = zx_ref[0, :, :d_inner]
    y = (yacc[...] + xs * dsk_ref[...]) * (z * _sigmoid(z))
    nw = nw_ref[...]
    for g in range(n_grp):
        gl = slice(g * hp, (g + 1) * hp)
        y_ref[0, :, gl] = _rms(y[:, gl], nw[:, gl]).astype(y_ref.dtype)

    @pl.when(c == pl.num_programs(1) - 1)
    def _fin():
        hfin_ref[0] = state[...]


def _ssd(zx, conv_buf, h0, conv_w, conv_b, dt_bias, a_log, d_skip, norm_w, *, t, valid, conv_done=False):
    b, l, in_dim = zx.shape
    n_heads = a_log.shape[0]
    d_inner = n_heads * SSM_HEAD_DIM
    conv_dim = conv_w.shape[1]
    n_grp, d_st = SSM_N_GROUPS, SSM_D_STATE
    hp = d_inner // n_grp
    assert l % t == 0 and in_dim == d_inner + conv_dim + n_heads
    cbuf = jnp.pad(conv_buf, ((0, 0), (SUBLANES - (SSM_CONV - 1), 0), (0, 0)))
    h0t = h0.reshape(b, n_grp, n_heads // n_grp, SSM_HEAD_DIM, d_st)
    h0t = h0t.transpose(0, 1, 4, 2, 3).reshape(b, n_grp, d_st, hp)
    body = functools.partial(_ssd_kernel, t=t, valid=valid, d_inner=d_inner, n_heads=n_heads,
                             conv_done=conv_done)
    tri = jnp.tril(jnp.ones((t, t), BF16))
    sel_hp = _spread_matrix(n_heads, SSM_HEAD_DIM)
    sel_hk = _spread_matrix(n_heads, t)
    sel_hn = _spread_matrix(n_heads, d_st)
    diag = jnp.tile(jnp.eye(t, dtype=F32), (1, n_heads))
    consts = (tri, sel_hp, sel_hk, sel_hn, diag)
    y, hfin = pl.pallas_call(
        body,
        grid=(b, l // t),
        in_specs=[
            pl.BlockSpec((1, t, in_dim), lambda i, c: (i, c, 0)),
            pl.BlockSpec((1, SUBLANES, conv_dim), lambda i, c: (i, 0, 0)),
            pl.BlockSpec((1, n_grp, d_st, hp), lambda i, c: (i, 0, 0, 0)),
            _resident((SSM_CONV, conv_dim)), _resident((1, conv_dim)),
            _resident((1, n_heads)), _resident((1, n_heads)),
            _resident((1, d_inner)), _resident((1, d_inner)),
        ] + [_resident(cst.shape) for cst in consts],
        out_specs=[
            pl.BlockSpec((1, t, d_inner), lambda i, c: (i, c, 0)),
            pl.BlockSpec((1, n_grp, d_st, hp), lambda i, c: (i, 0, 0, 0)),
        ],
        out_shape=[
            jax.ShapeDtypeStruct((b, l, d_inner), BF16),
            jax.ShapeDtypeStruct((b, n_grp, d_st, hp), F32),
        ],
        scratch_shapes=[
            pltpu.VMEM((SUBLANES, conv_dim), F32),
            pltpu.VMEM((n_grp, d_st, hp), F32),
            pltpu.VMEM((t, d_inner), F32),
        ],
        compiler_params=_cparams("parallel", "arbitrary"),
        name="ssd_scan",
    )(zx, cbuf, h0t, conv_w, conv_b.reshape(1, -1), dt_bias.reshape(1, -1), a_log.reshape(1, -1),
      jnp.repeat(d_skip, SSM_HEAD_DIM).reshape(1, -1), norm_w.reshape(1, -1), *consts)
    hfin = hfin.reshape(b, n_grp, d_st, n_heads // n_grp, SSM_HEAD_DIM)
    hfin = hfin.transpose(0, 1, 3, 4, 2).reshape(b, n_heads, SSM_HEAD_DIM, d_st)
    return y, hfin


def _gla_kernel(p_ref, s0_ref, wg_ref, gb_ref, nw_ref, tri_ref, o_ref, sfin_ref, state,
                *, tb, ch, valid, dk, dv):
    c = pl.program_id(1)
    n_h = GLA_N_HEADS
    hk, hv = dk // n_h, dv // n_h

    @pl.when(c == 0)
    def _init():
        state[...] = s0_ref[0]

    q = p_ref[0, :, 0:dk] * (hk ** -0.5)
    k = p_ref[0, :, dk:2 * dk]
    v = p_ref[0, :, 2 * dk:2 * dk + dv]
    r = p_ref[0, :, 2 * dk + dv:2 * dk + 2 * dv]
    g_low = p_ref[0, :, 2 * dk + 2 * dv:]
    x = _dot(g_low.astype(BF16), wg_ref[...]) + gb_ref[...]
    log_a = -_softplus(-x) / GLA_GATE_NORM
    if valid < tb:
        keep = lax.broadcasted_iota(jnp.int32, (tb, dk), 0) < valid
        log_a = jnp.where(keep, log_a, 0.0)
        k = jnp.where(keep, k, 0.0)

    bcum = _sel_left(tri_ref[...], log_a)
    nch = tb // ch

    def chunks(a):
        return a.reshape(nch, ch, a.shape[-1])

    bc3 = chunks(bcum)
    b_last = bc3[:, ch - 1:ch, :]
    q_t = chunks(q * jnp.exp(bcum)).astype(BF16)
    k_t = chunks(k * jnp.exp(-bcum)).astype(BF16)
    k_dec = (chunks(k) * jnp.exp(b_last - bc3)).astype(BF16)
    e_last = jnp.exp(b_last)
    v_b = chunks(v).astype(BF16)
    causal = (lax.broadcasted_iota(jnp.int32, (nch, ch, ch), 1)
              >= lax.broadcasted_iota(jnp.int32, (nch, ch, ch), 2))

    gate = r * _sigmoid(r)
    nw = nw_ref[...]
    for h in range(n_h):
        ks = slice(h * hk, (h + 1) * hk)
        vs = slice(h * hv, (h + 1) * hv)
        qh, kh, vh = q_t[:, :, ks], k_t[:, :, ks], v_b[:, :, vs]
        upd = jnp.einsum("jkv,jkd->jvd", vh, k_dec[:, :, ks], preferred_element_type=F32)
        s_run = state[h]
        entering = []
        for j in range(nch):
            entering.append(s_run.astype(BF16))
            s_run = s_run * e_last[j, :, ks] + upd[j]
        state[h] = s_run
        att = jnp.einsum("jqd,jkd->jqk", qh, kh, preferred_element_type=F32)
        att = jnp.where(causal, att, 0.0).astype(BF16)
        o = (jnp.einsum("jqk,jkv->jqv", att, vh, preferred_element_type=F32)
             + jnp.einsum("jqd,jvd->jqv", qh, jnp.stack(entering), preferred_element_type=F32))
        o_ref[0, :, vs] = (_rms(o.reshape(tb, hv), nw) * gate[:, vs]).astype(o_ref.dtype)

    @pl.when(c == pl.num_programs(1) - 1)
    def _fin():
        sfin_ref[0] = state[...]


def _gla(proj, s0, w_gate, gate_bias, norm_w, *, tb, ch, valid):
    b, l, in_dim = proj.shape
    n_h = GLA_N_HEADS
    rank, dk = w_gate.shape
    hv = norm_w.shape[0]
    dv = hv * n_h
    hk = dk // n_h
    assert l % tb == 0 and tb % ch == 0 and in_dim == 2 * dk + 2 * dv + rank
    body = functools.partial(_gla_kernel, tb=tb, ch=ch, valid=valid, dk=dk, dv=dv)
    blocktri = jnp.kron(jnp.eye(tb // ch, dtype=F32), jnp.tril(jnp.ones((ch, ch), F32))).astype(BF16)
    o, sfin = pl.pallas_call(
        body,
        grid=(b, l // tb),
        in_specs=[
            pl.BlockSpec((1, tb, in_dim), lambda i, c: (i, c, 0)),
            pl.BlockSpec((1, n_h, hv, hk), lambda i, c: (i, 0, 0, 0)),
            _resident((rank, dk)), _resident((1, dk)), _resident((1, hv)), _resident((tb, tb)),
        ],
        out_specs=[
            pl.BlockSpec((1, tb, dv), lambda i, c: (i, c, 0)),
            pl.BlockSpec((1, n_h, hv, hk), lambda i, c: (i, 0, 0, 0)),
        ],
        out_shape=[
            jax.ShapeDtypeStruct((b, l, dv), BF16),
            jax.ShapeDtypeStruct((b, n_h, hv, hk), F32),
        ],
        scratch_shapes=[pltpu.VMEM((n_h, hv, hk), F32)],
        compiler_params=_cparams("parallel", "arbitrary"),
        name="gla_scan",
    )(proj, jnp.swapaxes(s0, -1, -2), w_gate.astype(BF16), gate_bias.reshape(1, -1),
      norm_w.reshape(1, -1), blocktri)
    return o, jnp.swapaxes(sfin, -1, -2)


def _attn_prompt_kernel(q_ref, kp_ref, kc_ref, vp_ref, vc_ref, o_ref, lse_ref, *, dilation, nb, span):
    not_first = pl.program_id(2) > 0
    blk = ATT_BLOCK
    dh = ATT_HEAD_DIM
    units = [(r, jb) for r in range(dilation) for jb in range(nb)]
    n_u = len(units)
    iu = lax.broadcasted_iota(jnp.int32, (n_u, blk, blk), 0)
    iq = lax.broadcasted_iota(jnp.int32, (n_u, blk, blk), 1)
    ik = lax.broadcasted_iota(jnp.int32, (n_u, blk, blk), 2)
    ok_prev = (ik >= iq + (blk - span)) & (((iu & (nb - 1)) != 0) | not_first)
    ok_cur = ik <= iq
    scale = dh ** -0.5

    def rows(r, jb):
        start = r + dilation * jb * blk
        return pl.ds(start, blk) if dilation == 1 else pl.ds(start, blk, stride=dilation)

    def gather(cur_ref, prev_ref):
        cur = jnp.stack([cur_ref[0, rows(r, jb), :] for r, jb in units])
        prev = jnp.stack([prev_ref[0, rows(r, nb - 1), :] if jb == 0 else cur_ref[0, rows(r, jb - 1), :]
                          for r, jb in units])
        return cur.astype(BF16), prev.astype(BF16)

    q = jnp.stack([q_ref[0, rows(r, jb), :] for r, jb in units]).astype(BF16)
    k_cur, k_prev = gather(kc_ref, kp_ref)
    v_cur, v_prev = gather(vc_ref, vp_ref)
    outs, lses = [], []
    for hh in range(LANES // dh):
        hs = slice(hh * dh, (hh + 1) * dh)
        qh = q[:, :, hs]
        s_p = jnp.einsum("uqd,ukd->uqk", qh, k_prev[:, :, hs], preferred_element_type=F32) * scale
        s_c = jnp.einsum("uqd,ukd->uqk", qh, k_cur[:, :, hs], preferred_element_type=F32) * scale
        s_p = jnp.where(ok_prev, s_p, -jnp.inf)
        s_c = jnp.where(ok_cur, s_c, -jnp.inf)
        mx = jnp.max(jnp.maximum(s_p, s_c), axis=-1, keepdims=True)
        p_p = jnp.exp(s_p - mx)
        p_c = jnp.exp(s_c - mx)
        den = jnp.sum(p_p + p_c, axis=-1, keepdims=True)
        outs.append(jnp.einsum("uqk,ukd->uqd", (p_p / den).astype(BF16), v_prev[:, :, hs],
                               preferred_element_type=F32)
                    + jnp.einsum("uqk,ukd->uqd", (p_c / den).astype(BF16), v_cur[:, :, hs],
                                 preferred_element_type=F32))
        lses.append(jnp.broadcast_to(mx + jnp.log(den), (n_u, blk, dh)))
    o = jnp.concatenate(outs, axis=-1)
    lse = jnp.concatenate(lses, axis=-1)
    for u, (r, jb) in enumerate(units):
        o_ref[0, rows(r, jb), :] = o[u]
        lse_ref[0, rows(r, jb), :] = lse[u]


def _attn_prompt(qkv, gi, *, b, l, window, dilation):
    n_slab = qkv.shape[0] // 3
    spg = ATT_HEADS_PER_GROUP * ATT_HEAD_DIM // LANES
    span = window // dilation
    blk = ATT_BLOCK
    nb = max(1, min(l, ATT_SUPER_ROWS) // (dilation * blk))
    r_rows = dilation * nb * blk
    n_sup = l // r_rows
    assert l % r_rows == 0 and span <= blk and qkv.shape[1] == b * l and nb & (nb - 1) == 0

    def spec(part, prev):
        def index(bi, sp, i):
            i = jnp.maximum(i - 1, 0) if prev else i
            return (part * n_slab + gi * spg + sp, bi * n_sup + i, 0)
        return pl.BlockSpec((1, r_rows, LANES), index)

    out_spec = pl.BlockSpec((1, r_rows, LANES), lambda bi, sp, i: (sp, bi * n_sup + i, 0))
    return pl.pallas_call(
        functools.partial(_attn_prompt_kernel, dilation=dilation, nb=nb, span=span),
        grid=(b, spg, n_sup),
        in_specs=[spec(0, False), spec(1, True), spec(1, False), spec(2, True), spec(2, False)],
        out_specs=[out_spec, out_spec],
        out_shape=[jax.ShapeDtypeStruct((spg, b * l, LANES), F32)] * 2,
        compiler_params=_cparams("parallel", "parallel", "parallel"),
        name="attn_prompt",
    )(qkv, qkv, qkv, qkv, qkv)


def _attn_sample_kernel(q_ref, kn_ref, vn_ref, cache_ref, o_ref, lse_ref, cout_ref,
                        *, window, dilation, n_new, seqs):
    dh = ATT_HEAD_DIM
    rows = seqs * n_new
    scale = dh ** -0.5
    row = lax.broadcasted_iota(jnp.int32, (rows, window), 0)
    col = lax.broadcasted_iota(jnp.int32, (rows, window), 1)
    sq = row & (n_new - 1)
    ok_cache = (col >= sq) & (((col - sq) & (dilation - 1)) == 0)
    row_n = lax.broadcasted_iota(jnp.int32, (rows, rows), 0)
    col_n = lax.broadcasted_iota(jnp.int32, (rows, rows), 1)
    back = (row_n & (n_new - 1)) - (col_n & (n_new - 1))
    same_res = (back >= 0) & ((back & (dilation - 1)) == 0)
    lane_t = lax.broadcasted_iota(jnp.int32, (2 * rows, LANES), 1)
    tok_t = lax.broadcasted_iota(jnp.int32, (2 * rows, LANES), 0)
    lane_o = lax.broadcasted_iota(jnp.int32, (dh, LANES), 1)
    zpad = jnp.zeros((rows, dh), F32)
    for bb in range(seqs):
        r0 = bb * n_new
        ok_new = same_res & (col_n >= r0) & (col_n < r0 + n_new)
        place = jnp.where(lane_t - (LANES - n_new) == tok_t - r0, 1.0, 0.0).astype(BF16)
        for h in range(ATT_HEADS_PER_GROUP):
            sl = h * dh // LANES
            hs = slice(h * dh % LANES, h * dh % LANES + dh)
            q8 = q_ref[sl][:, hs].astype(BF16)
            k_new, v_new = kn_ref[sl][:, hs], vn_ref[sl][:, hs]
            k_t, v_t = cache_ref[bb, 0, h], cache_ref[bb, 1, h]
            s_c = jnp.where(ok_cache, _dot(q8, k_t.astype(BF16)) * scale, -jnp.inf)
            s_n = lax.dot_general(q8, k_new.astype(BF16), _NT, preferred_element_type=F32) * scale
            s_n = jnp.where(ok_new, s_n, -jnp.inf)
            mx = jnp.maximum(jnp.max(s_c, axis=-1, keepdims=True), jnp.max(s_n, axis=-1, keepdims=True))
            p_c = jnp.exp(s_c - mx)
            p_n = jnp.exp(s_n - mx)
            den = jnp.sum(p_c, axis=-1, keepdims=True) + jnp.sum(p_n, axis=-1, keepdims=True)
            o = (lax.dot_general((p_c / den).astype(BF16), v_t.astype(BF16), _NT, preferred_element_type=F32)
                 + _dot((p_n / den).astype(BF16), v_new.astype(BF16)))
            o_ref[sl, r0:r0 + n_new, hs] = o[r0:r0 + n_new]
            lse_ref[sl, r0:r0 + n_new, hs] = jnp.broadcast_to(mx + jnp.log(den), (rows, dh))[r0:r0 + n_new]
            for kv, old, new in ((0, k_t, k_new), (1, v_t, v_new)):
                hi, mid, lo = _split3(jnp.concatenate([new, zpad], axis=0))
                tail = sum(lax.dot_general(part, place, _TN, preferred_element_type=F32)
                           for part in (hi, mid, lo))
                shifted = pltpu.roll(old, window - n_new, 1)
                last = jnp.where(lane_o >= LANES - n_new, tail, shifted[:, window - LANES:])
                if window > LANES:
                    cout_ref[bb, kv, h, :, :window - LANES] = shifted[:, :window - LANES]
                cout_ref[bb, kv, h, :, window - LANES:] = last


def _attn_sample(qkv, cache_t, gi, *, window, dilation, n_new):
    n_slab = qkv.shape[0] // 3
    spg = ATT_HEADS_PER_GROUP * ATT_HEAD_DIM // LANES
    b = cache_t.shape[0]
    seqs = SUBLANES // n_new
    rows = seqs * n_new
    assert rows == SUBLANES and b % seqs == 0 and qkv.shape[1] == b * n_new
    assert cache_t.shape[-1] == window and window % dilation == 0 and window % LANES == 0
    assert dilation & (dilation - 1) == 0 and n_new & (n_new - 1) == 0

    def slabs(part):
        return pl.BlockSpec((spg, rows, LANES), lambda i: (part * n_slab // spg + gi, i, 0))

    cache_spec = pl.BlockSpec((seqs,) + cache_t.shape[1:], lambda i: (i, 0, 0, 0, 0))
    out_spec = pl.BlockSpec((spg, rows, LANES), lambda i: (0, i, 0))
    return pl.pallas_call(
        functools.partial(_attn_sample_kernel, window=window, dilation=dilation, n_new=n_new, seqs=seqs),
        grid=(b // seqs,),
        in_specs=[slabs(0), slabs(1), slabs(2), cache_spec],
        out_specs=[out_spec, out_spec, cache_spec],
        out_shape=[jax.ShapeDtypeStruct((spg, b * n_new, LANES), F32)] * 2
        + [jax.ShapeDtypeStruct(cache_t.shape, F32)],
        compiler_params=_cparams("parallel"),
        name="attn_sample",
    )(qkv, qkv, qkv, cache_t)


def _attn_out_kernel(o0_ref, o1_ref, o2_ref, l0_ref, l1_ref, l2_ref, w_ref, x_ref, out_ref):
    o_refs, l_refs = (o0_ref, o1_ref, o2_ref), (l0_ref, l1_ref, l2_ref)
    spg = o0_ref.shape[0]
    pieces = [[None] * spg for _ in o_refs]
    for s in range(spg):
        ls = [l_ref[s] for l_ref in l_refs]
        mx = jnp.maximum(jnp.maximum(ls[0], ls[1]), ls[2])
        es = [jnp.exp(l - mx) for l in ls]
        den = es[0] + es[1] + es[2]
        for g, (o_ref, e) in enumerate(zip(o_refs, es)):
            pieces[g][s] = o_ref[s] * (e / den)
    mixed = jnp.concatenate([p for grp in pieces for p in grp], axis=-1).astype(BF16)
    out_ref[...] = x_ref[...] + _dot(mixed, w_ref[...])


def _attn_out(os_, lses, wl, x, *, tm):
    m, d = x.shape
    w = wl[0]
    spg = os_[0].shape[0]
    row = pl.BlockSpec((spg, tm, LANES), lambda i: (0, i, 0))
    return pl.pallas_call(
        _attn_out_kernel,
        grid=(m // tm,),
        in_specs=[row] * 6 + [_layer(wl), pl.BlockSpec((tm, d), lambda i: (i, 0))],
        out_specs=pl.BlockSpec((tm, d), lambda i: (i, 0)),
        out_shape=jax.ShapeDtypeStruct((m, d), F32),
        compiler_params=_cparams("parallel"),
        name="attn_mix_proj",
    )(*os_, *lses, w, x)


def _rope_tables(pos):
    half = ATT_ROT_DIM // 2
    lane = lax.broadcasted_iota(jnp.int32, (pos.shape[0], LANES), 1) % ATT_HEAD_DIM
    x1, rot = lane < half, lane < ATT_ROT_DIM
    freq = jnp.where(x1, lane, lane - half)
    inv_freq = ROPE_THETA ** (-(2 * freq).astype(F32) / ATT_ROT_DIM)
    ang = pos.astype(F32)[:, None] * inv_freq
    cos, sin = jnp.cos(ang), jnp.sin(ang)
    c = jnp.where(rot, cos, 1.0)
    s1 = jnp.where(rot & jnp.logical_not(x1), sin, 0.0)
    s2 = jnp.where(x1, -sin, 0.0)
    return c, s1, s2


def _pad_rows(t, rows):
    return jnp.pad(t, ((0, 0), (0, rows - t.shape[1]), (0, 0)))


def kernel(x_prompt, x_sample, state_ssm, state_ssm_conv, state_gla, cache_kv_g0, cache_kv_g1, cache_kv_g2,
           norm_mix, norm_ffn, ffn_gate, ffn_up, ffn_down,
           ssm_w_in, ssm_conv_w, ssm_conv_b, ssm_dt_bias, ssm_a_log, ssm_d, ssm_norm, ssm_w_out,
           gla_w_in, gla_w_gate, gla_gate_bias, gla_norm, gla_w_out,
           att_w_qkv, att_w_out, norm_final):
    bp, lp, d = x_prompt.shape
    bs, ls, _ = x_sample.shape
    depth = norm_mix.shape[0]
    mp, ms = bp * lp, bs * ls
    tm_p = 512 if mp % 512 == 0 else mp
    tm_s = ms
    xp = x_prompt.reshape(mp, d)
    xs = x_sample.reshape(ms, d)
    att_caches = (cache_kv_g0, cache_kv_g1, cache_kv_g2)
    hg_w = ATT_HEADS_PER_GROUP * ATT_HEAD_DIM
    n_att = hg_w * len(ATT_GROUPS)
    rope_p = _rope_tables(jnp.arange(lp))
    rope_s = _rope_tables(jnp.tile(PAST_LEN + jnp.arange(ls), bs))
    ffn_w = tuple(w.astype(BF16) for w in (ffn_gate, ffn_up, ffn_down))
    ssm_w_in, ssm_w_out, gla_w_in, gla_w_out, att_w_qkv, att_w_out = (
        w.astype(BF16) for w in (ssm_w_in, ssm_w_out, gla_w_in, gla_w_out, att_w_qkv, att_w_out))

    ssm_p, ssm_s, conv_p, conv_s, gla_p, gla_s = [], [], [], [], [], []
    kv_p, kv_s = [[], [], []], [[], [], []]
    for i in range(depth):
        m, j = i % N_MIXERS, i // N_MIXERS
        if m == 0:
            w_in = (ssm_w_in, j)
            conv_dim = ssm_conv_w.shape[2]
            d_inner = ssm_w_out.shape[1]
            wts = (ssm_conv_w[j], ssm_conv_b[j], ssm_dt_bias[j], ssm_a_log[j], ssm_d[j], ssm_norm[j])
            conv0 = jnp.zeros((bp, SSM_CONV - 1, conv_dim), F32)
            zx_p, tails = _ssd_in_proj(xp, norm_mix[i], w_in, conv0, ssm_conv_w[j], ssm_conv_b[j],
                                       tm=tm_p, seq_len=lp, d_inner=d_inner)
            zx_p = zx_p.reshape(bp, lp, -1)
            zx_s = _norm_matmul(xs, norm_mix[i], w_in, tm=tm_s).reshape(bs, ls, -1)
            t_p = min(SSM_CHUNK, lp)
            y_p, h_p = _ssd(zx_p, conv0, jnp.zeros((bp,) + state_ssm.shape[2:], F32), *wts,
                            t=t_p, valid=t_p, conv_done=True)
            y_s, h_s = _ssd(_pad_rows(zx_s, SAMPLE_PAD), state_ssm_conv[j], state_ssm[j], *wts,
                            t=SAMPLE_PAD, valid=ls)
            y_s = y_s[:, :ls]
            xbc_s = zx_s[:, :, d_inner:d_inner + conv_dim]
            conv_p.append(tails.reshape(bp, lp // tm_p, SUBLANES, conv_dim)[:, -1, -(SSM_CONV - 1):])
            conv_s.append(jnp.concatenate([state_ssm_conv[j], xbc_s], axis=1)[:, -(SSM_CONV - 1):])
            ssm_p.append(h_p)
            ssm_s.append(h_s)
            w_out = (ssm_w_out, j)
            xp = _proj_res(y_p.reshape(mp, -1), w_out, xp, tm=tm_p)
            xs = _proj_res(y_s.reshape(ms, -1), w_out, xs, tm=tm_s)
        elif m == 1:
            w_in = (gla_w_in, j)
            wts = (gla_w_gate[j], gla_gate_bias[j], gla_norm[j])
            pr_p = _norm_matmul(xp, norm_mix[i], w_in, tm=tm_p).reshape(bp, lp, -1)
            pr_s = _norm_matmul(xs, norm_mix[i], w_in, tm=tm_s).reshape(bs, ls, -1)
            ch_p = min(GLA_CHUNK, lp)
            tb_p = 256 if lp % 256 == 0 else ch_p
            o_p, s_p = _gla(pr_p, jnp.zeros((bp,) + state_gla.shape[2:], F32), *wts,
                            tb=tb_p, ch=ch_p, valid=tb_p)
            o_s, s_s = _gla(_pad_rows(pr_s, SAMPLE_PAD), state_gla[j], *wts,
                            tb=SAMPLE_PAD, ch=SAMPLE_PAD, valid=ls)
            o_s = o_s[:, :ls]
            gla_p.append(s_p)
            gla_s.append(s_s)
            w_out = (gla_w_out, j)
            xp = _proj_res(o_p.reshape(mp, -1), w_out, xp, tm=tm_p)
            xs = _proj_res(o_s.reshape(ms, -1), w_out, xs, tm=tm_s)
        else:
            w_qkv = (att_w_qkv, j)
            qkv_p = _norm_matmul(xp, norm_mix[i], w_qkv, tm=tm_p, rope=(2 * n_att, rope_p))
            qkv_s = _norm_matmul(xs, norm_mix[i], w_qkv, tm=tm_s, rope=(2 * n_att, rope_s))
            n_slab = n_att // LANES
            spg = hg_w // LANES
            qkv_p4 = qkv_p.reshape(3 * n_slab, bp, lp, LANES)
            os_p, ls_p, os_s, ls_s = [], [], [], []
            for gi, (window, dilation) in enumerate(ATT_GROUPS):
                o, lse = _attn_prompt(qkv_p, gi, b=bp, l=lp, window=window, dilation=dilation)
                os_p.append(o)
                ls_p.append(lse)
                keep = min(window, lp)
                kv = jnp.stack([qkv_p4[part * n_slab + gi * spg:part * n_slab + (gi + 1) * spg, :, lp - keep:]
                                for part in (1, 2)])
                kv = kv.reshape(2, spg, bp, keep, LANES // ATT_HEAD_DIM, ATT_HEAD_DIM)
                kv_p[gi].append(kv.transpose(2, 3, 0, 1, 4, 5)
                                .reshape(bp, keep, 2, ATT_HEADS_PER_GROUP, ATT_HEAD_DIM))
                cache_t = att_caches[gi][j].transpose(0, 2, 3, 4, 1)
                o, lse, cache_new = _attn_sample(qkv_s, cache_t, gi, window=window, dilation=dilation, n_new=ls)
                os_s.append(o)
                ls_s.append(lse)
                kv_s[gi].append(cache_new.transpose(0, 4, 1, 2, 3))
            w_out = (att_w_out, j)
            xp = _attn_out(os_p, ls_p, w_out, xp, tm=tm_p)
            xs = _attn_out(os_s, ls_s, w_out, xs, tm=tm_s)
        final_g = norm_final if i == depth - 1 else None
        xp = _ffn(xp, norm_ffn[i], *ffn_w, i, tm=tm_p, final_g=final_g)
        xs = _ffn(xs, norm_ffn[i], *ffn_w, i, tm=tm_s, final_g=final_g)
    y_prompt = xp.reshape(bp, lp, d)
    y_sample = xs.reshape(bs, ls, d)
    return (y_prompt, y_sample,
            jnp.stack(ssm_p), jnp.stack(ssm_s), jnp.stack(conv_p), jnp.stack(conv_s),
            jnp.stack(gla_p), jnp.stack(gla_s),
            jnp.stack(kv_p[0]), jnp.stack(kv_s[0]), jnp.stack(kv_p[1]), jnp.stack(kv_s[1]),
            jnp.stack(kv_p[2]), jnp.stack(kv_s[2]))
```

```python
import functools

import jax
import jax.numpy as jnp
from jax import lax
from jax.experimental import pallas as pl
from jax.experimental.pallas import tpu as pltpu

F32 = jnp.float32
BF16 = jnp.bfloat16
LOG2_E = 1.4426950408889634

NORM_EPS = 1e-6
N_MIXERS = 3

SSM_HEAD_DIM = 64
SSM_N_GROUPS = 4
SSM_D_STATE = 128
SSM_CONV = 4
SSM_CHUNK = 128
SSM_IN_PROJ_CHUNKS = 4

GLA_N_HEADS = 4
GLA_GATE_NORM = 16.0
GLA_CHUNK = 32

ATT_GROUPS = ((128, 1), (512, 4), (2048, 16))
ATT_HEADS_PER_GROUP = 4
ATT_HEAD_DIM = 64
ATT_ROT_DIM = ATT_HEAD_DIM // 4
ROPE_THETA = 500000.0
ATT_BLOCK = 128
ATT_SUPER_ROWS = 2048
PAST_LEN = 8192

LANES = 128
SUBLANES = 8
SAMPLE_PAD = 16
VMEM_LIMIT = 56 * 1024 * 1024

_NT = (((1,), (1,)), ((), ()))
_TN = (((0,), (0,)), ((), ()))


def _cparams(*sem):
    return pltpu.CompilerParams(dimension_semantics=sem, vmem_limit_bytes=VMEM_LIMIT)


def _resident(shape):
    zeros = (0,) * len(shape)
    return pl.BlockSpec(shape, lambda *_: zeros, pipeline_mode=pl.Buffered(1))


def _layer(wl):
    w, layer = wl
    return pl.BlockSpec((None,) + w.shape[1:], lambda *_: (layer, 0, 0), pipeline_mode=pl.Buffered(1))


def _rms(x, g):
    inv = lax.rsqrt(jnp.mean(x * x, axis=-1, keepdims=True) + NORM_EPS)
    return x * inv * g


def _sigmoid(x):
    return 1.0 / (1.0 + jnp.exp(-x))


def _softplus(x):
    return jnp.maximum(x, 0.0) + jnp.log1p(jnp.exp(-jnp.abs(x)))


def _split3(v):
    hi = v.astype(BF16)
    r = v - hi.astype(F32)
    mid = r.astype(BF16)
    lo = (r - mid.astype(F32)).astype(BF16)
    return hi, mid, lo


def _dot(a, b):
    return jnp.dot(a, b, preferred_element_type=F32)


def _spread_matrix(n_blocks, width):
    k_dim = -(-3 * n_blocks // LANES) * LANES
    row = lax.broadcasted_iota(jnp.int32, (k_dim, n_blocks * width), 0)
    col = lax.broadcasted_iota(jnp.int32, (k_dim, n_blocks * width), 1)
    blk = row - jnp.where(row >= 2 * n_blocks, 2 * n_blocks, jnp.where(row >= n_blocks, n_blocks, 0))
    sel = (row < 3 * n_blocks) & (col >= blk * width) & (col < blk * width + width)
    return jnp.where(sel, 1.0, 0.0).astype(BF16)


def _spread(v, sel):
    rows, n_blocks = v.shape
    terms = [p.astype(F32) for p in _split3(v)]
    pad = jnp.zeros((rows, sel.shape[0] - 3 * n_blocks), F32)
    return _dot(jnp.concatenate(terms + [pad], axis=1).astype(BF16), sel)


def _sel_left(e, v):
    hi, mid, lo = _split3(v)
    return _dot(e, hi) + _dot(e, mid) + _dot(e, lo)


def _norm_matmul_kernel(x_ref, g_ref, w_ref, o_ref):
    h = _rms(x_ref[...], g_ref[...]).astype(BF16)
    o_ref[...] = _dot(h, w_ref[...])


def _norm_matmul_rope_kernel(x_ref, g_ref, w_ref, c_ref, s1_ref, s2_ref, o_ref, *, n_rot):
    h = _rms(x_ref[...], g_ref[...]).astype(BF16)
    res = _dot(h, w_ref[...])
    c, s1, s2 = c_ref[...], s1_ref[...], s2_ref[...]
    half = ATT_ROT_DIM // 2
    for j in range(o_ref.shape[0]):
        x = res[:, j * LANES:(j + 1) * LANES]
        if j < n_rot // LANES:
            x = x * c + pltpu.roll(x, half, 1) * s1 + pltpu.roll(x, LANES - half, 1) * s2
        o_ref[j] = x


def _norm_matmul(x, g, wl, *, tm, rope=None):
    m, d = x.shape
    n = wl[0].shape[2]
    assert m % tm == 0
    in_specs = [pl.BlockSpec((tm, d), lambda i: (i, 0)), _resident((1, d)), _layer(wl)]
    args = [x, g.reshape(1, d), wl[0]]
    if rope is None:
        body = _norm_matmul_kernel
        out_spec = pl.BlockSpec((tm, n), lambda i: (i, 0))
        out_shape = jax.ShapeDtypeStruct((m, n), F32)
    else:
        n_rot, tables = rope
        assert n % LANES == 0 and n_rot % LANES == 0
        body = functools.partial(_norm_matmul_rope_kernel, n_rot=n_rot)
        period = tables[0].shape[0] // tm
        assert tables[0].shape[0] % tm == 0 and (m // tm) % period == 0
        in_specs += [pl.BlockSpec((tm, LANES), lambda i: (i % period, 0))] * 3
        args += list(tables)
        out_spec = pl.BlockSpec((n // LANES, tm, LANES), lambda i: (0, i, 0))
        out_shape = jax.ShapeDtypeStruct((n // LANES, m, LANES), F32)
    return pl.pallas_call(
        body,
        grid=(m // tm,),
        in_specs=in_specs,
        out_specs=out_spec,
        out_shape=out_shape,
        compiler_params=_cparams("parallel"),
        name="norm_proj" if rope is None else "norm_proj_rope",
    )(*args)


def _conv_silu(raw, before, cw, cb):
    t, c = raw.shape
    groups = [before] + [raw[r:r + SUBLANES] for r in range(0, t, SUBLANES)]
    row8 = lax.broadcasted_iota(jnp.int32, (SUBLANES, c), 0)
    acc = cb
    for k in range(SSM_CONV - 1, 0, -1):
        rolled = [pltpu.roll(grp, k, 0) for grp in groups]
        shifted = jnp.concatenate([jnp.where(row8 < k, prev, here)
                                   for prev, here in zip(rolled[:-1], rolled[1:])], axis=0)
        acc = acc + shifted * cw[SSM_CONV - 1 - k:SSM_CONV - k, :]
    acc = acc + raw * cw[SSM_CONV - 1:SSM_CONV, :]
    return acc * _sigmoid(acc), groups[-1]


def _ssd_in_proj_kernel(x_ref, g_ref, w_ref, cbuf_ref, cw_ref, cb_ref, o_ref, tail_ref, tail,
                        *, d_inner, conv_dim, tiles_per_seq, col_chunk):
    @pl.when(pl.program_id(0) % tiles_per_seq == 0)
    def _start_of_sequence():
        tail[...] = cbuf_ref[0]

    h = _rms(x_ref[...], g_ref[...]).astype(BF16)
    n_chunks = conv_dim // col_chunk
    z_chunk = d_inner // n_chunks
    for j in range(n_chunks):
        c0 = j * col_chunk
        cs = slice(c0, c0 + col_chunk)
        raw = _dot(h, w_ref[:, d_inner + c0:d_inner + c0 + col_chunk])
        zs = slice(j * z_chunk, (j + 1) * z_chunk)
        o_ref[:, zs] = _dot(h, w_ref[:, zs])
        act, last = _conv_silu(raw, tail[:, cs], cw_ref[:, cs], cb_ref[:, cs])
        o_ref[:, d_inner + c0:d_inner + c0 + col_chunk] = act
        tail[:, cs] = last
        tail_ref[0, :, cs] = last
    o_ref[:, d_inner + conv_dim:] = _dot(h, w_ref[:, d_inner + conv_dim:])


def _ssd_in_proj(x, g, wl, conv_buf, conv_w, conv_b, *, tm, seq_len, d_inner):
    m, d = x.shape
    n = wl[0].shape[2]
    conv_dim = conv_w.shape[1]
    assert m % tm == 0 and seq_len % tm == 0
    cbuf = jnp.pad(conv_buf, ((0, 0), (SUBLANES - (SSM_CONV - 1), 0), (0, 0)))
    tiles_per_seq = seq_len // tm
    body = functools.partial(_ssd_in_proj_kernel, d_inner=d_inner, conv_dim=conv_dim,
                             tiles_per_seq=tiles_per_seq, col_chunk=conv_dim // SSM_IN_PROJ_CHUNKS)
    assert conv_dim % (SSM_IN_PROJ_CHUNKS * LANES) == 0 and d_inner % (SSM_IN_PROJ_CHUNKS * LANES) == 0
    return pl.pallas_call(
        body,
        grid=(m // tm,),
        in_specs=[pl.BlockSpec((tm, d), lambda i: (i, 0)), _resident((1, d)), _layer(wl),
                  pl.BlockSpec((1, SUBLANES, conv_dim), lambda i: (i // tiles_per_seq, 0, 0)),
                  _resident((SSM_CONV, conv_dim)), _resident((1, conv_dim))],
        out_specs=[pl.BlockSpec((tm, n), lambda i: (i, 0)),
                   pl.BlockSpec((1, SUBLANES, conv_dim), lambda i: (i, 0, 0))],
        out_shape=[jax.ShapeDtypeStruct((m, n), F32),
                   jax.ShapeDtypeStruct((m // tm, SUBLANES, conv_dim), F32)],
        scratch_shapes=[pltpu.VMEM((SUBLANES, conv_dim), F32)],
        compiler_params=_cparams("arbitrary"),
        name="ssd_in_proj",
    )(x, g.reshape(1, d), wl[0], cbuf, conv_w, conv_b.reshape(1, -1))


def _proj_res_kernel(y_ref, w_ref, x_ref, o_ref):
    o_ref[...] = x_ref[...] + _dot(y_ref[...], w_ref[...])


def _proj_res(y, wl, x, *, tm):
    m, k = y.shape
    w = wl[0]
    d = w.shape[2]
    assert m % tm == 0
    return pl.pallas_call(
        _proj_res_kernel,
        grid=(m // tm,),
        in_specs=[pl.BlockSpec((tm, k), lambda i: (i, 0)), _layer(wl),
                  pl.BlockSpec((tm, d), lambda i: (i, 0))],
        out_specs=pl.BlockSpec((tm, d), lambda i: (i, 0)),
        out_shape=jax.ShapeDtypeStruct((m, d), F32),
        compiler_params=_cparams("parallel"),
        name="proj_res",
    )(y, w, x)


def _ffn_kernel(x_ref, g_ref, wg_ref, wu_ref, wd_ref, gf_ref, o_ref, *, final_norm):
    x = x_ref[...]
    h = _rms(x, g_ref[...]).astype(BF16)
    gate = _dot(h, wg_ref[...])
    up = _dot(h, wu_ref[...])
    act = (gate * _sigmoid(gate) * up).astype(BF16)
    y = x + _dot(act, wd_ref[...])
    o_ref[...] = _rms(y, gf_ref[...]) if final_norm else y


def _ffn(x, g, wg, wu, wd, layer, *, tm, final_g=None):
    m, d = x.shape
    f = wg.shape[2]
    assert m % tm == 0
    gf = g if final_g is None else final_g
    return pl.pallas_call(
        functools.partial(_ffn_kernel, final_norm=final_g is not None),
        grid=(m // tm,),
        in_specs=[pl.BlockSpec((tm, d), lambda i: (i, 0)), _resident((1, d)),
                  _layer((wg, layer)), _layer((wu, layer)), _layer((wd, layer)), _resident((1, d))],
        out_specs=pl.BlockSpec((tm, d), lambda i: (i, 0)),
        out_shape=jax.ShapeDtypeStruct((m, d), F32),
        compiler_params=_cparams("parallel"),
        name="swiglu",
    )(x, g.reshape(1, d), wg, wu, wd, gf.reshape(1, d))


def _ssd_kernel(zx_ref, cbuf_ref, h0_ref, cw_ref, cb_ref, dtb_ref, alog_ref, dsk_ref, nw_ref,
                tri_ref, triu_ref, sel_hp_ref, sel_hk_ref, *rest,
                t, valid, d_inner, n_heads, conv_done, native):
    y_ref, hfin_ref, tail, state, yacc = rest[-5:]
    c = pl.program_id(1)
    n_grp = SSM_N_GROUPS
    d_st = SSM_D_STATE
    hp = d_inner // n_grp
    hpg = n_heads // n_grp
    pdim = SSM_HEAD_DIM
    bc_w = n_grp * d_st
    conv_dim = d_inner + 2 * bc_w

    @pl.when(c == 0)
    def _init():
        state[...] = h0_ref[0]
        tail[...] = cbuf_ref[0]

    xbc = zx_ref[0, :, d_inner:d_inner + conv_dim]
    if not conv_done:
        xbc, tail[...] = _conv_silu(xbc, tail[...], cw_ref[...], cb_ref[...])
    xs = xbc[:, :d_inner]
    bm = xbc[:, d_inner:d_inner + bc_w]
    cm = xbc[:, d_inner + bc_w:]

    dt = _softplus(zx_ref[0, :, d_inner + conv_dim:] + dtb_ref[...])
    if valid < t:
        dt = jnp.where(lax.broadcasted_iota(jnp.int32, dt.shape, 0) < valid, dt, 0.0)
    a = dt * (-jnp.exp(alog_ref[...]))

    iq = lax.broadcasted_iota(jnp.int32, (t, t), 0)
    ik = lax.broadcasted_iota(jnp.int32, (t, t), 1)
    low = iq >= ik
    a_cs = _sel_left(tri_ref[...], a)
    a_cs_t = sum(lax.dot_general(part, triu_ref[...], _TN, preferred_element_type=F32)
                 for part in _split3(a))

    a_last = a_cs[t - 1:t, :]
    pad_rows = 2 * SUBLANES
    stack = jnp.concatenate([dt, dt * jnp.exp(a_last - a_cs), jnp.exp(a_cs),
                             jnp.broadcast_to(jnp.exp(a_last), (pad_rows, n_heads))], axis=0)
    ex = _spread(stack, sel_hp_ref[...])
    xdt_b = (xs * ex[0:t]).astype(BF16)
    xdst_b = (xs * ex[t:2 * t]).astype(BF16)
    from_start = ex[2 * t:3 * t]
    chunk_decay = ex[3 * t:3 * t + pad_rows]
    if native:
        only0 = jnp.where(lax.broadcasted_iota(jnp.int32, chunk_decay.shape, 0) == 0, chunk_decay, 0.0)
        ones = jnp.ones((pad_rows, d_st), BF16)
        decay_rows = sum(lax.dot_general(part, ones, _TN, preferred_element_type=F32)
                         for part in _split3(only0))

    a_col = _spread(a_cs * LOG2_E, sel_hk_ref[...])
    a_row = a_cs_t * LOG2_E

    pair = LANES // pdim
    for g in range(n_grp):
        bg = bm[:, g * d_st:(g + 1) * d_st].astype(BF16)
        cg = cm[:, g * d_st:(g + 1) * d_st].astype(BF16)
        cb = lax.dot_general(cg, bg, _NT, preferred_element_type=F32)
        sg = state[g]
        sg_b = sg.astype(BF16)
        gl = slice(g * hp, (g + 1) * hp)
        carried = lax.dot_general(cg, sg_b, _NT, preferred_element_type=F32) if native else _dot(cg, sg_b)
        yacc[:, gl] = carried * from_start[:, gl]
        for h0 in range(0, hpg, pair):
            outs = []
            for hh in range(g * hpg + h0, g * hpg + h0 + pair):
                ks = slice(hh * t, (hh + 1) * t)
                ps = slice(hh * pdim, (hh + 1) * pdim)
                decay = jnp.exp2(jnp.where(low, a_col[:, ks] - a_row[hh:hh + 1, :], -jnp.inf))
                outs.append(_dot((cb * decay).astype(BF16), xdt_b[:, ps]))
            lo = (g * hpg + h0) * pdim
            yacc[:, lo:lo + pair * pdim] += jnp.concatenate(outs, axis=1)
        if native:
            upd = lax.dot_general(xdst_b[:, gl], bg, _TN, preferred_element_type=F32)
            state[g] = sg * decay_rows[gl, :] + upd
        else:
            upd = lax.dot_general(bg, xdst_b[:, gl], _TN, preferred_element_type=F32)
            state[g] = sg * chunk_decay[0:1, gl] + upd

    z = zx_ref[0, :, :d_inner]
    y = (yacc[...] + xs * dsk_ref[...]) * (z * _sigmoid(z))
    nw = nw_ref[...]
    for g in range(n_grp):
        gl = slice(g * hp, (g + 1) * hp)
        y_ref[0, :, gl] = _rms(y[:, gl], nw[:, gl]).astype(y_ref.dtype)

    @pl.when(c == pl.num_programs(1) - 1)
    def _fin():
        hfin_ref[0] = state[...]


def _ssd(zx, conv_buf, h0, conv_w, conv_b, dt_bias, a_log, d_skip, norm_w, *, t, valid, conv_done=False,
         stacked=None):
    b, l, in_dim = zx.shape
    n_heads = a_log.shape[0]
    d_inner = n_heads * SSM_HEAD_DIM
    conv_dim = conv_w.shape[1]
    n_grp, d_st = SSM_N_GROUPS, SSM_D_STATE
    hp = d_inner // n_grp
    assert l % t == 0 and in_dim == d_inner + conv_dim + n_heads
    cbuf = jnp.pad(conv_buf, ((0, 0), (SUBLANES - (SSM_CONV - 1), 0), (0, 0)))
    native = stacked is not None
    extra_in, extra_specs, aliases = [], [], {}
    if native:
        layer, out_buf = stacked
        state_shape = (n_grp, hp, d_st)
        h0k = h0.reshape(h0.shape[0], b, *state_shape)
        state_spec = pl.BlockSpec((None, 1) + state_shape, lambda i, c: (layer, i, 0, 0, 0))
        state_out = jax.ShapeDtypeStruct(h0k.shape, F32)
        if out_buf is not None:
            extra_in, extra_specs = [out_buf.reshape(h0k.shape)], [pl.BlockSpec(memory_space=pl.ANY)]
    else:
        state_shape = (n_grp, d_st, hp)
        h0k = h0.reshape(b, n_grp, n_heads // n_grp, SSM_HEAD_DIM, d_st)
        h0k = h0k.transpose(0, 1, 4, 2, 3).reshape(b, *state_shape)
        state_spec = pl.BlockSpec((1,) + state_shape, lambda i, c: (i, 0, 0, 0))
        state_out = jax.ShapeDtypeStruct(h0k.shape, F32)
    body = functools.partial(_ssd_kernel, t=t, valid=valid, d_inner=d_inner, n_heads=n_heads,
                             conv_done=conv_done, native=native)
    tri = jnp.tril(jnp.ones((t, t), BF16))
    consts = (tri, tri.T, _spread_matrix(n_heads, SSM_HEAD_DIM), _spread_matrix(n_heads, t))
    operands = [zx, cbuf, h0k, conv_w, conv_b.reshape(1, -1), dt_bias.reshape(1, -1), a_log.reshape(1, -1),
                jnp.repeat(d_skip, SSM_HEAD_DIM).reshape(1, -1), norm_w.reshape(1, -1), *consts]
    if extra_in:
        aliases = {len(operands): 1}
    y, hfin = pl.pallas_call(
        body,
        grid=(b, l // t),
        in_specs=[
            pl.BlockSpec((1, t, in_dim), lambda i, c: (i, c, 0)),
            pl.BlockSpec((1, SUBLANES, conv_dim), lambda i, c: (i, 0, 0)),
            state_spec,
            _resident((SSM_CONV, conv_dim)), _resident((1, conv_dim)),
            _resident((1, n_heads)), _resident((1, n_heads)),
            _resident((1, d_inner)), _resident((1, d_inner)),
        ] + [_resident(cst.shape) for cst in consts] + extra_specs,
        out_specs=[pl.BlockSpec((1, t, d_inner), lambda i, c: (i, c, 0)), state_spec],
        out_shape=[jax.ShapeDtypeStruct((b, l, d_inner), BF16), state_out],
        scratch_shapes=[
            pltpu.VMEM((SUBLANES, conv_dim), F32),
            pltpu.VMEM(state_shape, F32),
            pltpu.VMEM((t, d_inner), F32),
        ],
        input_output_aliases=aliases,
        compiler_params=_cparams("parallel", "arbitrary"),
        name="ssd_scan",
    )(*operands, *extra_in)
    if native:
        return y, hfin.reshape(h0.shape)
    hfin = hfin.reshape(b, n_grp, d_st, n_heads // n_grp, SSM_HEAD_DIM)
    hfin = hfin.transpose(0, 1, 3, 4, 2).reshape(b, n_heads, SSM_HEAD_DIM, d_st)
    return y, hfin


def _gla_kernel(p_ref, s0_ref, wg_ref, gb_ref, nw_ref, tri_ref, o_ref, sfin_ref, state,
                *, tb, ch, valid, dk, dv):
    c = pl.program_id(1)
    n_h = GLA_N_HEADS
    hk, hv = dk // n_h, dv // n_h

    @pl.when(c == 0)
    def _init():
        state[...] = s0_ref[0]

    q = p_ref[0, :, 0:dk] * (hk ** -0.5)
    k = p_ref[0, :, dk:2 * dk]
    v = p_ref[0, :, 2 * dk:2 * dk + dv]
    r = p_ref[0, :, 2 * dk + dv:2 * dk + 2 * dv]
    g_low = p_ref[0, :, 2 * dk + 2 * dv:]
    x = _dot(g_low.astype(BF16), wg_ref[...]) + gb_ref[...]
    log_a = -_softplus(-x) / GLA_GATE_NORM
    if valid < tb:
        keep = lax.broadcasted_iota(jnp.int32, (tb, dk), 0) < valid
        log_a = jnp.where(keep, log_a, 0.0)
        k = jnp.where(keep, k, 0.0)

    bcum = _sel_left(tri_ref[...], log_a)
    nch = tb // ch

    def chunks(a):
        return a.reshape(nch, ch, a.shape[-1])

    bc3 = chunks(bcum)
    b_last = bc3[:, ch - 1:ch, :]
    q_t = chunks(q * jnp.exp(bcum)).astype(BF16)
    k_t = chunks(k * jnp.exp(-bcum)).astype(BF16)
    k_dec = (chunks(k) * jnp.exp(b_last - bc3)).astype(BF16)
    e_last = jnp.exp(b_last)
    v_b = chunks(v).astype(BF16)
    causal = (lax.broadcasted_iota(jnp.int32, (nch, ch, ch), 1)
              >= lax.broadcasted_iota(jnp.int32, (nch, ch, ch), 2))

    gate = r * _sigmoid(r)
    nw = nw_ref[...]
    for h in range(n_h):
        ks = slice(h * hk, (h + 1) * hk)
        vs = slice(h * hv, (h + 1) * hv)
        qh, kh, vh = q_t[:, :, ks], k_t[:, :, ks], v_b[:, :, vs]
        upd = jnp.einsum("jkv,jkd->jvd", vh, k_dec[:, :, ks], preferred_element_type=F32)
        s_run = state[h]
        entering = []
        for j in range(nch):
            entering.append(s_run.astype(BF16))
            s_run = s_run * e_last[j, :, ks] + upd[j]
        state[h] = s_run
        att = jnp.einsum("jqd,jkd->jqk", qh, kh, preferred_element_type=F32)
        att = jnp.where(causal, att, 0.0).astype(BF16)
        o = (jnp.einsum("jqk,jkv->jqv", att, vh, preferred_element_type=F32)
             + jnp.einsum("jqd,jvd->jqv", qh, jnp.stack(entering), preferred_element_type=F32))
        o_ref[0, :, vs] = (_rms(o.reshape(tb, hv), nw) * gate[:, vs]).astype(o_ref.dtype)

    @pl.when(c == pl.num_programs(1) - 1)
    def _fin():
        sfin_ref[0] = state[...]


def _gla(proj, s0, w_gate, gate_bias, norm_w, *, tb, ch, valid):
    b, l, in_dim = proj.shape
    n_h = GLA_N_HEADS
    rank, dk = w_gate.shape
    hv = norm_w.shape[0]
    dv = hv * n_h
    hk = dk // n_h
    assert l % tb == 0 and tb % ch == 0 and in_dim == 2 * dk + 2 * dv + rank
    body = functools.partial(_gla_kernel, tb=tb, ch=ch, valid=valid, dk=dk, dv=dv)
    blocktri = jnp.kron(jnp.eye(tb // ch, dtype=F32), jnp.tril(jnp.ones((ch, ch), F32))).astype(BF16)
    o, sfin = pl.pallas_call(
        body,
        grid=(b, l // tb),
        in_specs=[
            pl.BlockSpec((1, tb, in_dim), lambda i, c: (i, c, 0)),
            pl.BlockSpec((1, n_h, hv, hk), lambda i, c: (i, 0, 0, 0)),
            _resident((rank, dk)), _resident((1, dk)), _resident((1, hv)), _resident((tb, tb)),
        ],
        out_specs=[
            pl.BlockSpec((1, tb, dv), lambda i, c: (i, c, 0)),
            pl.BlockSpec((1, n_h, hv, hk), lambda i, c: (i, 0, 0, 0)),
        ],
        out_shape=[
            jax.ShapeDtypeStruct((b, l, dv), BF16),
            jax.ShapeDtypeStruct((b, n_h, hv, hk), F32),
        ],
        scratch_shapes=[pltpu.VMEM((n_h, hv, hk), F32)],
        compiler_params=_cparams("parallel", "arbitrary"),
        name="gla_scan",
    )(proj, jnp.swapaxes(s0, -1, -2), w_gate.astype(BF16), gate_bias.reshape(1, -1),
      norm_w.reshape(1, -1), blocktri)
    return o, jnp.swapaxes(sfin, -1, -2)


def _attn_prompt_kernel(q_ref, kp_ref, kc_ref, vp_ref, vc_ref, o_ref, lse_ref, *, dilation, nb, span):
    not_first = pl.program_id(2) > 0
    blk = ATT_BLOCK
    dh = ATT_HEAD_DIM
    units = [(r, jb) for r in range(dilation) for jb in range(nb)]
    n_u = len(units)
    iu = lax.broadcasted_iota(jnp.int32, (n_u, blk, blk), 0)
    iq = lax.broadcasted_iota(jnp.int32, (n_u, blk, blk), 1)
    ik = lax.broadcasted_iota(jnp.int32, (n_u, blk, blk), 2)
    ok_prev = (ik >= iq + (blk - span)) & (((iu & (nb - 1)) != 0) | not_first)
    ok_cur = ik <= iq
    scale = dh ** -0.5

    def rows(r, jb):
        start = r + dilation * jb * blk
        return pl.ds(start, blk) if dilation == 1 else pl.ds(start, blk, stride=dilation)

    def gather(cur_ref, prev_ref):
        cur = jnp.stack([cur_ref[0, rows(r, jb), :] for r, jb in units])
        prev = jnp.stack([prev_ref[0, rows(r, nb - 1), :] if jb == 0 else cur_ref[0, rows(r, jb - 1), :]
                          for r, jb in units])
        return cur.astype(BF16), prev.astype(BF16)

    q = jnp.stack([q_ref[0, rows(r, jb), :] for r, jb in units]).astype(BF16)
    k_cur, k_prev = gather(kc_ref, kp_ref)
    v_cur, v_prev = gather(vc_ref, vp_ref)
    outs, lses = [], []
    for hh in range(LANES // dh):
        hs = slice(hh * dh, (hh + 1) * dh)
        qh = q[:, :, hs]
        s_p = jnp.einsum("uqd,ukd->uqk", qh, k_prev[:, :, hs], preferred_element_type=F32) * scale
        s_c = jnp.einsum("uqd,ukd->uqk", qh, k_cur[:, :, hs], preferred_element_type=F32) * scale
        s_p = jnp.where(ok_prev, s_p, -jnp.inf)
        s_c = jnp.where(ok_cur, s_c, -jnp.inf)
        mx = jnp.max(jnp.maximum(s_p, s_c), axis=-1, keepdims=True)
        p_p = jnp.exp(s_p - mx)
        p_c = jnp.exp(s_c - mx)
        den = jnp.sum(p_p + p_c, axis=-1, keepdims=True)
        outs.append(jnp.einsum("uqk,ukd->uqd", (p_p / den).astype(BF16), v_prev[:, :, hs],
                               preferred_element_type=F32)
                    + jnp.einsum("uqk,ukd->uqd", (p_c / den).astype(BF16), v_cur[:, :, hs],
                                 preferred_element_type=F32))
        lses.append(jnp.broadcast_to(mx + jnp.log(den), (n_u, blk, dh)))
    o = jnp.concatenate(outs, axis=-1)
    lse = jnp.concatenate(lses, axis=-1)
    for u, (r, jb) in enumerate(units):
        o_ref[0, rows(r, jb), :] = o[u]
        lse_ref[0, rows(r, jb), :] = lse[u]


def _attn_prompt(qkv, gi, *, b, l, window, dilation):
    n_slab = qkv.shape[0] // 3
    spg = ATT_HEADS_PER_GROUP * ATT_HEAD_DIM // LANES
    span = window // dilation
    blk = ATT_BLOCK
    nb = max(1, min(l, ATT_SUPER_ROWS) // (dilation * blk))
    r_rows = dilation * nb * blk
    n_sup = l // r_rows
    assert l % r_rows == 0 and span <= blk and qkv.shape[1] == b * l and nb & (nb - 1) == 0

    def spec(part, prev):
        def index(bi, sp, i):
            i = jnp.maximum(i - 1, 0) if prev else i
            return (part * n_slab + gi * spg + sp, bi * n_sup + i, 0)
        return pl.BlockSpec((1, r_rows, LANES), index)

    out_spec = pl.BlockSpec((1, r_rows, LANES), lambda bi, sp, i: (sp, bi * n_sup + i, 0))
    return pl.pallas_call(
        functools.partial(_attn_prompt_kernel, dilation=dilation, nb=nb, span=span),
        grid=(b, spg, n_sup),
        in_specs=[spec(0, False), spec(1, True), spec(1, False), spec(2, True), spec(2, False)],
        out_specs=[out_spec, out_spec],
        out_shape=[jax.ShapeDtypeStruct((spg, b * l, LANES), F32)] * 2,
        compiler_params=_cparams("parallel", "parallel", "parallel"),
        name="attn_prompt",
    )(qkv, qkv, qkv, qkv, qkv)


def _attn_sample_kernel(q_ref, kn_ref, vn_ref, cache_ref, o_ref, lse_ref, cout_ref,
                        *, window, dilation, n_new, seqs):
    dh = ATT_HEAD_DIM
    rows = seqs * n_new
    scale = dh ** -0.5
    row = lax.broadcasted_iota(jnp.int32, (rows, window), 0)
    col = lax.broadcasted_iota(jnp.int32, (rows, window), 1)
    sq = row & (n_new - 1)
    ok_cache = (col >= sq) & (((col - sq) & (dilation - 1)) == 0)
    row_n = lax.broadcasted_iota(jnp.int32, (rows, rows), 0)
    col_n = lax.broadcasted_iota(jnp.int32, (rows, rows), 1)
    back = (row_n & (n_new - 1)) - (col_n & (n_new - 1))
    same_res = (back >= 0) & ((back & (dilation - 1)) == 0)
    lane_t = lax.broadcasted_iota(jnp.int32, (2 * rows, LANES), 1)
    tok_t = lax.broadcasted_iota(jnp.int32, (2 * rows, LANES), 0)
    lane_o = lax.broadcasted_iota(jnp.int32, (dh, LANES), 1)
    zpad = jnp.zeros((rows, dh), F32)
    for bb in range(seqs):
        r0 = bb * n_new
        ok_new = same_res & (col_n >= r0) & (col_n < r0 + n_new)
        place = jnp.where(lane_t - (LANES - n_new) == tok_t - r0, 1.0, 0.0).astype(BF16)
        for h in range(ATT_HEADS_PER_GROUP):
            sl = h * dh // LANES
            hs = slice(h * dh % LANES, h * dh % LANES + dh)
            q8 = q_ref[sl][:, hs].astype(BF16)
            k_new, v_new = kn_ref[sl][:, hs], vn_ref[sl][:, hs]
            k_t, v_t = cache_ref[bb, 0, h], cache_ref[bb, 1, h]
            s_c = jnp.where(ok_cache, _dot(q8, k_t.astype(BF16)) * scale, -jnp.inf)
            s_n = lax.dot_general(q8, k_new.astype(BF16), _NT, preferred_element_type=F32) * scale
            s_n = jnp.where(ok_new, s_n, -jnp.inf)
            mx = jnp.maximum(jnp.max(s_c, axis=-1, keepdims=True), jnp.max(s_n, axis=-1, keepdims=True))
            p_c = jnp.exp(s_c - mx)
            p_n = jnp.exp(s_n - mx)
            den = jnp.sum(p_c, axis=-1, keepdims=True) + jnp.sum(p_n, axis=-1, keepdims=True)
            o = (lax.dot_general((p_c / den).astype(BF16), v_t.astype(BF16), _NT, preferred_element_type=F32)
                 + _dot((p_n / den).astype(BF16), v_new.astype(BF16)))
            o_ref[sl, r0:r0 + n_new, hs] = o[r0:r0 + n_new]
            lse_ref[sl, r0:r0 + n_new, hs] = jnp.broadcast_to(mx + jnp.log(den), (rows, dh))[r0:r0 + n_new]
            for kv, old, new in ((0, k_t, k_new), (1, v_t, v_new)):
                hi, mid, lo = _split3(jnp.concatenate([new, zpad], axis=0))
                tail = sum(lax.dot_general(part, place, _TN, preferred_element_type=F32)
                           for part in (hi, mid, lo))
                shifted = pltpu.roll(old, window - n_new, 1)
                last = jnp.where(lane_o >= LANES - n_new, tail, shifted[:, window - LANES:])
                if window > LANES:
                    cout_ref[bb, kv, h, :, :window - LANES] = shifted[:, :window - LANES]
                cout_ref[bb, kv, h, :, window - LANES:] = last


def _attn_sample(qkv, cache_t, gi, *, window, dilation, n_new):
    n_slab = qkv.shape[0] // 3
    spg = ATT_HEADS_PER_GROUP * ATT_HEAD_DIM // LANES
    b = cache_t.shape[0]
    seqs = SUBLANES // n_new
    rows = seqs * n_new
    assert rows == SUBLANES and b % seqs == 0 and qkv.shape[1] == b * n_new
    assert cache_t.shape[-1] == window and window % dilation == 0 and window % LANES == 0
    assert dilation & (dilation - 1) == 0 and n_new & (n_new - 1) == 0

    def slabs(part):
        return pl.BlockSpec((spg, rows, LANES), lambda i: (part * n_slab // spg + gi, i, 0))

    cache_spec = pl.BlockSpec((seqs,) + cache_t.shape[1:], lambda i: (i, 0, 0, 0, 0))
    out_spec = pl.BlockSpec((spg, rows, LANES), lambda i: (0, i, 0))
    return pl.pallas_call(
        functools.partial(_attn_sample_kernel, window=window, dilation=dilation, n_new=n_new, seqs=seqs),
        grid=(b // seqs,),
        in_specs=[slabs(0), slabs(1), slabs(2), cache_spec],
        out_specs=[out_spec, out_spec, cache_spec],
        out_shape=[jax.ShapeDtypeStruct((spg, b * n_new, LANES), F32)] * 2
        + [jax.ShapeDtypeStruct(cache_t.shape, F32)],
        compiler_params=_cparams("parallel"),
        name="attn_sample",
    )(qkv, qkv, qkv, cache_t)


def _attn_out_kernel(o0_ref, o1_ref, o2_ref, l0_ref, l1_ref, l2_ref, w_ref, x_ref, out_ref):
    o_refs, l_refs = (o0_ref, o1_ref, o2_ref), (l0_ref, l1_ref, l2_ref)
    spg = o0_ref.shape[0]
    pieces = [[None] * spg for _ in o_refs]
    for s in range(spg):
        ls = [l_ref[s] for l_ref in l_refs]
        mx = jnp.maximum(jnp.maximum(ls[0], ls[1]), ls[2])
        es = [jnp.exp(l - mx) for l in ls]
        den = es[0] + es[1] + es[2]
        for g, (o_ref, e) in enumerate(zip(o_refs, es)):
            pieces[g][s] = o_ref[s] * (e / den)
    mixed = jnp.concatenate([p for grp in pieces for p in grp], axis=-1).astype(BF16)
    out_ref[...] = x_ref[...] + _dot(mixed, w_ref[...])


def _attn_out(os_, lses, wl, x, *, tm):
    m, d = x.shape
    w = wl[0]
    spg = os_[0].shape[0]
    row = pl.BlockSpec((spg, tm, LANES), lambda i: (0, i, 0))
    return pl.pallas_call(
        _attn_out_kernel,
        grid=(m // tm,),
        in_specs=[row] * 6 + [_layer(wl), pl.BlockSpec((tm, d), lambda i: (i, 0))],
        out_specs=pl.BlockSpec((tm, d), lambda i: (i, 0)),
        out_shape=jax.ShapeDtypeStruct((m, d), F32),
        compiler_params=_cparams("parallel"),
        name="attn_mix_proj",
    )(*os_, *lses, w, x)


def _rope_tables(pos):
    half = ATT_ROT_DIM // 2
    lane = lax.broadcasted_iota(jnp.int32, (pos.shape[0], LANES), 1) % ATT_HEAD_DIM
    x1, rot = lane < half, lane < ATT_ROT_DIM
    freq = jnp.where(x1, lane, lane - half)
    inv_freq = ROPE_THETA ** (-(2 * freq).astype(F32) / ATT_ROT_DIM)
    ang = pos.astype(F32)[:, None] * inv_freq
    cos, sin = jnp.cos(ang), jnp.sin(ang)
    c = jnp.where(rot, cos, 1.0)
    s1 = jnp.where(rot & jnp.logical_not(x1), sin, 0.0)
    s2 = jnp.where(x1, -sin, 0.0)
    return c, s1, s2


def _pad_rows(t, rows):
    return jnp.pad(t, ((0, 0), (0, rows - t.shape[1]), (0, 0)))


def kernel(x_prompt, x_sample, state_ssm, state_ssm_conv, state_gla, cache_kv_g0, cache_kv_g1, cache_kv_g2,
           norm_mix, norm_ffn, ffn_gate, ffn_up, ffn_down,
           ssm_w_in, ssm_conv_w, ssm_conv_b, ssm_dt_bias, ssm_a_log, ssm_d, ssm_norm, ssm_w_out,
           gla_w_in, gla_w_gate, gla_gate_bias, gla_norm, gla_w_out,
           att_w_qkv, att_w_out, norm_final):
    bp, lp, d = x_prompt.shape
    bs, ls, _ = x_sample.shape
    depth = norm_mix.shape[0]
    mp, ms = bp * lp, bs * ls
    tm_p = 512 if mp % 512 == 0 else mp
    tm_s = ms
    xp = x_prompt.reshape(mp, d)
    xs = x_sample.reshape(ms, d)
    att_caches = (cache_kv_g0, cache_kv_g1, cache_kv_g2)
    hg_w = ATT_HEADS_PER_GROUP * ATT_HEAD_DIM
    n_att = hg_w * len(ATT_GROUPS)
    rope_p = _rope_tables(jnp.arange(lp))
    rope_s = _rope_tables(jnp.tile(PAST_LEN + jnp.arange(ls), bs))
    ffn_w = tuple(w.astype(BF16) for w in (ffn_gate, ffn_up, ffn_down))
    ssm_w_in, ssm_w_out, gla_w_in, gla_w_out, att_w_qkv, att_w_out = (
        w.astype(BF16) for w in (ssm_w_in, ssm_w_out, gla_w_in, gla_w_out, att_w_qkv, att_w_out))

    ssm_p, conv_p, conv_s, gla_p, gla_s = [], [], [], [], []
    ssm_s = None
    kv_p, kv_s = [[], [], []], [[], [], []]
    for i in range(depth):
        m, j = i % N_MIXERS, i // N_MIXERS
        if m == 0:
            w_in = (ssm_w_in, j)
            conv_dim = ssm_conv_w.shape[2]
            d_inner = ssm_w_out.shape[1]
            wts = (ssm_conv_w[j], ssm_conv_b[j], ssm_dt_bias[j], ssm_a_log[j], ssm_d[j], ssm_norm[j])
            conv0 = jnp.zeros((bp, SSM_CONV - 1, conv_dim), F32)
            zx_p, tails = _ssd_in_proj(xp, norm_mix[i], w_in, conv0, ssm_conv_w[j], ssm_conv_b[j],
                                       tm=tm_p, seq_len=lp, d_inner=d_inner)
            zx_p = zx_p.reshape(bp, lp, -1)
            zx_s = _norm_matmul(xs, norm_mix[i], w_in, tm=tm_s).reshape(bs, ls, -1)
            t_p = min(SSM_CHUNK, lp)
            y_p, h_p = _ssd(zx_p, conv0, jnp.zeros((bp,) + state_ssm.shape[2:], F32), *wts,
                            t=t_p, valid=t_p, conv_done=True)
            y_s, ssm_s = _ssd(_pad_rows(zx_s, SAMPLE_PAD), state_ssm_conv[j], state_ssm, *wts,
                              t=SAMPLE_PAD, valid=ls, stacked=(j, ssm_s))
            y_s = y_s[:, :ls]
            xbc_s = zx_s[:, :, d_inner:d_inner + conv_dim]
            conv_p.append(tails.reshape(bp, lp // tm_p, SUBLANES, conv_dim)[:, -1, -(SSM_CONV - 1):])
            conv_s.append(jnp.concatenate([state_ssm_conv[j], xbc_s], axis=1)[:, -(SSM_CONV - 1):])
            ssm_p.append(h_p)
            w_out = (ssm_w_out, j)
            xp = _proj_res(y_p.reshape(mp, -1), w_out, xp, tm=tm_p)
            xs = _proj_res(y_s.reshape(ms, -1), w_out, xs, tm=tm_s)
        elif m == 1:
            w_in = (gla_w_in, j)
            wts = (gla_w_gate[j], gla_gate_bias[j], gla_norm[j])
            pr_p = _norm_matmul(xp, norm_mix[i], w_in, tm=tm_p).reshape(bp, lp, -1)
            pr_s = _norm_matmul(xs, norm_mix[i], w_in, tm=tm_s).reshape(bs, ls, -1)
            ch_p = min(GLA_CHUNK, lp)
            tb_p = 256 if lp % 256 == 0 else ch_p
            o_p, s_p = _gla(pr_p, jnp.zeros((bp,) + state_gla.shape[2:], F32), *wts,
                            tb=tb_p, ch=ch_p, valid=tb_p)
            o_s, s_s = _gla(_pad_rows(pr_s, SAMPLE_PAD), state_gla[j], *wts,
                            tb=SAMPLE_PAD, ch=SAMPLE_PAD, valid=ls)
            o_s = o_s[:, :ls]
            gla_p.append(s_p)
            gla_s.append(s_s)
            w_out = (gla_w_out, j)
            xp = _proj_res(o_p.reshape(mp, -1), w_out, xp, tm=tm_p)
            xs = _proj_res(o_s.reshape(ms, -1), w_out, xs, tm=tm_s)
        else:
            w_qkv = (att_w_qkv, j)
            qkv_p = _norm_matmul(xp, norm_mix[i], w_qkv, tm=tm_p, rope=(2 * n_att, rope_p))
            qkv_s = _norm_matmul(xs, norm_mix[i], w_qkv, tm=tm_s, rope=(2 * n_att, rope_s))
            n_slab = n_att // LANES
            spg = hg_w // LANES
            qkv_p4 = qkv_p.reshape(3 * n_slab, bp, lp, LANES)
            os_p, ls_p, os_s, ls_s = [], [], [], []
            for gi, (window, dilation) in enumerate(ATT_GROUPS):
                o, lse = _attn_prompt(qkv_p, gi, b=bp, l=lp, window=window, dilation=dilation)
                os_p.append(o)
                ls_p.append(lse)
                keep = min(window, lp)
                kv = jnp.stack([qkv_p4[part * n_slab + gi * spg:part * n_slab + (gi + 1) * spg, :, lp - keep:]
                                for part in (1, 2)])
                kv = kv.reshape(2, spg, bp, keep, LANES // ATT_HEAD_DIM, ATT_HEAD_DIM)
                kv_p[gi].append(kv.transpose(2, 3, 0, 1, 4, 5)
                                .reshape(bp, keep, 2, ATT_HEADS_PER_GROUP, ATT_HEAD_DIM))
                cache_t = att_caches[gi][j].transpose(0, 2, 3, 4, 1)
                o, lse, cache_new = _attn_sample(qkv_s, cache_t, gi, window=window, dilation=dilation, n_new=ls)
                os_s.append(o)
                ls_s.append(lse)
                kv_s[gi].append(cache_new.transpose(0, 4, 1, 2, 3))
            w_out = (att_w_out, j)
            xp = _attn_out(os_p, ls_p, w_out, xp, tm=tm_p)
            xs = _attn_out(os_s, ls_s, w_out, xs, tm=tm_s)
        final_g = norm_final if i == depth - 1 else None
        xp = _ffn(xp, norm_ffn[i], *ffn_w, i, tm=tm_p, final_g=final_g)
        xs = _ffn(xs, norm_ffn[i], *ffn_w, i, tm=tm_s, final_g=final_g)
    y_prompt = xp.reshape(bp, lp, d)
    y_sample = xs.reshape(bs, ls, d)
    return (y_prompt, y_sample,
            jnp.stack(ssm_p), ssm_s, jnp.stack(conv_p), jnp.stack(conv_s),
            jnp.stack(gla_p), jnp.stack(gla_s),
            jnp.stack(kv_p[0]), jnp.stack(kv_s[0]), jnp.stack(kv_p[1]), jnp.stack(kv_s[1]),
            jnp.stack(kv_p[2]), jnp.stack(kv_s[2]))
```

```python
import functools

import jax
import jax.numpy as jnp
from jax import lax
from jax.experimental import pallas as pl
from jax.experimental.pallas import tpu as pltpu

F32 = jnp.float32
BF16 = jnp.bfloat16
LOG2_E = 1.4426950408889634

NORM_EPS = 1e-6
N_MIXERS = 3

SSM_HEAD_DIM = 64
SSM_N_GROUPS = 4
SSM_D_STATE = 128
SSM_CONV = 4
SSM_CHUNK = 128
SSM_IN_PROJ_CHUNKS = 4

GLA_N_HEADS = 4
GLA_GATE_NORM = 16.0
GLA_CHUNK = 32

ATT_GROUPS = ((128, 1), (512, 4), (2048, 16))
ATT_HEADS_PER_GROUP = 4
ATT_HEAD_DIM = 64
ATT_ROT_DIM = ATT_HEAD_DIM // 4
ROPE_THETA = 500000.0
ATT_BLOCK = 128
ATT_SUPER_ROWS = 2048
PAST_LEN = 8192

LANES = 128
SUBLANES = 8
SAMPLE_PAD = 16
VMEM_LIMIT = 56 * 1024 * 1024

_NT = (((1,), (1,)), ((), ()))
_TN = (((0,), (0,)), ((), ()))


def _cparams(*sem):
    return pltpu.CompilerParams(dimension_semantics=sem, vmem_limit_bytes=VMEM_LIMIT)


def _resident(shape):
    zeros = (0,) * len(shape)
    return pl.BlockSpec(shape, lambda *_: zeros, pipeline_mode=pl.Buffered(1))


def _layer(wl):
    w, layer = wl
    return pl.BlockSpec((None,) + w.shape[1:], lambda *_: (layer, 0, 0), pipeline_mode=pl.Buffered(1))


def _rms(x, g):
    inv = lax.rsqrt(jnp.mean(x * x, axis=-1, keepdims=True) + NORM_EPS)
    return x * inv * g


def _sigmoid(x):
    return 1.0 / (1.0 + jnp.exp(-x))


def _softplus(x):
    return jnp.maximum(x, 0.0) + jnp.log1p(jnp.exp(-jnp.abs(x)))


def _split3(v):
    hi = v.astype(BF16)
    r = v - hi.astype(F32)
    mid = r.astype(BF16)
    lo = (r - mid.astype(F32)).astype(BF16)
    return hi, mid, lo


def _dot(a, b):
    return jnp.dot(a, b, preferred_element_type=F32)


def _spread_matrix(n_blocks, width):
    k_dim = -(-3 * n_blocks // LANES) * LANES
    row = lax.broadcasted_iota(jnp.int32, (k_dim, n_blocks * width), 0)
    col = lax.broadcasted_iota(jnp.int32, (k_dim, n_blocks * width), 1)
    blk = row - jnp.where(row >= 2 * n_blocks, 2 * n_blocks, jnp.where(row >= n_blocks, n_blocks, 0))
    sel = (row < 3 * n_blocks) & (col >= blk * width) & (col < blk * width + width)
    return jnp.where(sel, 1.0, 0.0).astype(BF16)


def _spread(v, sel):
    rows, n_blocks = v.shape
    terms = [p.astype(F32) for p in _split3(v)]
    pad = jnp.zeros((rows, sel.shape[0] - 3 * n_blocks), F32)
    return _dot(jnp.concatenate(terms + [pad], axis=1).astype(BF16), sel)


def _sel_left(e, v):
    hi, mid, lo = _split3(v)
    return _dot(e, hi) + _dot(e, mid) + _dot(e, lo)


def _norm_matmul_kernel(x_ref, g_ref, w_ref, o_ref):
    h = _rms(x_ref[...], g_ref[...]).astype(BF16)
    o_ref[...] = _dot(h, w_ref[...])


def _norm_matmul_rope_kernel(x_ref, g_ref, w_ref, c_ref, s1_ref, s2_ref, o_ref, *, n_rot):
    h = _rms(x_ref[...], g_ref[...]).astype(BF16)
    res = _dot(h, w_ref[...])
    c, s1, s2 = c_ref[...], s1_ref[...], s2_ref[...]
    half = ATT_ROT_DIM // 2
    for j in range(o_ref.shape[0]):
        x = res[:, j * LANES:(j + 1) * LANES]
        if j < n_rot // LANES:
            x = x * c + pltpu.roll(x, half, 1) * s1 + pltpu.roll(x, LANES - half, 1) * s2
        o_ref[j] = x


def _norm_matmul(x, g, wl, *, tm, rope=None):
    m, d = x.shape
    n = wl[0].shape[2]
    assert m % tm == 0
    in_specs = [pl.BlockSpec((tm, d), lambda i: (i, 0)), _resident((1, d)), _layer(wl)]
    args = [x, g.reshape(1, d), wl[0]]
    if rope is None:
        body = _norm_matmul_kernel
        out_spec = pl.BlockSpec((tm, n), lambda i: (i, 0))
        out_shape = jax.ShapeDtypeStruct((m, n), F32)
    else:
        n_rot, tables = rope
        assert n % LANES == 0 and n_rot % LANES == 0
        body = functools.partial(_norm_matmul_rope_kernel, n_rot=n_rot)
        period = tables[0].shape[0] // tm
        assert tables[0].shape[0] % tm == 0 and (m // tm) % period == 0
        in_specs += [pl.BlockSpec((tm, LANES), lambda i: (i % period, 0))] * 3
        args += list(tables)
        out_spec = pl.BlockSpec((n // LANES, tm, LANES), lambda i: (0, i, 0))
        out_shape = jax.ShapeDtypeStruct((n // LANES, m, LANES), F32)
    return pl.pallas_call(
        body,
        grid=(m // tm,),
        in_specs=in_specs,
        out_specs=out_spec,
        out_shape=out_shape,
        compiler_params=_cparams("parallel"),
        name="norm_proj" if rope is None else "norm_proj_rope",
    )(*args)


def _conv_silu(raw, before, cw, cb):
    t, c = raw.shape
    groups = [before] + [raw[r:r + SUBLANES] for r in range(0, t, SUBLANES)]
    row8 = lax.broadcasted_iota(jnp.int32, (SUBLANES, c), 0)
    acc = cb
    for k in range(SSM_CONV - 1, 0, -1):
        rolled = [pltpu.roll(grp, k, 0) for grp in groups]
        shifted = jnp.concatenate([jnp.where(row8 < k, prev, here)
                                   for prev, here in zip(rolled[:-1], rolled[1:])], axis=0)
        acc = acc + shifted * cw[SSM_CONV - 1 - k:SSM_CONV - k, :]
    acc = acc + raw * cw[SSM_CONV - 1:SSM_CONV, :]
    return acc * _sigmoid(acc), groups[-1]


def _ssd_in_proj_kernel(x_ref, g_ref, w_ref, cbuf_ref, cw_ref, cb_ref, o_ref, tail_ref, tail,
                        *, d_inner, conv_dim, tiles_per_seq, col_chunk):
    @pl.when(pl.program_id(0) % tiles_per_seq == 0)
    def _start_of_sequence():
        tail[...] = cbuf_ref[0]

    h = _rms(x_ref[...], g_ref[...]).astype(BF16)
    n_chunks = conv_dim // col_chunk
    z_chunk = d_inner // n_chunks
    for j in range(n_chunks):
        c0 = j * col_chunk
        cs = slice(c0, c0 + col_chunk)
        raw = _dot(h, w_ref[:, d_inner + c0:d_inner + c0 + col_chunk])
        zs = slice(j * z_chunk, (j + 1) * z_chunk)
        o_ref[:, zs] = _dot(h, w_ref[:, zs])
        act, last = _conv_silu(raw, tail[:, cs], cw_ref[:, cs], cb_ref[:, cs])
        o_ref[:, d_inner + c0:d_inner + c0 + col_chunk] = act
        tail[:, cs] = last
        tail_ref[0, :, cs] = last
    o_ref[:, d_inner + conv_dim:] = _dot(h, w_ref[:, d_inner + conv_dim:])


def _ssd_in_proj(x, g, wl, conv_buf, conv_w, conv_b, *, tm, seq_len, d_inner):
    m, d = x.shape
    n = wl[0].shape[2]
    conv_dim = conv_w.shape[1]
    assert m % tm == 0 and seq_len % tm == 0
    cbuf = jnp.pad(conv_buf, ((0, 0), (SUBLANES - (SSM_CONV - 1), 0), (0, 0)))
    tiles_per_seq = seq_len // tm
    body = functools.partial(_ssd_in_proj_kernel, d_inner=d_inner, conv_dim=conv_dim,
                             tiles_per_seq=tiles_per_seq, col_chunk=conv_dim // SSM_IN_PROJ_CHUNKS)
    assert conv_dim % (SSM_IN_PROJ_CHUNKS * LANES) == 0 and d_inner % (SSM_IN_PROJ_CHUNKS * LANES) == 0
    return pl.pallas_call(
        body,
        grid=(m // tm,),
        in_specs=[pl.BlockSpec((tm, d), lambda i: (i, 0)), _resident((1, d)), _layer(wl),
                  pl.BlockSpec((1, SUBLANES, conv_dim), lambda i: (i // tiles_per_seq, 0, 0)),
                  _resident((SSM_CONV, conv_dim)), _resident((1, conv_dim))],
        out_specs=[pl.BlockSpec((tm, n), lambda i: (i, 0)),
                   pl.BlockSpec((1, SUBLANES, conv_dim), lambda i: (i, 0, 0))],
        out_shape=[jax.ShapeDtypeStruct((m, n), F32),
                   jax.ShapeDtypeStruct((m // tm, SUBLANES, conv_dim), F32)],
        scratch_shapes=[pltpu.VMEM((SUBLANES, conv_dim), F32)],
        compiler_params=_cparams("arbitrary"),
        name="ssd_in_proj",
    )(x, g.reshape(1, d), wl[0], cbuf, conv_w, conv_b.reshape(1, -1))


def _mixed_heads(o_refs, l_refs):
    spg = o_refs[0].shape[0]
    pieces = [[None] * spg for _ in o_refs]
    for s in range(spg):
        ls = [l_ref[s] for l_ref in l_refs]
        mx = functools.reduce(jnp.maximum, ls)
        es = [jnp.exp(l - mx) for l in ls]
        den = functools.reduce(lambda a, b: a + b, es)
        for grp, (o_ref, e) in enumerate(zip(o_refs, es)):
            pieces[grp][s] = o_ref[s] * (e / den)
    return jnp.concatenate([p for grp in pieces for p in grp], axis=-1).astype(BF16)


def _block_tail_kernel(*refs, n_groups, final_norm):
    n_mix = 2 * n_groups if n_groups else 1
    wo_ref, x_ref, g_ref, wg_ref, wu_ref, wd_ref, gf_ref, o_ref = refs[n_mix:]
    y = _mixed_heads(refs[:n_groups], refs[n_groups:n_mix]) if n_groups else refs[0][...]
    x = x_ref[...] + _dot(y, wo_ref[...])
    h = _rms(x, g_ref[...]).astype(BF16)
    gate = _dot(h, wg_ref[...])
    up = _dot(h, wu_ref[...])
    act = (gate * _sigmoid(gate) * up).astype(BF16)
    x = x + _dot(act, wd_ref[...])
    o_ref[...] = _rms(x, gf_ref[...]) if final_norm else x


def _block_tail(mix, wol, x, g, wg, wu, wd, layer, *, tm, final_g=None):
    m, d = x.shape
    assert m % tm == 0
    gf = g if final_g is None else final_g
    if isinstance(mix, tuple):
        os_, lses = mix
        n_groups = len(os_)
        mix_args = [*os_, *lses]
        mix_specs = [pl.BlockSpec((os_[0].shape[0], tm, LANES), lambda i: (0, i, 0))] * (2 * n_groups)
    else:
        n_groups = 0
        mix_args = [mix]
        mix_specs = [pl.BlockSpec((tm, mix.shape[1]), lambda i: (i, 0))]
    return pl.pallas_call(
        functools.partial(_block_tail_kernel, n_groups=n_groups, final_norm=final_g is not None),
        grid=(m // tm,),
        in_specs=mix_specs + [_layer(wol), pl.BlockSpec((tm, d), lambda i: (i, 0)), _resident((1, d)),
                              _layer((wg, layer)), _layer((wu, layer)), _layer((wd, layer)),
                              _resident((1, d))],
        out_specs=pl.BlockSpec((tm, d), lambda i: (i, 0)),
        out_shape=jax.ShapeDtypeStruct((m, d), F32),
        compiler_params=_cparams("parallel"),
        name="proj_swiglu",
    )(*mix_args, wol[0], x, g.reshape(1, d), wg, wu, wd, gf.reshape(1, d))


def _ssd_kernel(zx_ref, cbuf_ref, h0_ref, cw_ref, cb_ref, dtb_ref, alog_ref, dsk_ref, nw_ref,
                tri_ref, triu_ref, sel_hp_ref, sel_hk_ref, *rest,
                t, valid, d_inner, n_heads, conv_done, native):
    y_ref, hfin_ref, tail, state, yacc = rest[-5:]
    c = pl.program_id(1)
    n_grp = SSM_N_GROUPS
    d_st = SSM_D_STATE
    hp = d_inner // n_grp
    hpg = n_heads // n_grp
    pdim = SSM_HEAD_DIM
    bc_w = n_grp * d_st
    conv_dim = d_inner + 2 * bc_w

    @pl.when(c == 0)
    def _init():
        state[...] = h0_ref[0]
        tail[...] = cbuf_ref[0]

    xbc = zx_ref[0, :, d_inner:d_inner + conv_dim]
    if not conv_done:
        xbc, tail[...] = _conv_silu(xbc, tail[...], cw_ref[...], cb_ref[...])
    xs = xbc[:, :d_inner]
    bm = xbc[:, d_inner:d_inner + bc_w]
    cm = xbc[:, d_inner + bc_w:]

    dt = _softplus(zx_ref[0, :, d_inner + conv_dim:] + dtb_ref[...])
    if valid < t:
        dt = jnp.where(lax.broadcasted_iota(jnp.int32, dt.shape, 0) < valid, dt, 0.0)
    a = dt * (-jnp.exp(alog_ref[...]))

    iq = lax.broadcasted_iota(jnp.int32, (t, t), 0)
    ik = lax.broadcasted_iota(jnp.int32, (t, t), 1)
    low = iq >= ik
    a_cs = _sel_left(tri_ref[...], a)
    a_cs_t = sum(lax.dot_general(part, triu_ref[...], _TN, preferred_element_type=F32)
                 for part in _split3(a))

    a_last = a_cs[t - 1:t, :]
    pad_rows = 2 * SUBLANES
    stack = jnp.concatenate([dt, dt * jnp.exp(a_last - a_cs), jnp.exp(a_cs),
                             jnp.broadcast_to(jnp.exp(a_last), (pad_rows, n_heads))], axis=0)
    ex = _spread(stack, sel_hp_ref[...])
    xdt_b = (xs * ex[0:t]).astype(BF16)
    xdst_b = (xs * ex[t:2 * t]).astype(BF16)
    from_start = ex[2 * t:3 * t]
    chunk_decay = ex[3 * t:3 * t + pad_rows]
    if native:
        only0 = jnp.where(lax.broadcasted_iota(jnp.int32, chunk_decay.shape, 0) == 0, chunk_decay, 0.0)
        ones = jnp.ones((pad_rows, d_st), BF16)
        decay_rows = sum(lax.dot_general(part, ones, _TN, preferred_element_type=F32)
                         for part in _split3(only0))

    a_col = _spread(a_cs * LOG2_E, sel_hk_ref[...])
    a_row = a_cs_t * LOG2_E

    pair = LANES // pdim
    for g in range(n_grp):
        bg = bm[:, g * d_st:(g + 1) * d_st].astype(BF16)
        cg = cm[:, g * d_st:(g + 1) * d_st].astype(BF16)
        cb = lax.dot_general(cg, bg, _NT, preferred_element_type=F32)
        sg = state[g]
        sg_b = sg.astype(BF16)
        gl = slice(g * hp, (g + 1) * hp)
        carried = lax.dot_general(cg, sg_b, _NT, preferred_element_type=F32) if native else _dot(cg, sg_b)
        yacc[:, gl] = carried * from_start[:, gl]
        for h0 in range(0, hpg, pair):
            outs = []
            for hh in range(g * hpg + h0, g * hpg + h0 + pair):
                ks = slice(hh * t, (hh + 1) * t)
                ps = slice(hh * pdim, (hh + 1) * pdim)
                decay = jnp.exp2(jnp.where(low, a_col[:, ks] - a_row[hh:hh + 1, :], -jnp.inf))
                outs.append(_dot((cb * decay).astype(BF16), xdt_b[:, ps]))
            lo = (g * hpg + h0) * pdim
            yacc[:, lo:lo + pair * pdim] += jnp.concatenate(outs, axis=1)
        if native:
            upd = lax.dot_general(xdst_b[:, gl], bg, _TN, preferred_element_type=F32)
            state[g] = sg * decay_rows[gl, :] + upd
        else:
            upd = lax.dot_general(bg, xdst_b[:, gl], _TN, preferred_element_type=F32)
            state[g] = sg * chunk_decay[0:1, gl] + upd

    z = zx_ref[0, :, :d_inner]
    y = (yacc[...] + xs * dsk_ref[...]) * (z * _sigmoid(z))
    nw = nw_ref[...]
    for g in range(n_grp):
        gl = slice(g * hp, (g + 1) * hp)
        y_ref[0, :, gl] = _rms(y[:, gl], nw[:, gl]).astype(y_ref.dtype)

    @pl.when(c == pl.num_programs(1) - 1)
    def _fin():
        hfin_ref[0] = state[...]


def _ssd(zx, conv_buf, h0, conv_w, conv_b, dt_bias, a_log, d_skip, norm_w, *, t, valid, conv_done=False,
         stacked=None):
    b, l, in_dim = zx.shape
    n_heads = a_log.shape[0]
    d_inner = n_heads * SSM_HEAD_DIM
    conv_dim = conv_w.shape[1]
    n_grp, d_st = SSM_N_GROUPS, SSM_D_STATE
    hp = d_inner // n_grp
    assert l % t == 0 and in_dim == d_inner + conv_dim + n_heads
    cbuf = jnp.pad(conv_buf, ((0, 0), (SUBLANES - (SSM_CONV - 1), 0), (0, 0)))
    native = stacked is not None
    extra_in, extra_specs, aliases = [], [], {}
    if native:
        layer, out_buf = stacked
        state_shape = (n_grp, hp, d_st)
        h0k = h0.reshape(h0.shape[0], b, *state_shape)
        state_spec = pl.BlockSpec((None, 1) + state_shape, lambda i, c: (layer, i, 0, 0, 0))
        state_out = jax.ShapeDtypeStruct(h0k.shape, F32)
        if out_buf is not None:
            extra_in, extra_specs = [out_buf.reshape(h0k.shape)], [pl.BlockSpec(memory_space=pl.ANY)]
    else:
        state_shape = (n_grp, d_st, hp)
        h0k = h0.reshape(b, n_grp, n_heads // n_grp, SSM_HEAD_DIM, d_st)
        h0k = h0k.transpose(0, 1, 4, 2, 3).reshape(b, *state_shape)
        state_spec = pl.BlockSpec((1,) + state_shape, lambda i, c: (i, 0, 0, 0))
        state_out = jax.ShapeDtypeStruct(h0k.shape, F32)
    body = functools.partial(_ssd_kernel, t=t, valid=valid, d_inner=d_inner, n_heads=n_heads,
                             conv_done=conv_done, native=native)
    tri = jnp.tril(jnp.ones((t, t), BF16))
    consts = (tri, tri.T, _spread_matrix(n_heads, SSM_HEAD_DIM), _spread_matrix(n_heads, t))
    operands = [zx, cbuf, h0k, conv_w, conv_b.reshape(1, -1), dt_bias.reshape(1, -1), a_log.reshape(1, -1),
                jnp.repeat(d_skip, SSM_HEAD_DIM).reshape(1, -1), norm_w.reshape(1, -1), *consts]
    if extra_in:
        aliases = {len(operands): 1}
    y, hfin = pl.pallas_call(
        body,
        grid=(b, l // t),
        in_specs=[
            pl.BlockSpec((1, t, in_dim), lambda i, c: (i, c, 0)),
            pl.BlockSpec((1, SUBLANES, conv_dim), lambda i, c: (i, 0, 0)),
            state_spec,
            _resident((SSM_CONV, conv_dim)), _resident((1, conv_dim)),
            _resident((1, n_heads)), _resident((1, n_heads)),
            _resident((1, d_inner)), _resident((1, d_inner)),
        ] + [_resident(cst.shape) for cst in consts] + extra_specs,
        out_specs=[pl.BlockSpec((1, t, d_inner), lambda i, c: (i, c, 0)), state_spec],
        out_shape=[jax.ShapeDtypeStruct((b, l, d_inner), BF16), state_out],
        scratch_shapes=[
            pltpu.VMEM((SUBLANES, conv_dim), F32),
            pltpu.VMEM(state_shape, F32),
            pltpu.VMEM((t, d_inner), F32),
        ],
        input_output_aliases=aliases,
        compiler_params=_cparams("parallel", "arbitrary"),
        name="ssd_scan",
    )(*operands, *extra_in)
    if native:
        return y, hfin.reshape(h0.shape)
    hfin = hfin.reshape(b, n_grp, d_st, n_heads // n_grp, SSM_HEAD_DIM)
    hfin = hfin.transpose(0, 1, 3, 4, 2).reshape(b, n_heads, SSM_HEAD_DIM, d_st)
    return y, hfin


def _gla_kernel(p_ref, s0_ref, wg_ref, gb_ref, nw_ref, tri_ref, o_ref, sfin_ref, state,
                *, tb, ch, valid, dk, dv):
    c = pl.program_id(1)
    n_h = GLA_N_HEADS
    hk, hv = dk // n_h, dv // n_h

    @pl.when(c == 0)
    def _init():
        state[...] = s0_ref[0]

    q = p_ref[0, :, 0:dk] * (hk ** -0.5)
    k = p_ref[0, :, dk:2 * dk]
    v = p_ref[0, :, 2 * dk:2 * dk + dv]
    r = p_ref[0, :, 2 * dk + dv:2 * dk + 2 * dv]
    g_low = p_ref[0, :, 2 * dk + 2 * dv:]
    x = _dot(g_low.astype(BF16), wg_ref[...]) + gb_ref[...]
    log_a = -_softplus(-x) / GLA_GATE_NORM
    if valid < tb:
        keep = lax.broadcasted_iota(jnp.int32, (tb, dk), 0) < valid
        log_a = jnp.where(keep, log_a, 0.0)
        k = jnp.where(keep, k, 0.0)

    bcum = _sel_left(tri_ref[...], log_a)
    nch = tb // ch

    def chunks(a):
        return a.reshape(nch, ch, a.shape[-1])

    bc3 = chunks(bcum)
    b_last = bc3[:, ch - 1:ch, :]
    q_t = chunks(q * jnp.exp(bcum)).astype(BF16)
    k_t = chunks(k * jnp.exp(-bcum)).astype(BF16)
    k_dec = (chunks(k) * jnp.exp(b_last - bc3)).astype(BF16)
    e_last = jnp.exp(b_last)
    v_b = chunks(v).astype(BF16)
    causal = (lax.broadcasted_iota(jnp.int32, (nch, ch, ch), 1)
              >= lax.broadcasted_iota(jnp.int32, (nch, ch, ch), 2))

    gate = r * _sigmoid(r)
    nw = nw_ref[...]
    for h in range(n_h):
        ks = slice(h * hk, (h + 1) * hk)
        vs = slice(h * hv, (h + 1) * hv)
        qh, kh, vh = q_t[:, :, ks], k_t[:, :, ks], v_b[:, :, vs]
        upd = jnp.einsum("jkv,jkd->jvd", vh, k_dec[:, :, ks], preferred_element_type=F32)
        s_run = state[h]
        entering = []
        for j in range(nch):
            entering.append(s_run.astype(BF16))
            s_run = s_run * e_last[j, :, ks] + upd[j]
        state[h] = s_run
        att = jnp.einsum("jqd,jkd->jqk", qh, kh, preferred_element_type=F32)
        att = jnp.where(causal, att, 0.0).astype(BF16)
        o = (jnp.einsum("jqk,jkv->jqv", att, vh, preferred_element_type=F32)
             + jnp.einsum("jqd,jvd->jqv", qh, jnp.stack(entering), preferred_element_type=F32))
        o_ref[0, :, vs] = (_rms(o.reshape(tb, hv), nw) * gate[:, vs]).astype(o_ref.dtype)

    @pl.when(c == pl.num_programs(1) - 1)
    def _fin():
        sfin_ref[0] = state[...]


def _gla(proj, s0, w_gate, gate_bias, norm_w, *, tb, ch, valid):
    b, l, in_dim = proj.shape
    n_h = GLA_N_HEADS
    rank, dk = w_gate.shape
    hv = norm_w.shape[0]
    dv = hv * n_h
    hk = dk // n_h
    assert l % tb == 0 and tb % ch == 0 and in_dim == 2 * dk + 2 * dv + rank
    body = functools.partial(_gla_kernel, tb=tb, ch=ch, valid=valid, dk=dk, dv=dv)
    blocktri = jnp.kron(jnp.eye(tb // ch, dtype=F32), jnp.tril(jnp.ones((ch, ch), F32))).astype(BF16)
    o, sfin = pl.pallas_call(
        body,
        grid=(b, l // tb),
        in_specs=[
            pl.BlockSpec((1, tb, in_dim), lambda i, c: (i, c, 0)),
            pl.BlockSpec((1, n_h, hv, hk), lambda i, c: (i, 0, 0, 0)),
            _resident((rank, dk)), _resident((1, dk)), _resident((1, hv)), _resident((tb, tb)),
        ],
        out_specs=[
            pl.BlockSpec((1, tb, dv), lambda i, c: (i, c, 0)),
            pl.BlockSpec((1, n_h, hv, hk), lambda i, c: (i, 0, 0, 0)),
        ],
        out_shape=[
            jax.ShapeDtypeStruct((b, l, dv), BF16),
            jax.ShapeDtypeStruct((b, n_h, hv, hk), F32),
        ],
        scratch_shapes=[pltpu.VMEM((n_h, hv, hk), F32)],
        compiler_params=_cparams("parallel", "arbitrary"),
        name="gla_scan",
    )(proj, jnp.swapaxes(s0, -1, -2), w_gate.astype(BF16), gate_bias.reshape(1, -1),
      norm_w.reshape(1, -1), blocktri)
    return o, jnp.swapaxes(sfin, -1, -2)


def _attn_prompt_kernel(q_ref, kp_ref, kc_ref, vp_ref, vc_ref, o_ref, lse_ref, *, dilation, nb, span):
    not_first = pl.program_id(2) > 0
    blk = ATT_BLOCK
    dh = ATT_HEAD_DIM
    units = [(r, jb) for r in range(dilation) for jb in range(nb)]
    n_u = len(units)
    iu = lax.broadcasted_iota(jnp.int32, (n_u, blk, blk), 0)
    iq = lax.broadcasted_iota(jnp.int32, (n_u, blk, blk), 1)
    ik = lax.broadcasted_iota(jnp.int32, (n_u, blk, blk), 2)
    ok_prev = (ik >= iq + (blk - span)) & (((iu & (nb - 1)) != 0) | not_first)
    ok_cur = ik <= iq
    scale = dh ** -0.5

    def rows(r, jb):
        start = r + dilation * jb * blk
        return pl.ds(start, blk) if dilation == 1 else pl.ds(start, blk, stride=dilation)

    def gather(cur_ref, prev_ref):
        cur = jnp.stack([cur_ref[0, rows(r, jb), :] for r, jb in units])
        prev = jnp.stack([prev_ref[0, rows(r, nb - 1), :] if jb == 0 else cur_ref[0, rows(r, jb - 1), :]
                          for r, jb in units])
        return cur.astype(BF16), prev.astype(BF16)

    q = jnp.stack([q_ref[0, rows(r, jb), :] for r, jb in units]).astype(BF16)
    k_cur, k_prev = gather(kc_ref, kp_ref)
    v_cur, v_prev = gather(vc_ref, vp_ref)
    outs, lses = [], []
    for hh in range(LANES // dh):
        hs = slice(hh * dh, (hh + 1) * dh)
        qh = q[:, :, hs]
        s_p = jnp.einsum("uqd,ukd->uqk", qh, k_prev[:, :, hs], preferred_element_type=F32) * scale
        s_c = jnp.einsum("uqd,ukd->uqk", qh, k_cur[:, :, hs], preferred_element_type=F32) * scale
        s_p = jnp.where(ok_prev, s_p, -jnp.inf)
        s_c = jnp.where(ok_cur, s_c, -jnp.inf)
        mx = jnp.max(jnp.maximum(s_p, s_c), axis=-1, keepdims=True)
        p_p = jnp.exp(s_p - mx)
        p_c = jnp.exp(s_c - mx)
        den = jnp.sum(p_p + p_c, axis=-1, keepdims=True)
        outs.append(jnp.einsum("uqk,ukd->uqd", (p_p / den).astype(BF16), v_prev[:, :, hs],
                               preferred_element_type=F32)
                    + jnp.einsum("uqk,ukd->uqd", (p_c / den).astype(BF16), v_cur[:, :, hs],
                                 preferred_element_type=F32))
        lses.append(jnp.broadcast_to(mx + jnp.log(den), (n_u, blk, dh)))
    o = jnp.concatenate(outs, axis=-1)
    lse = jnp.concatenate(lses, axis=-1)
    for u, (r, jb) in enumerate(units):
        o_ref[0, rows(r, jb), :] = o[u]
        lse_ref[0, rows(r, jb), :] = lse[u]


def _attn_prompt(qkv, gi, *, b, l, window, dilation):
    n_slab = qkv.shape[0] // 3
    spg = ATT_HEADS_PER_GROUP * ATT_HEAD_DIM // LANES
    span = window // dilation
    blk = ATT_BLOCK
    nb = max(1, min(l, ATT_SUPER_ROWS) // (dilation * blk))
    r_rows = dilation * nb * blk
    n_sup = l // r_rows
    assert l % r_rows == 0 and span <= blk and qkv.shape[1] == b * l and nb & (nb - 1) == 0

    def spec(part, prev):
        def index(bi, sp, i):
            i = jnp.maximum(i - 1, 0) if prev else i
            return (part * n_slab + gi * spg + sp, bi * n_sup + i, 0)
        return pl.BlockSpec((1, r_rows, LANES), index)

    out_spec = pl.BlockSpec((1, r_rows, LANES), lambda bi, sp, i: (sp, bi * n_sup + i, 0))
    return pl.pallas_call(
        functools.partial(_attn_prompt_kernel, dilation=dilation, nb=nb, span=span),
        grid=(b, spg, n_sup),
        in_specs=[spec(0, False), spec(1, True), spec(1, False), spec(2, True), spec(2, False)],
        out_specs=[out_spec, out_spec],
        out_shape=[jax.ShapeDtypeStruct((spg, b * l, LANES), F32)] * 2,
        compiler_params=_cparams("parallel", "parallel", "parallel"),
        name="attn_prompt",
    )(qkv, qkv, qkv, qkv, qkv)


def _attn_sample_kernel(q_ref, kn_ref, vn_ref, cache_ref, o_ref, lse_ref, cout_ref,
                        *, window, dilation, n_new, seqs):
    dh = ATT_HEAD_DIM
    n_h = ATT_HEADS_PER_GROUP
    rows = seqs * n_new
    scale = dh ** -0.5
    units = [(bb, h) for bb in range(seqs) for h in range(n_h)]
    n_u = len(units)

    def heads_of(ref):
        return jnp.stack([ref[h * dh // LANES][:, h * dh % LANES:h * dh % LANES + dh] for _, h in units])

    q = heads_of(q_ref).astype(BF16)
    k_new, v_new = heads_of(kn_ref), heads_of(vn_ref)
    k_t = cache_ref[:, 0].reshape(n_u, dh, window)
    v_t = cache_ref[:, 1].reshape(n_u, dh, window)

    row = lax.broadcasted_iota(jnp.int32, (n_u, rows, window), 1)
    col = lax.broadcasted_iota(jnp.int32, (n_u, rows, window), 2)
    sq = row & (n_new - 1)
    ok_cache = (col >= sq) & (((col - sq) & (dilation - 1)) == 0)
    row_n = lax.broadcasted_iota(jnp.int32, (n_u, rows, rows), 1)
    col_n = lax.broadcasted_iota(jnp.int32, (n_u, rows, rows), 2)
    unit = lax.broadcasted_iota(jnp.int32, (n_u, rows, rows), 0)
    first = sum(jnp.where(unit >= bb * n_h, n_new, 0) for bb in range(1, seqs))
    back = (row_n & (n_new - 1)) - (col_n & (n_new - 1))
    ok_new = (back >= 0) & ((back & (dilation - 1)) == 0) & (col_n >= first) & (col_n < first + n_new)

    s_c = jnp.einsum("urd,udw->urw", q, k_t.astype(BF16), preferred_element_type=F32) * scale
    s_n = jnp.einsum("urd,utd->urt", q, k_new.astype(BF16), preferred_element_type=F32) * scale
    s_c = jnp.where(ok_cache, s_c, -jnp.inf)
    s_n = jnp.where(ok_new, s_n, -jnp.inf)
    mx = jnp.maximum(jnp.max(s_c, axis=-1, keepdims=True), jnp.max(s_n, axis=-1, keepdims=True))
    p_c = jnp.exp(s_c - mx)
    p_n = jnp.exp(s_n - mx)
    den = jnp.sum(p_c, axis=-1, keepdims=True) + jnp.sum(p_n, axis=-1, keepdims=True)
    o = (jnp.einsum("urw,udw->urd", (p_c / den).astype(BF16), v_t.astype(BF16), preferred_element_type=F32)
         + jnp.einsum("urt,utd->urd", (p_n / den).astype(BF16), v_new.astype(BF16),
                      preferred_element_type=F32))
    lse = jnp.broadcast_to(mx + jnp.log(den), (n_u, rows, dh))

    lane_t = lax.broadcasted_iota(jnp.int32, (2 * rows, LANES), 1)
    tok_t = lax.broadcasted_iota(jnp.int32, (2 * rows, LANES), 0)
    lane_o = lax.broadcasted_iota(jnp.int32, (dh, LANES), 1)
    zpad = jnp.zeros((rows, dh), F32)
    for u, (bb, h) in enumerate(units):
        r0 = bb * n_new
        sl = h * dh // LANES
        hs = slice(h * dh % LANES, h * dh % LANES + dh)
        o_ref[sl, r0:r0 + n_new, hs] = o[u, r0:r0 + n_new]
        lse_ref[sl, r0:r0 + n_new, hs] = lse[u, r0:r0 + n_new]
        place = jnp.where(lane_t - (LANES - n_new) == tok_t - r0, 1.0, 0.0).astype(BF16)
        for kv, old, new in ((0, k_t[u], k_new[u]), (1, v_t[u], v_new[u])):
            tail = sum(lax.dot_general(part, place, _TN, preferred_element_type=F32)
                       for part in _split3(jnp.concatenate([new, zpad], axis=0)))
            shifted = pltpu.roll(old, window - n_new, 1)
            last = jnp.where(lane_o >= LANES - n_new, tail, shifted[:, window - LANES:])
            if window > LANES:
                cout_ref[bb, kv, h, :, :window - LANES] = shifted[:, :window - LANES]
            cout_ref[bb, kv, h, :, window - LANES:] = last


def _attn_sample(qkv, cache_t, gi, *, window, dilation, n_new):
    n_slab = qkv.shape[0] // 3
    spg = ATT_HEADS_PER_GROUP * ATT_HEAD_DIM // LANES
    b = cache_t.shape[0]
    seqs = SUBLANES // n_new
    rows = seqs * n_new
    assert rows == SUBLANES and b % seqs == 0 and qkv.shape[1] == b * n_new
    assert cache_t.shape[-1] == window and window % dilation == 0 and window % LANES == 0
    assert dilation & (dilation - 1) == 0 and n_new & (n_new - 1) == 0

    def slabs(part):
        return pl.BlockSpec((spg, rows, LANES), lambda i: (part * n_slab // spg + gi, i, 0))

    cache_spec = pl.BlockSpec((seqs,) + cache_t.shape[1:], lambda i: (i, 0, 0, 0, 0))
    out_spec = pl.BlockSpec((spg, rows, LANES), lambda i: (0, i, 0))
    return pl.pallas_call(
        functools.partial(_attn_sample_kernel, window=window, dilation=dilation, n_new=n_new, seqs=seqs),
        grid=(b // seqs,),
        in_specs=[slabs(0), slabs(1), slabs(2), cache_spec],
        out_specs=[out_spec, out_spec, cache_spec],
        out_shape=[jax.ShapeDtypeStruct((spg, b * n_new, LANES), F32)] * 2
        + [jax.ShapeDtypeStruct(cache_t.shape, F32)],
        compiler_params=_cparams("parallel"),
        name="attn_sample",
    )(qkv, qkv, qkv, cache_t)


def _rope_tables(pos):
    half = ATT_ROT_DIM // 2
    lane = lax.broadcasted_iota(jnp.int32, (pos.shape[0], LANES), 1) % ATT_HEAD_DIM
    x1, rot = lane < half, lane < ATT_ROT_DIM
    freq = jnp.where(x1, lane, lane - half)
    inv_freq = ROPE_THETA ** (-(2 * freq).astype(F32) / ATT_ROT_DIM)
    ang = pos.astype(F32)[:, None] * inv_freq
    cos, sin = jnp.cos(ang), jnp.sin(ang)
    c = jnp.where(rot, cos, 1.0)
    s1 = jnp.where(rot & jnp.logical_not(x1), sin, 0.0)
    s2 = jnp.where(x1, -sin, 0.0)
    return c, s1, s2


def _pad_rows(t, rows):
    return jnp.pad(t, ((0, 0), (0, rows - t.shape[1]), (0, 0)))


def kernel(x_prompt, x_sample, state_ssm, state_ssm_conv, state_gla, cache_kv_g0, cache_kv_g1, cache_kv_g2,
           norm_mix, norm_ffn, ffn_gate, ffn_up, ffn_down,
           ssm_w_in, ssm_conv_w, ssm_conv_b, ssm_dt_bias, ssm_a_log, ssm_d, ssm_norm, ssm_w_out,
           gla_w_in, gla_w_gate, gla_gate_bias, gla_norm, gla_w_out,
           att_w_qkv, att_w_out, norm_final):
    bp, lp, d = x_prompt.shape
    bs, ls, _ = x_sample.shape
    depth = norm_mix.shape[0]
    mp, ms = bp * lp, bs * ls
    tm_p = 512 if mp % 512 == 0 else mp
    tm_s = ms
    xp = x_prompt.reshape(mp, d)
    xs = x_sample.reshape(ms, d)
    att_caches = (cache_kv_g0, cache_kv_g1, cache_kv_g2)
    hg_w = ATT_HEADS_PER_GROUP * ATT_HEAD_DIM
    n_att = hg_w * len(ATT_GROUPS)
    rope_p = _rope_tables(jnp.arange(lp))
    rope_s = _rope_tables(jnp.tile(PAST_LEN + jnp.arange(ls), bs))
    ffn_w = tuple(w.astype(BF16) for w in (ffn_gate, ffn_up, ffn_down))
    ssm_w_in, ssm_w_out, gla_w_in, gla_w_out, att_w_qkv, att_w_out = (
        w.astype(BF16) for w in (ssm_w_in, ssm_w_out, gla_w_in, gla_w_out, att_w_qkv, att_w_out))

    ssm_p, conv_p, conv_s, gla_p, gla_s = [], [], [], [], []
    ssm_s = None
    kv_p, kv_s = [[], [], []], [[], [], []]
    for i in range(depth):
        m, j = i % N_MIXERS, i // N_MIXERS
        if m == 0:
            w_in = (ssm_w_in, j)
            conv_dim = ssm_conv_w.shape[2]
            d_inner = ssm_w_out.shape[1]
            wts = (ssm_conv_w[j], ssm_conv_b[j], ssm_dt_bias[j], ssm_a_log[j], ssm_d[j], ssm_norm[j])
            conv0 = jnp.zeros((bp, SSM_CONV - 1, conv_dim), F32)
            zx_p, tails = _ssd_in_proj(xp, norm_mix[i], w_in, conv0, ssm_conv_w[j], ssm_conv_b[j],
                                       tm=tm_p, seq_len=lp, d_inner=d_inner)
            zx_p = zx_p.reshape(bp, lp, -1)
            zx_s = _norm_matmul(xs, norm_mix[i], w_in, tm=tm_s).reshape(bs, ls, -1)
            t_p = min(SSM_CHUNK, lp)
            y_p, h_p = _ssd(zx_p, conv0, jnp.zeros((bp,) + state_ssm.shape[2:], F32), *wts,
                            t=t_p, valid=t_p, conv_done=True)
            y_s, ssm_s = _ssd(_pad_rows(zx_s, SAMPLE_PAD), state_ssm_conv[j], state_ssm, *wts,
                              t=SAMPLE_PAD, valid=ls, stacked=(j, ssm_s))
            y_s = y_s[:, :ls]
            xbc_s = zx_s[:, :, d_inner:d_inner + conv_dim]
            conv_p.append(tails.reshape(bp, lp // tm_p, SUBLANES, conv_dim)[:, -1, -(SSM_CONV - 1):])
            conv_s.append(jnp.concatenate([state_ssm_conv[j], xbc_s], axis=1)[:, -(SSM_CONV - 1):])
            ssm_p.append(h_p)
            w_out = (ssm_w_out, j)
            mix_p, mix_s = y_p.reshape(mp, -1), y_s.reshape(ms, -1)
        elif m == 1:
            w_in = (gla_w_in, j)
            wts = (gla_w_gate[j], gla_gate_bias[j], gla_norm[j])
            pr_p = _norm_matmul(xp, norm_mix[i], w_in, tm=tm_p).reshape(bp, lp, -1)
            pr_s = _norm_matmul(xs, norm_mix[i], w_in, tm=tm_s).reshape(bs, ls, -1)
            ch_p = min(GLA_CHUNK, lp)
            tb_p = 256 if lp % 256 == 0 else ch_p
            o_p, s_p = _gla(pr_p, jnp.zeros((bp,) + state_gla.shape[2:], F32), *wts,
                            tb=tb_p, ch=ch_p, valid=tb_p)
            o_s, s_s = _gla(_pad_rows(pr_s, SAMPLE_PAD), state_gla[j], *wts,
                            tb=SAMPLE_PAD, ch=SAMPLE_PAD, valid=ls)
            o_s = o_s[:, :ls]
            gla_p.append(s_p)
            gla_s.append(s_s)
            w_out = (gla_w_out, j)
            mix_p, mix_s = o_p.reshape(mp, -1), o_s.reshape(ms, -1)
        else:
            w_qkv = (att_w_qkv, j)
            qkv_p = _norm_matmul(xp, norm_mix[i], w_qkv, tm=tm_p, rope=(2 * n_att, rope_p))
            qkv_s = _norm_matmul(xs, norm_mix[i], w_qkv, tm=tm_s, rope=(2 * n_att, rope_s))
            n_slab = n_att // LANES
            spg = hg_w // LANES
            qkv_p4 = qkv_p.reshape(3 * n_slab, bp, lp, LANES)
            os_p, ls_p, os_s, ls_s = [], [], [], []
            for gi, (window, dilation) in enumerate(ATT_GROUPS):
                o, lse = _attn_prompt(qkv_p, gi, b=bp, l=lp, window=window, dilation=dilation)
                os_p.append(o)
                ls_p.append(lse)
                keep = min(window, lp)
                kv = jnp.stack([qkv_p4[part * n_slab + gi * spg:part * n_slab + (gi + 1) * spg, :, lp - keep:]
                                for part in (1, 2)])
                kv = kv.reshape(2, spg, bp, keep, LANES // ATT_HEAD_DIM, ATT_HEAD_DIM)
                kv_p[gi].append(kv.transpose(2, 3, 0, 1, 4, 5)
                                .reshape(bp, keep, 2, ATT_HEADS_PER_GROUP, ATT_HEAD_DIM))
                cache_t = att_caches[gi][j].transpose(0, 2, 3, 4, 1)
                o, lse, cache_new = _attn_sample(qkv_s, cache_t, gi, window=window, dilation=dilation, n_new=ls)
                os_s.append(o)
                ls_s.append(lse)
                kv_s[gi].append(cache_new.transpose(0, 4, 1, 2, 3))
            w_out = (att_w_out, j)
            mix_p, mix_s = (os_p, ls_p), (os_s, ls_s)
        final_g = norm_final if i == depth - 1 else None
        xp = _block_tail(mix_p, w_out, xp, norm_ffn[i], *ffn_w, i, tm=tm_p, final_g=final_g)
        xs = _block_tail(mix_s, w_out, xs, norm_ffn[i], *ffn_w, i, tm=tm_s, final_g=final_g)
    y_prompt = xp.reshape(bp, lp, d)
    y_sample = xs.reshape(bs, ls, d)
    return (y_prompt, y_sample,
            jnp.stack(ssm_p), ssm_s, jnp.stack(conv_p), jnp.stack(conv_s),
            jnp.stack(gla_p), jnp.stack(gla_s),
            jnp.stack(kv_p[0]), jnp.stack(kv_s[0]), jnp.stack(kv_p[1]), jnp.stack(kv_s[1]),
            jnp.stack(kv_p[2]), jnp.stack(kv_s[2]))
```

```python
import functools

import jax
import jax.numpy as jnp
from jax import lax
from jax.experimental import pallas as pl
from jax.experimental.pallas import tpu as pltpu

F32 = jnp.float32
BF16 = jnp.bfloat16
LOG2_E = 1.4426950408889634

NORM_EPS = 1e-6
N_MIXERS = 3

SSM_HEAD_DIM = 64
SSM_N_GROUPS = 4
SSM_D_STATE = 128
SSM_CONV = 4
SSM_CHUNK = 128
SSM_IN_PROJ_CHUNKS = 4

GLA_N_HEADS = 4
GLA_GATE_NORM = 16.0
GLA_CHUNK = 32

ATT_GROUPS = ((128, 1), (512, 4), (2048, 16))
ATT_HEADS_PER_GROUP = 4
ATT_HEAD_DIM = 64
ATT_ROT_DIM = ATT_HEAD_DIM // 4
ROPE_THETA = 500000.0
ATT_BLOCK = 128
ATT_SUPER_ROWS = 2048
PAST_LEN = 8192

LANES = 128
SUBLANES = 8
SAMPLE_PAD = 16
VMEM_LIMIT = 56 * 1024 * 1024

_NT = (((1,), (1,)), ((), ()))
_TN = (((0,), (0,)), ((), ()))


def _cparams(*sem):
    return pltpu.CompilerParams(dimension_semantics=sem, vmem_limit_bytes=VMEM_LIMIT)


def _resident(shape):
    zeros = (0,) * len(shape)
    return pl.BlockSpec(shape, lambda *_: zeros, pipeline_mode=pl.Buffered(1))


def _layer(wl):
    w, layer = wl
    return pl.BlockSpec((None,) + w.shape[1:], lambda *_: (layer, 0, 0), pipeline_mode=pl.Buffered(1))


def _rms(x, g):
    inv = lax.rsqrt(jnp.mean(x * x, axis=-1, keepdims=True) + NORM_EPS)
    return x * inv * g


def _sigmoid(x):
    return 1.0 / (1.0 + jnp.exp(-x))


def _softplus(x):
    return jnp.maximum(x, 0.0) + jnp.log1p(jnp.exp(-jnp.abs(x)))


def _split3(v):
    hi = v.astype(BF16)
    r = v - hi.astype(F32)
    mid = r.astype(BF16)
    lo = (r - mid.astype(F32)).astype(BF16)
    return hi, mid, lo


def _dot(a, b):
    return jnp.dot(a, b, preferred_element_type=F32)


def _spread_matrix(n_blocks, width):
    k_dim = -(-3 * n_blocks // LANES) * LANES
    row = lax.broadcasted_iota(jnp.int32, (k_dim, n_blocks * width), 0)
    col = lax.broadcasted_iota(jnp.int32, (k_dim, n_blocks * width), 1)
    blk = row - jnp.where(row >= 2 * n_blocks, 2 * n_blocks, jnp.where(row >= n_blocks, n_blocks, 0))
    sel = (row < 3 * n_blocks) & (col >= blk * width) & (col < blk * width + width)
    return jnp.where(sel, 1.0, 0.0).astype(BF16)


def _spread(v, sel):
    rows, n_blocks = v.shape
    terms = [p.astype(F32) for p in _split3(v)]
    pad = jnp.zeros((rows, sel.shape[0] - 3 * n_blocks), F32)
    return _dot(jnp.concatenate(terms + [pad], axis=1).astype(BF16), sel)


def _sel_left(e, v):
    hi, mid, lo = _split3(v)
    return _dot(e, hi) + _dot(e, mid) + _dot(e, lo)


def _norm_matmul_kernel(x_ref, g_ref, w_ref, o_ref):
    h = _rms(x_ref[...], g_ref[...]).astype(BF16)
    o_ref[...] = _dot(h, w_ref[...])


def _norm_matmul_rope_kernel(x_ref, g_ref, w_ref, c_ref, s1_ref, s2_ref, o_ref, *, n_rot):
    h = _rms(x_ref[...], g_ref[...]).astype(BF16)
    res = _dot(h, w_ref[...])
    c, s1, s2 = c_ref[...], s1_ref[...], s2_ref[...]
    half = ATT_ROT_DIM // 2
    for j in range(o_ref.shape[0]):
        x = res[:, j * LANES:(j + 1) * LANES]
        if j < n_rot // LANES:
            x = x * c + pltpu.roll(x, half, 1) * s1 + pltpu.roll(x, LANES - half, 1) * s2
        o_ref[j] = x


def _norm_matmul(x, g, wl, *, tm, rope=None):
    m, d = x.shape
    n = wl[0].shape[2]
    assert m % tm == 0
    in_specs = [pl.BlockSpec((tm, d), lambda i: (i, 0)), _resident((1, d)), _layer(wl)]
    args = [x, g.reshape(1, d), wl[0]]
    if rope is None:
        body = _norm_matmul_kernel
        out_spec = pl.BlockSpec((tm, n), lambda i: (i, 0))
        out_shape = jax.ShapeDtypeStruct((m, n), F32)
    else:
        n_rot, tables = rope
        assert n % LANES == 0 and n_rot % LANES == 0
        body = functools.partial(_norm_matmul_rope_kernel, n_rot=n_rot)
        period = tables[0].shape[0] // tm
        assert tables[0].shape[0] % tm == 0 and (m // tm) % period == 0
        in_specs += [pl.BlockSpec((tm, LANES), lambda i: (i % period, 0))] * 3
        args += list(tables)
        out_spec = pl.BlockSpec((n // LANES, tm, LANES), lambda i: (0, i, 0))
        out_shape = jax.ShapeDtypeStruct((n // LANES, m, LANES), F32)
    return pl.pallas_call(
        body,
        grid=(m // tm,),
        in_specs=in_specs,
        out_specs=out_spec,
        out_shape=out_shape,
        compiler_params=_cparams("parallel"),
        name="norm_proj" if rope is None else "norm_proj_rope",
    )(*args)


def _conv_silu(raw, before, cw, cb):
    t, c = raw.shape
    groups = [before] + [raw[r:r + SUBLANES] for r in range(0, t, SUBLANES)]
    row8 = lax.broadcasted_iota(jnp.int32, (SUBLANES, c), 0)
    acc = cb
    for k in range(SSM_CONV - 1, 0, -1):
        rolled = [pltpu.roll(grp, k, 0) for grp in groups]
        shifted = jnp.concatenate([jnp.where(row8 < k, prev, here)
                                   for prev, here in zip(rolled[:-1], rolled[1:])], axis=0)
        acc = acc + shifted * cw[SSM_CONV - 1 - k:SSM_CONV - k, :]
    acc = acc + raw * cw[SSM_CONV - 1:SSM_CONV, :]
    return acc * _sigmoid(acc), groups[-1]


def _ssd_in_proj_kernel(x_ref, g_ref, w_ref, cbuf_ref, cw_ref, cb_ref, o_ref, tail_ref, tail,
                        *, d_inner, conv_dim, tiles_per_seq, col_chunk):
    @pl.when(pl.program_id(0) % tiles_per_seq == 0)
    def _start_of_sequence():
        tail[...] = cbuf_ref[0]

    h = _rms(x_ref[...], g_ref[...]).astype(BF16)
    n_chunks = conv_dim // col_chunk
    z_chunk = d_inner // n_chunks
    for j in range(n_chunks):
        c0 = j * col_chunk
        cs = slice(c0, c0 + col_chunk)
        raw = _dot(h, w_ref[:, d_inner + c0:d_inner + c0 + col_chunk])
        zs = slice(j * z_chunk, (j + 1) * z_chunk)
        o_ref[:, zs] = _dot(h, w_ref[:, zs])
        act, last = _conv_silu(raw, tail[:, cs], cw_ref[:, cs], cb_ref[:, cs])
        o_ref[:, d_inner + c0:d_inner + c0 + col_chunk] = act
        tail[:, cs] = last
        tail_ref[0, :, cs] = last
    o_ref[:, d_inner + conv_dim:] = _dot(h, w_ref[:, d_inner + conv_dim:])


def _ssd_in_proj(x, g, wl, conv_buf, conv_w, conv_b, *, tm, seq_len, d_inner):
    m, d = x.shape
    n = wl[0].shape[2]
    conv_dim = conv_w.shape[1]
    assert m % tm == 0 and seq_len % tm == 0
    cbuf = jnp.pad(conv_buf, ((0, 0), (SUBLANES - (SSM_CONV - 1), 0), (0, 0)))
    tiles_per_seq = seq_len // tm
    body = functools.partial(_ssd_in_proj_kernel, d_inner=d_inner, conv_dim=conv_dim,
                             tiles_per_seq=tiles_per_seq, col_chunk=conv_dim // SSM_IN_PROJ_CHUNKS)
    assert conv_dim % (SSM_IN_PROJ_CHUNKS * LANES) == 0 and d_inner % (SSM_IN_PROJ_CHUNKS * LANES) == 0
    return pl.pallas_call(
        body,
        grid=(m // tm,),
        in_specs=[pl.BlockSpec((tm, d), lambda i: (i, 0)), _resident((1, d)), _layer(wl),
                  pl.BlockSpec((1, SUBLANES, conv_dim), lambda i: (i // tiles_per_seq, 0, 0)),
                  _resident((SSM_CONV, conv_dim)), _resident((1, conv_dim))],
        out_specs=[pl.BlockSpec((tm, n), lambda i: (i, 0)),
                   pl.BlockSpec((1, SUBLANES, conv_dim), lambda i: (i, 0, 0))],
        out_shape=[jax.ShapeDtypeStruct((m, n), F32),
                   jax.ShapeDtypeStruct((m // tm, SUBLANES, conv_dim), F32)],
        scratch_shapes=[pltpu.VMEM((SUBLANES, conv_dim), F32)],
        compiler_params=_cparams("arbitrary"),
        name="ssd_in_proj",
    )(x, g.reshape(1, d), wl[0], cbuf, conv_w, conv_b.reshape(1, -1))


def _mixed_heads(o_refs, l_refs):
    spg = o_refs[0].shape[0]
    pieces = [[None] * spg for _ in o_refs]
    for s in range(spg):
        ls = [l_ref[s] for l_ref in l_refs]
        mx = functools.reduce(jnp.maximum, ls)
        es = [jnp.exp(l - mx) for l in ls]
        den = functools.reduce(lambda a, b: a + b, es)
        for grp, (o_ref, e) in enumerate(zip(o_refs, es)):
            pieces[grp][s] = o_ref[s] * (e / den)
    return jnp.concatenate([p for grp in pieces for p in grp], axis=-1).astype(BF16)


def _block_tail_kernel(*refs, n_groups, final_norm):
    n_mix = 2 * n_groups if n_groups else 1
    wo_ref, x_ref, g_ref, wg_ref, wu_ref, wd_ref, gf_ref, o_ref = refs[n_mix:]
    y = _mixed_heads(refs[:n_groups], refs[n_groups:n_mix]) if n_groups else refs[0][...]
    x = x_ref[...] + _dot(y, wo_ref[...])
    h = _rms(x, g_ref[...]).astype(BF16)
    gate = _dot(h, wg_ref[...])
    up = _dot(h, wu_ref[...])
    act = (gate * _sigmoid(gate) * up).astype(BF16)
    x = x + _dot(act, wd_ref[...])
    o_ref[...] = _rms(x, gf_ref[...]) if final_norm else x


def _block_tail(mix, wol, x, g, wg, wu, wd, layer, *, tm, final_g=None):
    m, d = x.shape
    assert m % tm == 0
    gf = g if final_g is None else final_g
    if isinstance(mix, tuple):
        os_, lses = mix
        n_groups = len(os_)
        mix_args = [*os_, *lses]
        mix_specs = [pl.BlockSpec((os_[0].shape[0], tm, LANES), lambda i: (0, i, 0))] * (2 * n_groups)
    else:
        n_groups = 0
        mix_args = [mix]
        mix_specs = [pl.BlockSpec((tm, mix.shape[1]), lambda i: (i, 0))]
    return pl.pallas_call(
        functools.partial(_block_tail_kernel, n_groups=n_groups, final_norm=final_g is not None),
        grid=(m // tm,),
        in_specs=mix_specs + [_layer(wol), pl.BlockSpec((tm, d), lambda i: (i, 0)), _resident((1, d)),
                              _layer((wg, layer)), _layer((wu, layer)), _layer((wd, layer)),
                              _resident((1, d))],
        out_specs=pl.BlockSpec((tm, d), lambda i: (i, 0)),
        out_shape=jax.ShapeDtypeStruct((m, d), F32),
        compiler_params=_cparams("parallel"),
        name="proj_swiglu",
    )(*mix_args, wol[0], x, g.reshape(1, d), wg, wu, wd, gf.reshape(1, d))


def _ssd_kernel(zx_ref, cbuf_ref, h0_ref, cw_ref, cb_ref, dtb_ref, alog_ref, dsk_ref, nw_ref,
                tri_ref, triu_ref, sel_hp_ref, *rest,
                t, valid, d_inner, n_heads, conv_done, native):
    y_ref, hfin_ref, tail, state, yacc = rest[-5:]
    c = pl.program_id(1)
    n_grp = SSM_N_GROUPS
    d_st = SSM_D_STATE
    hp = d_inner // n_grp
    hpg = n_heads // n_grp
    pdim = SSM_HEAD_DIM
    bc_w = n_grp * d_st
    conv_dim = d_inner + 2 * bc_w

    @pl.when(c == 0)
    def _init():
        state[...] = h0_ref[0]
        tail[...] = cbuf_ref[0]

    xbc = zx_ref[0, :, d_inner:d_inner + conv_dim]
    if not conv_done:
        xbc, tail[...] = _conv_silu(xbc, tail[...], cw_ref[...], cb_ref[...])
    xs = xbc[:, :d_inner]
    bm = xbc[:, d_inner:d_inner + bc_w]
    cm = xbc[:, d_inner + bc_w:]

    dt = _softplus(zx_ref[0, :, d_inner + conv_dim:] + dtb_ref[...])
    if valid < t:
        dt = jnp.where(lax.broadcasted_iota(jnp.int32, dt.shape, 0) < valid, dt, 0.0)
    a = dt * (-jnp.exp(alog_ref[...]))

    iq = lax.broadcasted_iota(jnp.int32, (t, t), 0)
    ik = lax.broadcasted_iota(jnp.int32, (t, t), 1)
    low = iq >= ik
    a_cs = _sel_left(tri_ref[...], a)
    a_cs_t = sum(lax.dot_general(part, triu_ref[...], _TN, preferred_element_type=F32)
                 for part in _split3(a))

    a_last = a_cs[t - 1:t, :]
    pad_rows = 2 * SUBLANES
    stack = jnp.concatenate([dt, dt * jnp.exp(a_last - a_cs), jnp.exp(a_cs),
                             jnp.broadcast_to(jnp.exp(a_last), (pad_rows, n_heads))], axis=0)
    ex = _spread(stack, sel_hp_ref[...])
    xdt_b = (xs * ex[0:t]).astype(BF16)
    xdst_b = (xs * ex[t:2 * t]).astype(BF16)
    from_start = ex[2 * t:3 * t]
    chunk_decay = ex[3 * t:3 * t + pad_rows]
    if native:
        only0 = jnp.where(lax.broadcasted_iota(jnp.int32, chunk_decay.shape, 0) == 0, chunk_decay, 0.0)
        ones = jnp.ones((pad_rows, d_st), BF16)
        decay_rows = sum(lax.dot_general(part, ones, _TN, preferred_element_type=F32)
                         for part in _split3(only0))

    a_cs2 = a_cs * LOG2_E
    a_row = a_cs_t * LOG2_E

    pair = LANES // pdim
    for g in range(n_grp):
        bg = bm[:, g * d_st:(g + 1) * d_st].astype(BF16)
        cg = cm[:, g * d_st:(g + 1) * d_st].astype(BF16)
        cb = lax.dot_general(cg, bg, _NT, preferred_element_type=F32)
        sg = state[g]
        sg_b = sg.astype(BF16)
        gl = slice(g * hp, (g + 1) * hp)
        carried = lax.dot_general(cg, sg_b, _NT, preferred_element_type=F32) if native else _dot(cg, sg_b)
        yacc[:, gl] = carried * from_start[:, gl]
        for h0 in range(0, hpg, pair):
            outs = []
            for hh in range(g * hpg + h0, g * hpg + h0 + pair):
                ps = slice(hh * pdim, (hh + 1) * pdim)
                decay = jnp.exp2(jnp.where(low, a_cs2[:, hh:hh + 1] - a_row[hh:hh + 1, :], -jnp.inf))
                outs.append(_dot((cb * decay).astype(BF16), xdt_b[:, ps]))
            lo = (g * hpg + h0) * pdim
            yacc[:, lo:lo + pair * pdim] += jnp.concatenate(outs, axis=1)
        if native:
            upd = lax.dot_general(xdst_b[:, gl], bg, _TN, preferred_element_type=F32)
            state[g] = sg * decay_rows[gl, :] + upd
        else:
            upd = lax.dot_general(bg, xdst_b[:, gl], _TN, preferred_element_type=F32)
            state[g] = sg * chunk_decay[0:1, gl] + upd

    z = zx_ref[0, :, :d_inner]
    y = (yacc[...] + xs * dsk_ref[...]) * (z * _sigmoid(z))
    nw = nw_ref[...]
    for g in range(n_grp):
        gl = slice(g * hp, (g + 1) * hp)
        y_ref[0, :, gl] = _rms(y[:, gl], nw[:, gl]).astype(y_ref.dtype)

    @pl.when(c == pl.num_programs(1) - 1)
    def _fin():
        hfin_ref[0] = state[...]


def _ssd(zx, conv_buf, h0, conv_w, conv_b, dt_bias, a_log, d_skip, norm_w, *, t, valid, conv_done=False,
         stacked=None):
    b, l, in_dim = zx.shape
    n_heads = a_log.shape[0]
    d_inner = n_heads * SSM_HEAD_DIM
    conv_dim = conv_w.shape[1]
    n_grp, d_st = SSM_N_GROUPS, SSM_D_STATE
    hp = d_inner // n_grp
    assert l % t == 0 and in_dim == d_inner + conv_dim + n_heads
    cbuf = jnp.pad(conv_buf, ((0, 0), (SUBLANES - (SSM_CONV - 1), 0), (0, 0)))
    native = stacked is not None
    extra_in, extra_specs, aliases = [], [], {}
    if native:
        layer, out_buf = stacked
        state_shape = (n_grp, hp, d_st)
        h0k = h0.reshape(h0.shape[0], b, *state_shape)
        state_spec = pl.BlockSpec((None, 1) + state_shape, lambda i, c: (layer, i, 0, 0, 0))
        state_out = jax.ShapeDtypeStruct(h0k.shape, F32)
        extra_in, extra_specs = [out_buf.reshape(h0k.shape)], [pl.BlockSpec(memory_space=pl.ANY)]
    else:
        state_shape = (n_grp, d_st, hp)
        h0k = h0.reshape(b, n_grp, n_heads // n_grp, SSM_HEAD_DIM, d_st)
        h0k = h0k.transpose(0, 1, 4, 2, 3).reshape(b, *state_shape)
        state_spec = pl.BlockSpec((1,) + state_shape, lambda i, c: (i, 0, 0, 0))
        state_out = jax.ShapeDtypeStruct(h0k.shape, F32)
    body = functools.partial(_ssd_kernel, t=t, valid=valid, d_inner=d_inner, n_heads=n_heads,
                             conv_done=conv_done, native=native)
    tri = jnp.tril(jnp.ones((t, t), BF16))
    consts = (tri, tri.T, _spread_matrix(n_heads, SSM_HEAD_DIM))
    operands = [zx, cbuf, h0k, conv_w, conv_b.reshape(1, -1), dt_bias.reshape(1, -1), a_log.reshape(1, -1),
                jnp.repeat(d_skip, SSM_HEAD_DIM).reshape(1, -1), norm_w.reshape(1, -1), *consts]
    if extra_in:
        aliases = {len(operands): 1}
    y, hfin = pl.pallas_call(
        body,
        grid=(b, l // t),
        in_specs=[
            pl.BlockSpec((1, t, in_dim), lambda i, c: (i, c, 0)),
            pl.BlockSpec((1, SUBLANES, conv_dim), lambda i, c: (i, 0, 0)),
            state_spec,
            _resident((SSM_CONV, conv_dim)), _resident((1, conv_dim)),
            _resident((1, n_heads)), _resident((1, n_heads)),
            _resident((1, d_inner)), _resident((1, d_inner)),
        ] + [_resident(cst.shape) for cst in consts] + extra_specs,
        out_specs=[pl.BlockSpec((1, t, d_inner), lambda i, c: (i, c, 0)), state_spec],
        out_shape=[jax.ShapeDtypeStruct((b, l, d_inner), BF16), state_out],
        scratch_shapes=[
            pltpu.VMEM((SUBLANES, conv_dim), F32),
            pltpu.VMEM(state_shape, F32),
            pltpu.VMEM((t, d_inner), F32),
        ],
        input_output_aliases=aliases,
        compiler_params=_cparams("parallel", "arbitrary"),
        name="ssd_scan",
    )(*operands, *extra_in)
    if native:
        return y, hfin.reshape(h0.shape)
    hfin = hfin.reshape(b, n_grp, d_st, n_heads // n_grp, SSM_HEAD_DIM)
    hfin = hfin.transpose(0, 1, 3, 4, 2).reshape(b, n_heads, SSM_HEAD_DIM, d_st)
    return y, hfin


def _gla_kernel(p_ref, s0_ref, wg_ref, gb_ref, nw_ref, tri_ref, o_ref, sfin_ref, state,
                *, tb, ch, valid, dk, dv):
    c = pl.program_id(1)
    n_h = GLA_N_HEADS
    hk, hv = dk // n_h, dv // n_h

    @pl.when(c == 0)
    def _init():
        state[...] = s0_ref[0]

    q = p_ref[0, :, 0:dk] * (hk ** -0.5)
    k = p_ref[0, :, dk:2 * dk]
    v = p_ref[0, :, 2 * dk:2 * dk + dv]
    r = p_ref[0, :, 2 * dk + dv:2 * dk + 2 * dv]
    g_low = p_ref[0, :, 2 * dk + 2 * dv:]
    x = _dot(g_low.astype(BF16), wg_ref[...]) + gb_ref[...]
    log_a = (jnp.minimum(x, 0.0) - jnp.log(1.0 + jnp.exp(-jnp.abs(x)))) / GLA_GATE_NORM
    if valid < tb:
        keep = lax.broadcasted_iota(jnp.int32, (tb, dk), 0) < valid
        log_a = jnp.where(keep, log_a, 0.0)
        k = jnp.where(keep, k, 0.0)

    bcum = _sel_left(tri_ref[...], log_a)
    nch = tb // ch

    def chunks(a):
        return a.reshape(nch, ch, a.shape[-1])

    bc3 = chunks(bcum)
    b_last = bc3[:, ch - 1:ch, :]
    q_t = chunks(q * jnp.exp(bcum)).astype(BF16)
    k_t = chunks(k * jnp.exp(-bcum)).astype(BF16)
    k_dec = (chunks(k) * jnp.exp(b_last - bc3)).astype(BF16)
    e_last = jnp.exp(b_last)
    v_b = chunks(v).astype(BF16)
    causal = (lax.broadcasted_iota(jnp.int32, (nch, ch, ch), 1)
              >= lax.broadcasted_iota(jnp.int32, (nch, ch, ch), 2))

    gate = r * _sigmoid(r)
    nw = nw_ref[...]
    for h in range(n_h):
        ks = slice(h * hk, (h + 1) * hk)
        vs = slice(h * hv, (h + 1) * hv)
        qh, kh, vh = q_t[:, :, ks], k_t[:, :, ks], v_b[:, :, vs]
        upd = jnp.einsum("jkv,jkd->jvd", vh, k_dec[:, :, ks], preferred_element_type=F32)
        s_run = state[h]
        entering = []
        for j in range(nch):
            entering.append(s_run.astype(BF16))
            s_run = s_run * e_last[j, :, ks] + upd[j]
        state[h] = s_run
        att = jnp.einsum("jqd,jkd->jqk", qh, kh, preferred_element_type=F32)
        att = jnp.where(causal, att, 0.0).astype(BF16)
        o = (jnp.einsum("jqk,jkv->jqv", att, vh, preferred_element_type=F32)
             + jnp.einsum("jqd,jvd->jqv", qh, jnp.stack(entering), preferred_element_type=F32))
        o_ref[0, :, vs] = (_rms(o.reshape(tb, hv), nw) * gate[:, vs]).astype(o_ref.dtype)

    @pl.when(c == pl.num_programs(1) - 1)
    def _fin():
        sfin_ref[0] = state[...]


def _gla(proj, s0, w_gate, gate_bias, norm_w, *, tb, ch, valid):
    b, l, in_dim = proj.shape
    n_h = GLA_N_HEADS
    rank, dk = w_gate.shape
    hv = norm_w.shape[0]
    dv = hv * n_h
    hk = dk // n_h
    assert l % tb == 0 and tb % ch == 0 and in_dim == 2 * dk + 2 * dv + rank
    body = functools.partial(_gla_kernel, tb=tb, ch=ch, valid=valid, dk=dk, dv=dv)
    blocktri = jnp.kron(jnp.eye(tb // ch, dtype=F32), jnp.tril(jnp.ones((ch, ch), F32))).astype(BF16)
    o, sfin = pl.pallas_call(
        body,
        grid=(b, l // tb),
        in_specs=[
            pl.BlockSpec((1, tb, in_dim), lambda i, c: (i, c, 0)),
            pl.BlockSpec((1, n_h, hv, hk), lambda i, c: (i, 0, 0, 0)),
            _resident((rank, dk)), _resident((1, dk)), _resident((1, hv)), _resident((tb, tb)),
        ],
        out_specs=[
            pl.BlockSpec((1, tb, dv), lambda i, c: (i, c, 0)),
            pl.BlockSpec((1, n_h, hv, hk), lambda i, c: (i, 0, 0, 0)),
        ],
        out_shape=[
            jax.ShapeDtypeStruct((b, l, dv), BF16),
            jax.ShapeDtypeStruct((b, n_h, hv, hk), F32),
        ],
        scratch_shapes=[pltpu.VMEM((n_h, hv, hk), F32)],
        compiler_params=_cparams("parallel", "arbitrary"),
        name="gla_scan",
    )(proj, jnp.swapaxes(s0, -1, -2), w_gate.astype(BF16), gate_bias.reshape(1, -1),
      norm_w.reshape(1, -1), blocktri)
    return o, jnp.swapaxes(sfin, -1, -2)


def _attn_prompt_kernel(q_ref, kp_ref, kc_ref, vp_ref, vc_ref, o_ref, lse_ref, *, dilation, nb, span):
    not_first = pl.program_id(2) > 0
    blk = ATT_BLOCK
    dh = ATT_HEAD_DIM
    units = [(r, jb) for r in range(dilation) for jb in range(nb)]
    n_u = len(units)
    iu = lax.broadcasted_iota(jnp.int32, (n_u, blk, blk), 0)
    iq = lax.broadcasted_iota(jnp.int32, (n_u, blk, blk), 1)
    ik = lax.broadcasted_iota(jnp.int32, (n_u, blk, blk), 2)
    ok_prev = (ik >= iq + (blk - span)) & (((iu & (nb - 1)) != 0) | not_first)
    ok_cur = ik <= iq
    scale = dh ** -0.5

    def rows(r, jb):
        start = r + dilation * jb * blk
        return pl.ds(start, blk) if dilation == 1 else pl.ds(start, blk, stride=dilation)

    def gather(cur_ref, prev_ref):
        cur = jnp.stack([cur_ref[0, rows(r, jb), :] for r, jb in units])
        prev = jnp.stack([prev_ref[0, rows(r, nb - 1), :] if jb == 0 else cur_ref[0, rows(r, jb - 1), :]
                          for r, jb in units])
        return cur.astype(BF16), prev.astype(BF16)

    q = jnp.stack([q_ref[0, rows(r, jb), :] for r, jb in units]).astype(BF16)
    k_cur, k_prev = gather(kc_ref, kp_ref)
    v_cur, v_prev = gather(vc_ref, vp_ref)
    outs, lses = [], []
    for hh in range(LANES // dh):
        hs = slice(hh * dh, (hh + 1) * dh)
        qh = q[:, :, hs]
        s_p = jnp.einsum("uqd,ukd->uqk", qh, k_prev[:, :, hs], preferred_element_type=F32) * scale
        s_c = jnp.einsum("uqd,ukd->uqk", qh, k_cur[:, :, hs], preferred_element_type=F32) * scale
        s_p = jnp.where(ok_prev, s_p, -jnp.inf)
        s_c = jnp.where(ok_cur, s_c, -jnp.inf)
        mx = jnp.max(jnp.maximum(s_p, s_c), axis=-1, keepdims=True)
        p_p = jnp.exp(s_p - mx)
        p_c = jnp.exp(s_c - mx)
        den = jnp.sum(p_p + p_c, axis=-1, keepdims=True)
        outs.append(jnp.einsum("uqk,ukd->uqd", (p_p / den).astype(BF16), v_prev[:, :, hs],
                               preferred_element_type=F32)
                    + jnp.einsum("uqk,ukd->uqd", (p_c / den).astype(BF16), v_cur[:, :, hs],
                                 preferred_element_type=F32))
        lses.append(jnp.broadcast_to(mx + jnp.log(den), (n_u, blk, dh)))
    o = jnp.concatenate(outs, axis=-1)
    lse = jnp.concatenate(lses, axis=-1)
    for u, (r, jb) in enumerate(units):
        o_ref[0, rows(r, jb), :] = o[u]
        lse_ref[0, rows(r, jb), :] = lse[u]


def _attn_prompt(qkv, gi, *, b, l, window, dilation):
    n_slab = qkv.shape[0] // 3
    spg = ATT_HEADS_PER_GROUP * ATT_HEAD_DIM // LANES
    span = window // dilation
    blk = ATT_BLOCK
    nb = max(1, min(l, ATT_SUPER_ROWS) // (dilation * blk))
    r_rows = dilation * nb * blk
    n_sup = l // r_rows
    assert l % r_rows == 0 and span <= blk and qkv.shape[1] == b * l and nb & (nb - 1) == 0

    def spec(part, prev):
        def index(bi, sp, i):
            i = jnp.maximum(i - 1, 0) if prev else i
            return (part * n_slab + gi * spg + sp, bi * n_sup + i, 0)
        return pl.BlockSpec((1, r_rows, LANES), index)

    out_spec = pl.BlockSpec((1, r_rows, LANES), lambda bi, sp, i: (sp, bi * n_sup + i, 0))
    return pl.pallas_call(
        functools.partial(_attn_prompt_kernel, dilation=dilation, nb=nb, span=span),
        grid=(b, spg, n_sup),
        in_specs=[spec(0, False), spec(1, True), spec(1, False), spec(2, True), spec(2, False)],
        out_specs=[out_spec, out_spec],
        out_shape=[jax.ShapeDtypeStruct((spg, b * l, LANES), F32)] * 2,
        compiler_params=_cparams("parallel", "parallel", "parallel"),
        name="attn_prompt",
    )(qkv, qkv, qkv, qkv, qkv)


def _attn_sample_kernel(q_ref, kn_ref, vn_ref, cache_ref, o_ref, lse_ref, cout_ref,
                        *, window, dilation, n_new, seqs):
    dh = ATT_HEAD_DIM
    n_h = ATT_HEADS_PER_GROUP
    rows = seqs * n_new
    scale = dh ** -0.5
    units = [(bb, h) for bb in range(seqs) for h in range(n_h)]
    n_u = len(units)

    def heads_of(ref):
        return jnp.stack([ref[h * dh // LANES][:, h * dh % LANES:h * dh % LANES + dh] for _, h in units])

    q = heads_of(q_ref).astype(BF16)
    k_new, v_new = heads_of(kn_ref), heads_of(vn_ref)
    k_t = cache_ref[:, 0].reshape(n_u, dh, window)
    v_t = cache_ref[:, 1].reshape(n_u, dh, window)

    row = lax.broadcasted_iota(jnp.int32, (n_u, rows, window), 1)
    col = lax.broadcasted_iota(jnp.int32, (n_u, rows, window), 2)
    sq = row & (n_new - 1)
    ok_cache = (col >= sq) & (((col - sq) & (dilation - 1)) == 0)
    row_n = lax.broadcasted_iota(jnp.int32, (n_u, rows, rows), 1)
    col_n = lax.broadcasted_iota(jnp.int32, (n_u, rows, rows), 2)
    unit = lax.broadcasted_iota(jnp.int32, (n_u, rows, rows), 0)
    first = sum(jnp.where(unit >= bb * n_h, n_new, 0) for bb in range(1, seqs))
    back = (row_n & (n_new - 1)) - (col_n & (n_new - 1))
    ok_new = (back >= 0) & ((back & (dilation - 1)) == 0) & (col_n >= first) & (col_n < first + n_new)

    s_c = jnp.einsum("urd,udw->urw", q, k_t.astype(BF16), preferred_element_type=F32) * scale
    s_n = jnp.einsum("urd,utd->urt", q, k_new.astype(BF16), preferred_element_type=F32) * scale
    s_c = jnp.where(ok_cache, s_c, -jnp.inf)
    s_n = jnp.where(ok_new, s_n, -jnp.inf)
    mx = jnp.maximum(jnp.max(s_c, axis=-1, keepdims=True), jnp.max(s_n, axis=-1, keepdims=True))
    p_c = jnp.exp(s_c - mx)
    p_n = jnp.exp(s_n - mx)
    den = jnp.sum(p_c, axis=-1, keepdims=True) + jnp.sum(p_n, axis=-1, keepdims=True)
    o = (jnp.einsum("urw,udw->urd", (p_c / den).astype(BF16), v_t.astype(BF16), preferred_element_type=F32)
         + jnp.einsum("urt,utd->urd", (p_n / den).astype(BF16), v_new.astype(BF16),
                      preferred_element_type=F32))
    lse = jnp.broadcast_to(mx + jnp.log(den), (n_u, rows, dh))

    lane_t = lax.broadcasted_iota(jnp.int32, (2 * rows, LANES), 1)
    tok_t = lax.broadcasted_iota(jnp.int32, (2 * rows, LANES), 0)
    lane_o = lax.broadcasted_iota(jnp.int32, (dh, LANES), 1)
    zpad = jnp.zeros((rows, dh), F32)
    for u, (bb, h) in enumerate(units):
        r0 = bb * n_new
        sl = h * dh // LANES
        hs = slice(h * dh % LANES, h * dh % LANES + dh)
        o_ref[sl, r0:r0 + n_new, hs] = o[u, r0:r0 + n_new]
        lse_ref[sl, r0:r0 + n_new, hs] = lse[u, r0:r0 + n_new]
        place = jnp.where(lane_t - (LANES - n_new) == tok_t - r0, 1.0, 0.0).astype(BF16)
        for kv, old, new in ((0, k_t[u], k_new[u]), (1, v_t[u], v_new[u])):
            tail = sum(lax.dot_general(part, place, _TN, preferred_element_type=F32)
                       for part in _split3(jnp.concatenate([new, zpad], axis=0)))
            shifted = pltpu.roll(old, window - n_new, 1)
            last = jnp.where(lane_o >= LANES - n_new, tail, shifted[:, window - LANES:])
            if window > LANES:
                cout_ref[bb, kv, h, :, :window - LANES] = shifted[:, :window - LANES]
            cout_ref[bb, kv, h, :, window - LANES:] = last


def _attn_sample(qkv, cache_t, gi, *, window, dilation, n_new):
    n_slab = qkv.shape[0] // 3
    spg = ATT_HEADS_PER_GROUP * ATT_HEAD_DIM // LANES
    b = cache_t.shape[0]
    seqs = SUBLANES // n_new
    rows = seqs * n_new
    assert rows == SUBLANES and b % seqs == 0 and qkv.shape[1] == b * n_new
    assert cache_t.shape[-1] == window and window % dilation == 0 and window % LANES == 0
    assert dilation & (dilation - 1) == 0 and n_new & (n_new - 1) == 0

    def slabs(part):
        return pl.BlockSpec((spg, rows, LANES), lambda i: (part * n_slab // spg + gi, i, 0))

    cache_spec = pl.BlockSpec((seqs,) + cache_t.shape[1:], lambda i: (i, 0, 0, 0, 0))
    out_spec = pl.BlockSpec((spg, rows, LANES), lambda i: (0, i, 0))
    return pl.pallas_call(
        functools.partial(_attn_sample_kernel, window=window, dilation=dilation, n_new=n_new, seqs=seqs),
        grid=(b // seqs,),
        in_specs=[slabs(0), slabs(1), slabs(2), cache_spec],
        out_specs=[out_spec, out_spec, cache_spec],
        out_shape=[jax.ShapeDtypeStruct((spg, b * n_new, LANES), F32)] * 2
        + [jax.ShapeDtypeStruct(cache_t.shape, F32)],
        compiler_params=_cparams("parallel"),
        name="attn_sample",
    )(qkv, qkv, qkv, cache_t)


def _rope_tables(pos):
    half = ATT_ROT_DIM // 2
    lane = lax.broadcasted_iota(jnp.int32, (pos.shape[0], LANES), 1) % ATT_HEAD_DIM
    x1, rot = lane < half, lane < ATT_ROT_DIM
    freq = jnp.where(x1, lane, lane - half)
    inv_freq = ROPE_THETA ** (-(2 * freq).astype(F32) / ATT_ROT_DIM)
    ang = pos.astype(F32)[:, None] * inv_freq
    cos, sin = jnp.cos(ang), jnp.sin(ang)
    c = jnp.where(rot, cos, 1.0)
    s1 = jnp.where(rot & jnp.logical_not(x1), sin, 0.0)
    s2 = jnp.where(x1, -sin, 0.0)
    return c, s1, s2


def _pad_rows(t, rows):
    return jnp.pad(t, ((0, 0), (0, rows - t.shape[1]), (0, 0)))


def kernel(x_prompt, x_sample, state_ssm, state_ssm_conv, state_gla, cache_kv_g0, cache_kv_g1, cache_kv_g2,
           norm_mix, norm_ffn, ffn_gate, ffn_up, ffn_down,
           ssm_w_in, ssm_conv_w, ssm_conv_b, ssm_dt_bias, ssm_a_log, ssm_d, ssm_norm, ssm_w_out,
           gla_w_in, gla_w_gate, gla_gate_bias, gla_norm, gla_w_out,
           att_w_qkv, att_w_out, norm_final):
    bp, lp, d = x_prompt.shape
    bs, ls, _ = x_sample.shape
    depth = norm_mix.shape[0]
    mp, ms = bp * lp, bs * ls
    tm_p = 512 if mp % 512 == 0 else mp
    tm_s = ms
    xp = x_prompt.reshape(mp, d)
    xs = x_sample.reshape(ms, d)
    att_caches = (cache_kv_g0, cache_kv_g1, cache_kv_g2)
    hg_w = ATT_HEADS_PER_GROUP * ATT_HEAD_DIM
    n_att = hg_w * len(ATT_GROUPS)
    rope_p = _rope_tables(jnp.arange(lp))
    rope_s = _rope_tables(jnp.tile(PAST_LEN + jnp.arange(ls), bs))
    ffn_w = tuple(w.astype(BF16) for w in (ffn_gate, ffn_up, ffn_down))
    ssm_w_in, ssm_w_out, gla_w_in, gla_w_out, att_w_qkv, att_w_out = (
        w.astype(BF16) for w in (ssm_w_in, ssm_w_out, gla_w_in, gla_w_out, att_w_qkv, att_w_out))

    ssm_p, conv_p, conv_s, gla_p, gla_s = [], [], [], [], []
    ssm_s = jnp.zeros(state_ssm.shape, F32)
    kv_p, kv_s = [[], [], []], [[], [], []]
    for i in range(depth):
        m, j = i % N_MIXERS, i // N_MIXERS
        if m == 0:
            w_in = (ssm_w_in, j)
            conv_dim = ssm_conv_w.shape[2]
            d_inner = ssm_w_out.shape[1]
            wts = (ssm_conv_w[j], ssm_conv_b[j], ssm_dt_bias[j], ssm_a_log[j], ssm_d[j], ssm_norm[j])
            conv0 = jnp.zeros((bp, SSM_CONV - 1, conv_dim), F32)
            zx_p, tails = _ssd_in_proj(xp, norm_mix[i], w_in, conv0, ssm_conv_w[j], ssm_conv_b[j],
                                       tm=tm_p, seq_len=lp, d_inner=d_inner)
            zx_p = zx_p.reshape(bp, lp, -1)
            zx_s = _norm_matmul(xs, norm_mix[i], w_in, tm=tm_s).reshape(bs, ls, -1)
            t_p = min(SSM_CHUNK, lp)
            y_p, h_p = _ssd(zx_p, conv0, jnp.zeros((bp,) + state_ssm.shape[2:], F32), *wts,
                            t=t_p, valid=t_p, conv_done=True)
            y_s, ssm_s = _ssd(_pad_rows(zx_s, SAMPLE_PAD), state_ssm_conv[j], state_ssm, *wts,
                              t=SAMPLE_PAD, valid=ls, stacked=(j, ssm_s))
            y_s = y_s[:, :ls]
            xbc_s = zx_s[:, :, d_inner:d_inner + conv_dim]
            conv_p.append(tails.reshape(bp, lp // tm_p, SUBLANES, conv_dim)[:, -1, -(SSM_CONV - 1):])
            conv_s.append(jnp.concatenate([state_ssm_conv[j], xbc_s], axis=1)[:, -(SSM_CONV - 1):])
            ssm_p.append(h_p)
            w_out = (ssm_w_out, j)
            mix_p, mix_s = y_p.reshape(mp, -1), y_s.reshape(ms, -1)
        elif m == 1:
            w_in = (gla_w_in, j)
            wts = (gla_w_gate[j], gla_gate_bias[j], gla_norm[j])
            pr_p = _norm_matmul(xp, norm_mix[i], w_in, tm=tm_p).reshape(bp, lp, -1)
            pr_s = _norm_matmul(xs, norm_mix[i], w_in, tm=tm_s).reshape(bs, ls, -1)
            ch_p = min(GLA_CHUNK, lp)
            tb_p = 256 if lp % 256 == 0 else ch_p
            o_p, s_p = _gla(pr_p, jnp.zeros((bp,) + state_gla.shape[2:], F32), *wts,
                            tb=tb_p, ch=ch_p, valid=tb_p)
            o_s, s_s = _gla(_pad_rows(pr_s, SAMPLE_PAD), state_gla[j], *wts,
                            tb=SAMPLE_PAD, ch=SAMPLE_PAD, valid=ls)
            o_s = o_s[:, :ls]
            gla_p.append(s_p)
            gla_s.append(s_s)
            w_out = (gla_w_out, j)
            mix_p, mix_s = o_p.reshape(mp, -1), o_s.reshape(ms, -1)
        else:
            w_qkv = (att_w_qkv, j)
            qkv_p = _norm_matmul(xp, norm_mix[i], w_qkv, tm=tm_p, rope=(2 * n_att, rope_p))
            qkv_s = _norm_matmul(xs, norm_mix[i], w_qkv, tm=tm_s, rope=(2 * n_att, rope_s))
            n_slab = n_att // LANES
            spg = hg_w // LANES
            qkv_p4 = qkv_p.reshape(3 * n_slab, bp, lp, LANES)
            os_p, ls_p, os_s, ls_s = [], [], [], []
            for gi, (window, dilation) in enumerate(ATT_GROUPS):
                o, lse = _attn_prompt(qkv_p, gi, b=bp, l=lp, window=window, dilation=dilation)
                os_p.append(o)
                ls_p.append(lse)
                keep = min(window, lp)
                kv = jnp.stack([qkv_p4[part * n_slab + gi * spg:part * n_slab + (gi + 1) * spg, :, lp - keep:]
                                for part in (1, 2)])
                kv = kv.reshape(2, spg, bp, keep, LANES // ATT_HEAD_DIM, ATT_HEAD_DIM)
                kv_p[gi].append(kv.transpose(2, 3, 0, 1, 4, 5)
                                .reshape(bp, keep, 2, ATT_HEADS_PER_GROUP, ATT_HEAD_DIM))
                cache_t = att_caches[gi][j].transpose(0, 2, 3, 4, 1)
                o, lse, cache_new = _attn_sample(qkv_s, cache_t, gi, window=window, dilation=dilation, n_new=ls)
                os_s.append(o)
                ls_s.append(lse)
                kv_s[gi].append(cache_new.transpose(0, 4, 1, 2, 3))
            w_out = (att_w_out, j)
            mix_p, mix_s = (os_p, ls_p), (os_s, ls_s)
        final_g = norm_final if i == depth - 1 else None
        xp = _block_tail(mix_p, w_out, xp, norm_ffn[i], *ffn_w, i, tm=tm_p, final_g=final_g)
        xs = _block_tail(mix_s, w_out, xs, norm_ffn[i], *ffn_w, i, tm=tm_s, final_g=final_g)
    y_prompt = xp.reshape(bp, lp, d)
    y_sample = xs.reshape(bs, ls, d)
    return (y_prompt, y_sample,
            jnp.stack(ssm_p), ssm_s, jnp.stack(conv_p), jnp.stack(conv_s),
            jnp.stack(gla_p), jnp.stack(gla_s),
            jnp.stack(kv_p[0]), jnp.stack(kv_s[0]), jnp.stack(kv_p[1]), jnp.stack(kv_s[1]),
            jnp.stack(kv_p[2]), jnp.stack(kv_s[2]))
```

```python
import functools

import jax
import jax.numpy as jnp
from jax import lax
from jax.experimental import pallas as pl
from jax.experimental.pallas import tpu as pltpu

F32 = jnp.float32
BF16 = jnp.bfloat16
LOG2_E = 1.4426950408889634

NORM_EPS = 1e-6
N_MIXERS = 3

SSM_HEAD_DIM = 64
SSM_N_GROUPS = 4
SSM_D_STATE = 128
SSM_CONV = 4
SSM_CHUNK = 128
SSM_IN_PROJ_CHUNKS = 4

GLA_N_HEADS = 4
GLA_GATE_NORM = 16.0
GLA_CHUNK = 32

ATT_GROUPS = ((128, 1), (512, 4), (2048, 16))
ATT_HEADS_PER_GROUP = 4
ATT_HEAD_DIM = 64
ATT_ROT_DIM = ATT_HEAD_DIM // 4
ROPE_THETA = 500000.0
ATT_BLOCK = 128
ATT_SUPER_ROWS = 2048
PAST_LEN = 8192

LANES = 128
SUBLANES = 8
SAMPLE_PAD = 16
VMEM_LIMIT = 56 * 1024 * 1024

_NT = (((1,), (1,)), ((), ()))
_TN = (((0,), (0,)), ((), ()))


def _cparams(*sem):
    return pltpu.CompilerParams(dimension_semantics=sem, vmem_limit_bytes=VMEM_LIMIT)


def _resident(shape):
    zeros = (0,) * len(shape)
    return pl.BlockSpec(shape, lambda *_: zeros, pipeline_mode=pl.Buffered(1))


def _layer(wl):
    w, layer = wl
    return pl.BlockSpec((None,) + w.shape[1:], lambda *_: (layer, 0, 0), pipeline_mode=pl.Buffered(1))


def _rms(x, g):
    inv = lax.rsqrt(jnp.mean(x * x, axis=-1, keepdims=True) + NORM_EPS)
    return x * inv * g


def _sigmoid(x):
    return 1.0 / (1.0 + jnp.exp(-x))


def _softplus(x):
    return jnp.maximum(x, 0.0) + jnp.log1p(jnp.exp(-jnp.abs(x)))


def _split3(v):
    hi = v.astype(BF16)
    r = v - hi.astype(F32)
    mid = r.astype(BF16)
    lo = (r - mid.astype(F32)).astype(BF16)
    return hi, mid, lo


def _dot(a, b):
    return jnp.dot(a, b, preferred_element_type=F32)


def _spread_matrix(n_blocks, width):
    k_dim = -(-3 * n_blocks // LANES) * LANES
    row = lax.broadcasted_iota(jnp.int32, (k_dim, n_blocks * width), 0)
    col = lax.broadcasted_iota(jnp.int32, (k_dim, n_blocks * width), 1)
    blk = row - jnp.where(row >= 2 * n_blocks, 2 * n_blocks, jnp.where(row >= n_blocks, n_blocks, 0))
    sel = (row < 3 * n_blocks) & (col >= blk * width) & (col < blk * width + width)
    return jnp.where(sel, 1.0, 0.0).astype(BF16)


def _spread(v, sel):
    rows, n_blocks = v.shape
    terms = [p.astype(F32) for p in _split3(v)]
    pad = jnp.zeros((rows, sel.shape[0] - 3 * n_blocks), F32)
    return _dot(jnp.concatenate(terms + [pad], axis=1).astype(BF16), sel)


def _sel_left(e, v):
    hi, mid, lo = _split3(v)
    return _dot(e, hi) + _dot(e, mid) + _dot(e, lo)


def _norm_matmul_kernel(x_ref, g_ref, w_ref, o_ref):
    h = _rms(x_ref[...], g_ref[...]).astype(BF16)
    o_ref[...] = _dot(h, w_ref[...])


def _norm_matmul_rope_kernel(x_ref, g_ref, w_ref, c_ref, s1_ref, s2_ref, o_ref, *, n_rot):
    h = _rms(x_ref[...], g_ref[...]).astype(BF16)
    res = _dot(h, w_ref[...])
    c, s1, s2 = c_ref[...], s1_ref[...], s2_ref[...]
    half = ATT_ROT_DIM // 2
    for j in range(o_ref.shape[0]):
        x = res[:, j * LANES:(j + 1) * LANES]
        if j < n_rot // LANES:
            x = x * c + pltpu.roll(x, half, 1) * s1 + pltpu.roll(x, LANES - half, 1) * s2
        o_ref[j] = x


def _norm_matmul(x, g, wl, *, tm, rope=None):
    m, d = x.shape
    n = wl[0].shape[2]
    assert m % tm == 0
    in_specs = [pl.BlockSpec((tm, d), lambda i: (i, 0)), _resident((1, d)), _layer(wl)]
    args = [x, g.reshape(1, d), wl[0]]
    if rope is None:
        body = _norm_matmul_kernel
        out_spec = pl.BlockSpec((tm, n), lambda i: (i, 0))
        out_shape = jax.ShapeDtypeStruct((m, n), F32)
    else:
        n_rot, tables = rope
        assert n % LANES == 0 and n_rot % LANES == 0
        body = functools.partial(_norm_matmul_rope_kernel, n_rot=n_rot)
        period = tables[0].shape[0] // tm
        assert tables[0].shape[0] % tm == 0 and (m // tm) % period == 0
        in_specs += [pl.BlockSpec((tm, LANES), lambda i: (i % period, 0))] * 3
        args += list(tables)
        out_spec = pl.BlockSpec((n // LANES, tm, LANES), lambda i: (0, i, 0))
        out_shape = jax.ShapeDtypeStruct((n // LANES, m, LANES), F32)
    return pl.pallas_call(
        body,
        grid=(m // tm,),
        in_specs=in_specs,
        out_specs=out_spec,
        out_shape=out_shape,
        compiler_params=_cparams("parallel"),
        name="norm_proj" if rope is None else "norm_proj_rope",
    )(*args)


def _conv_silu(raw, before, cw, cb):
    t, c = raw.shape
    groups = [before] + [raw[r:r + SUBLANES] for r in range(0, t, SUBLANES)]
    row8 = lax.broadcasted_iota(jnp.int32, (SUBLANES, c), 0)
    acc = cb
    for k in range(SSM_CONV - 1, 0, -1):
        rolled = [pltpu.roll(grp, k, 0) for grp in groups]
        shifted = jnp.concatenate([jnp.where(row8 < k, prev, here)
                                   for prev, here in zip(rolled[:-1], rolled[1:])], axis=0)
        acc = acc + shifted * cw[SSM_CONV - 1 - k:SSM_CONV - k, :]
    acc = acc + raw * cw[SSM_CONV - 1:SSM_CONV, :]
    return acc * _sigmoid(acc), groups[-1]


def _ssd_in_proj_kernel(x_ref, g_ref, w_ref, cbuf_ref, cw_ref, cb_ref, o_ref, tail_ref, tail,
                        *, d_inner, conv_dim, tiles_per_seq, col_chunk):
    @pl.when(pl.program_id(0) % tiles_per_seq == 0)
    def _start_of_sequence():
        tail[...] = cbuf_ref[0]

    h = _rms(x_ref[...], g_ref[...]).astype(BF16)
    n_chunks = conv_dim // col_chunk
    z_chunk = d_inner // n_chunks
    for j in range(n_chunks):
        c0 = j * col_chunk
        cs = slice(c0, c0 + col_chunk)
        raw = _dot(h, w_ref[:, d_inner + c0:d_inner + c0 + col_chunk])
        zs = slice(j * z_chunk, (j + 1) * z_chunk)
        o_ref[:, zs] = _dot(h, w_ref[:, zs])
        act, last = _conv_silu(raw, tail[:, cs], cw_ref[:, cs], cb_ref[:, cs])
        o_ref[:, d_inner + c0:d_inner + c0 + col_chunk] = act
        tail[:, cs] = last
        tail_ref[0, :, cs] = last
    o_ref[:, d_inner + conv_dim:] = _dot(h, w_ref[:, d_inner + conv_dim:])


def _ssd_in_proj(x, g, wl, conv_buf, conv_w, conv_b, *, tm, seq_len, d_inner):
    m, d = x.shape
    n = wl[0].shape[2]
    conv_dim = conv_w.shape[1]
    assert m % tm == 0 and seq_len % tm == 0
    cbuf = jnp.pad(conv_buf, ((0, 0), (SUBLANES - (SSM_CONV - 1), 0), (0, 0)))
    tiles_per_seq = seq_len // tm
    body = functools.partial(_ssd_in_proj_kernel, d_inner=d_inner, conv_dim=conv_dim,
                             tiles_per_seq=tiles_per_seq, col_chunk=conv_dim // SSM_IN_PROJ_CHUNKS)
    assert conv_dim % (SSM_IN_PROJ_CHUNKS * LANES) == 0 and d_inner % (SSM_IN_PROJ_CHUNKS * LANES) == 0
    return pl.pallas_call(
        body,
        grid=(m // tm,),
        in_specs=[pl.BlockSpec((tm, d), lambda i: (i, 0)), _resident((1, d)), _layer(wl),
                  pl.BlockSpec((1, SUBLANES, conv_dim), lambda i: (i // tiles_per_seq, 0, 0)),
                  _resident((SSM_CONV, conv_dim)), _resident((1, conv_dim))],
        out_specs=[pl.BlockSpec((tm, n), lambda i: (i, 0)),
                   pl.BlockSpec((1, SUBLANES, conv_dim), lambda i: (i, 0, 0))],
        out_shape=[jax.ShapeDtypeStruct((m, n), F32),
                   jax.ShapeDtypeStruct((m // tm, SUBLANES, conv_dim), F32)],
        scratch_shapes=[pltpu.VMEM((SUBLANES, conv_dim), F32)],
        compiler_params=_cparams("arbitrary"),
        name="ssd_in_proj",
    )(x, g.reshape(1, d), wl[0], cbuf, conv_w, conv_b.reshape(1, -1))


def _mixed_heads(o_refs, l_refs):
    spg = o_refs[0].shape[0]
    pieces = [[None] * spg for _ in o_refs]
    for s in range(spg):
        ls = [l_ref[s] for l_ref in l_refs]
        mx = functools.reduce(jnp.maximum, ls)
        es = [jnp.exp(l - mx) for l in ls]
        den = functools.reduce(lambda a, b: a + b, es)
        for grp, (o_ref, e) in enumerate(zip(o_refs, es)):
            pieces[grp][s] = o_ref[s] * (e / den)
    return jnp.concatenate([p for grp in pieces for p in grp], axis=-1).astype(BF16)


def _block_tail_kernel(*refs, n_groups, final_norm):
    n_mix = 2 * n_groups if n_groups else 1
    wo_ref, x_ref, g_ref, wg_ref, wu_ref, wd_ref, gf_ref, o_ref = refs[n_mix:]
    y = _mixed_heads(refs[:n_groups], refs[n_groups:n_mix]) if n_groups else refs[0][...]
    x = x_ref[...] + _dot(y, wo_ref[...])
    h = _rms(x, g_ref[...]).astype(BF16)
    gate = _dot(h, wg_ref[...])
    up = _dot(h, wu_ref[...])
    act = (gate * _sigmoid(gate) * up).astype(BF16)
    x = x + _dot(act, wd_ref[...])
    o_ref[...] = _rms(x, gf_ref[...]) if final_norm else x


def _block_tail(mix, wol, x, g, wg, wu, wd, layer, *, tm, final_g=None):
    m, d = x.shape
    assert m % tm == 0
    gf = g if final_g is None else final_g
    if isinstance(mix, tuple):
        os_, lses = mix
        n_groups = len(os_)
        mix_args = [*os_, *lses]
        mix_specs = [pl.BlockSpec((os_[0].shape[0], tm, LANES), lambda i: (0, i, 0))] * (2 * n_groups)
    else:
        n_groups = 0
        mix_args = [mix]
        mix_specs = [pl.BlockSpec((tm, mix.shape[1]), lambda i: (i, 0))]
    return pl.pallas_call(
        functools.partial(_block_tail_kernel, n_groups=n_groups, final_norm=final_g is not None),
        grid=(m // tm,),
        in_specs=mix_specs + [_layer(wol), pl.BlockSpec((tm, d), lambda i: (i, 0)), _resident((1, d)),
                              _layer((wg, layer)), _layer((wu, layer)), _layer((wd, layer)),
                              _resident((1, d))],
        out_specs=pl.BlockSpec((tm, d), lambda i: (i, 0)),
        out_shape=jax.ShapeDtypeStruct((m, d), F32),
        compiler_params=_cparams("parallel"),
        name="proj_swiglu",
    )(*mix_args, wol[0], x, g.reshape(1, d), wg, wu, wd, gf.reshape(1, d))


def _ssd_kernel(zx_ref, cbuf_ref, h0_ref, cw_ref, cb_ref, dtb_ref, alog_ref, dsk_ref, nw_ref,
                tri_ref, triu_ref, sel_hp_ref, *rest,
                t, valid, d_inner, n_heads, conv_done, native):
    y_ref, hfin_ref, tail, state, yacc = rest[-5:]
    c = pl.program_id(1)
    n_grp = SSM_N_GROUPS
    d_st = SSM_D_STATE
    hp = d_inner // n_grp
    hpg = n_heads // n_grp
    pdim = SSM_HEAD_DIM
    bc_w = n_grp * d_st
    conv_dim = d_inner + 2 * bc_w

    @pl.when(c == 0)
    def _init():
        state[...] = h0_ref[0]
        tail[...] = cbuf_ref[0]

    xbc = zx_ref[0, :, d_inner:d_inner + conv_dim]
    if not conv_done:
        xbc, tail[...] = _conv_silu(xbc, tail[...], cw_ref[...], cb_ref[...])
    xs = xbc[:, :d_inner]
    bm = xbc[:, d_inner:d_inner + bc_w]
    cm = xbc[:, d_inner + bc_w:]

    dt = _softplus(zx_ref[0, :, d_inner + conv_dim:] + dtb_ref[...])
    if valid < t:
        dt = jnp.where(lax.broadcasted_iota(jnp.int32, dt.shape, 0) < valid, dt, 0.0)
    a = dt * (-jnp.exp(alog_ref[...]))

    iq = lax.broadcasted_iota(jnp.int32, (t, t), 0)
    ik = lax.broadcasted_iota(jnp.int32, (t, t), 1)
    low = iq >= ik
    a_cs = _sel_left(tri_ref[...], a)
    a_cs_t = sum(lax.dot_general(part, triu_ref[...], _TN, preferred_element_type=F32)
                 for part in _split3(a))

    a_last = a_cs[t - 1:t, :]
    pad_rows = 2 * SUBLANES
    stack = jnp.concatenate([dt, dt * jnp.exp(a_last - a_cs), jnp.exp(a_cs),
                             jnp.broadcast_to(jnp.exp(a_last), (pad_rows, n_heads))], axis=0)
    ex = _spread(stack, sel_hp_ref[...])
    xdt_b = (xs * ex[0:t]).astype(BF16)
    xdst_b = (xs * ex[t:2 * t]).astype(BF16)
    from_start = ex[2 * t:3 * t]
    chunk_decay = ex[3 * t:3 * t + pad_rows]
    if native:
        only0 = jnp.where(lax.broadcasted_iota(jnp.int32, chunk_decay.shape, 0) == 0, chunk_decay, 0.0)
        terms = jnp.concatenate(_split3(only0), axis=0)
        decay_rows = lax.dot_general(terms, jnp.ones((terms.shape[0], d_st), BF16), _TN,
                                     preferred_element_type=F32)

    a_cs2 = a_cs * LOG2_E
    a_row = a_cs_t * LOG2_E

    pair = LANES // pdim
    for g in range(n_grp):
        bg = bm[:, g * d_st:(g + 1) * d_st].astype(BF16)
        cg = cm[:, g * d_st:(g + 1) * d_st].astype(BF16)
        cb = lax.dot_general(cg, bg, _NT, preferred_element_type=F32)
        sg = state[g]
        sg_b = sg.astype(BF16)
        gl = slice(g * hp, (g + 1) * hp)
        carried = lax.dot_general(cg, sg_b, _NT, preferred_element_type=F32) if native else _dot(cg, sg_b)
        yacc[:, gl] = carried * from_start[:, gl]
        for h0 in range(0, hpg, pair):
            outs = []
            for hh in range(g * hpg + h0, g * hpg + h0 + pair):
                ps = slice(hh * pdim, (hh + 1) * pdim)
                decay = jnp.exp2(jnp.where(low, a_cs2[:, hh:hh + 1] - a_row[hh:hh + 1, :], -jnp.inf))
                outs.append(_dot((cb * decay).astype(BF16), xdt_b[:, ps]))
            lo = (g * hpg + h0) * pdim
            yacc[:, lo:lo + pair * pdim] += jnp.concatenate(outs, axis=1)
        if native:
            upd = lax.dot_general(xdst_b[:, gl], bg, _TN, preferred_element_type=F32)
            state[g] = sg * decay_rows[gl, :] + upd
        else:
            upd = lax.dot_general(bg, xdst_b[:, gl], _TN, preferred_element_type=F32)
            state[g] = sg * chunk_decay[0:1, gl] + upd

    z = zx_ref[0, :, :d_inner]
    y = (yacc[...] + xs * dsk_ref[...]) * (z * _sigmoid(z))
    nw = nw_ref[...]
    for g in range(n_grp):
        gl = slice(g * hp, (g + 1) * hp)
        y_ref[0, :, gl] = _rms(y[:, gl], nw[:, gl]).astype(y_ref.dtype)

    @pl.when(c == pl.num_programs(1) - 1)
    def _fin():
        hfin_ref[0] = state[...]


def _ssd(zx, conv_buf, h0, conv_w, conv_b, dt_bias, a_log, d_skip, norm_w, *, t, valid, conv_done=False,
         stacked=None):
    b, l, in_dim = zx.shape
    n_heads = a_log.shape[0]
    d_inner = n_heads * SSM_HEAD_DIM
    conv_dim = conv_w.shape[1]
    n_grp, d_st = SSM_N_GROUPS, SSM_D_STATE
    hp = d_inner // n_grp
    assert l % t == 0 and in_dim == d_inner + conv_dim + n_heads
    cbuf = jnp.pad(conv_buf, ((0, 0), (SUBLANES - (SSM_CONV - 1), 0), (0, 0)))
    native = stacked is not None
    extra_in, extra_specs, aliases = [], [], {}
    if native:
        layer, out_buf = stacked
        state_shape = (n_grp, hp, d_st)
        h0k = h0.reshape(h0.shape[0], b, *state_shape)
        state_spec = pl.BlockSpec((None, 1) + state_shape, lambda i, c: (layer, i, 0, 0, 0))
        state_out = jax.ShapeDtypeStruct(h0k.shape, F32)
        extra_in, extra_specs = [out_buf.reshape(h0k.shape)], [pl.BlockSpec(memory_space=pl.ANY)]
    else:
        state_shape = (n_grp, d_st, hp)
        h0k = h0.reshape(b, n_grp, n_heads // n_grp, SSM_HEAD_DIM, d_st)
        h0k = h0k.transpose(0, 1, 4, 2, 3).reshape(b, *state_shape)
        state_spec = pl.BlockSpec((1,) + state_shape, lambda i, c: (i, 0, 0, 0))
        state_out = jax.ShapeDtypeStruct(h0k.shape, F32)
    body = functools.partial(_ssd_kernel, t=t, valid=valid, d_inner=d_inner, n_heads=n_heads,
                             conv_done=conv_done, native=native)
    tri = jnp.tril(jnp.ones((t, t), BF16))
    consts = (tri, tri.T, _spread_matrix(n_heads, SSM_HEAD_DIM))
    operands = [zx, cbuf, h0k, conv_w, conv_b.reshape(1, -1), dt_bias.reshape(1, -1), a_log.reshape(1, -1),
                jnp.repeat(d_skip, SSM_HEAD_DIM).reshape(1, -1), norm_w.reshape(1, -1), *consts]
    if extra_in:
        aliases = {len(operands): 1}
    y, hfin = pl.pallas_call(
        body,
        grid=(b, l // t),
        in_specs=[
            pl.BlockSpec((1, t, in_dim), lambda i, c: (i, c, 0)),
            pl.BlockSpec((1, SUBLANES, conv_dim), lambda i, c: (i, 0, 0)),
            state_spec,
            _resident((SSM_CONV, conv_dim)), _resident((1, conv_dim)),
            _resident((1, n_heads)), _resident((1, n_heads)),
            _resident((1, d_inner)), _resident((1, d_inner)),
        ] + [_resident(cst.shape) for cst in consts] + extra_specs,
        out_specs=[pl.BlockSpec((1, t, d_inner), lambda i, c: (i, c, 0)), state_spec],
        out_shape=[jax.ShapeDtypeStruct((b, l, d_inner), BF16), state_out],
        scratch_shapes=[
            pltpu.VMEM((SUBLANES, conv_dim), F32),
            pltpu.VMEM(state_shape, F32),
            pltpu.VMEM((t, d_inner), F32),
        ],
        input_output_aliases=aliases,
        compiler_params=_cparams("parallel", "arbitrary"),
        name="ssd_scan",
    )(*operands, *extra_in)
    if native:
        return y, hfin.reshape(h0.shape)
    hfin = hfin.reshape(b, n_grp, d_st, n_heads // n_grp, SSM_HEAD_DIM)
    hfin = hfin.transpose(0, 1, 3, 4, 2).reshape(b, n_heads, SSM_HEAD_DIM, d_st)
    return y, hfin


def _gla_kernel(p_ref, s0_ref, wg_ref, gb_ref, nw_ref, tri_ref, o_ref, sfin_ref, state,
                *, tb, ch, valid, dk, dv):
    c = pl.program_id(1)
    n_h = GLA_N_HEADS
    hk, hv = dk // n_h, dv // n_h

    @pl.when(c == 0)
    def _init():
        state[...] = s0_ref[0]

    q = p_ref[0, :, 0:dk] * (hk ** -0.5)
    k = p_ref[0, :, dk:2 * dk]
    v = p_ref[0, :, 2 * dk:2 * dk + dv]
    r = p_ref[0, :, 2 * dk + dv:2 * dk + 2 * dv]
    g_low = p_ref[0, :, 2 * dk + 2 * dv:]
    x = _dot(g_low.astype(BF16), wg_ref[...]) + gb_ref[...]
    log_a = (jnp.minimum(x, 0.0) - jnp.log(1.0 + jnp.exp(-jnp.abs(x)))) / GLA_GATE_NORM
    if valid < tb:
        keep = lax.broadcasted_iota(jnp.int32, (tb, dk), 0) < valid
        log_a = jnp.where(keep, log_a, 0.0)
        k = jnp.where(keep, k, 0.0)

    bcum = _sel_left(tri_ref[...], log_a)
    nch = tb // ch

    def chunks(a):
        return a.reshape(nch, ch, a.shape[-1])

    bc3 = chunks(bcum)
    b_last = bc3[:, ch - 1:ch, :]
    q_t = chunks(q * jnp.exp(bcum)).astype(BF16)
    k_t = chunks(k * jnp.exp(-bcum)).astype(BF16)
    k_dec = (chunks(k) * jnp.exp(b_last - bc3)).astype(BF16)
    e_last = jnp.exp(b_last)
    v_b = chunks(v).astype(BF16)
    causal = (lax.broadcasted_iota(jnp.int32, (nch, ch, ch), 1)
              >= lax.broadcasted_iota(jnp.int32, (nch, ch, ch), 2))

    gate = r * _sigmoid(r)
    nw = nw_ref[...]
    for h in range(n_h):
        ks = slice(h * hk, (h + 1) * hk)
        vs = slice(h * hv, (h + 1) * hv)
        qh, kh, vh = q_t[:, :, ks], k_t[:, :, ks], v_b[:, :, vs]
        upd = jnp.einsum("jkv,jkd->jvd", vh, k_dec[:, :, ks], preferred_element_type=F32)
        s_run = state[h]
        entering = []
        for j in range(nch):
            entering.append(s_run.astype(BF16))
            s_run = s_run * e_last[j, :, ks] + upd[j]
        state[h] = s_run
        att = jnp.einsum("jqd,jkd->jqk", qh, kh, preferred_element_type=F32)
        att = jnp.where(causal, att, 0.0).astype(BF16)
        o = (jnp.einsum("jqk,jkv->jqv", att, vh, preferred_element_type=F32)
             + jnp.einsum("jqd,jvd->jqv", qh, jnp.stack(entering), preferred_element_type=F32))
        o_ref[0, :, vs] = (_rms(o.reshape(tb, hv), nw) * gate[:, vs]).astype(o_ref.dtype)

    @pl.when(c == pl.num_programs(1) - 1)
    def _fin():
        sfin_ref[0] = state[...]


def _gla(proj, s0, w_gate, gate_bias, norm_w, *, tb, ch, valid):
    b, l, in_dim = proj.shape
    n_h = GLA_N_HEADS
    rank, dk = w_gate.shape
    hv = norm_w.shape[0]
    dv = hv * n_h
    hk = dk // n_h
    assert l % tb == 0 and tb % ch == 0 and in_dim == 2 * dk + 2 * dv + rank
    body = functools.partial(_gla_kernel, tb=tb, ch=ch, valid=valid, dk=dk, dv=dv)
    blocktri = jnp.kron(jnp.eye(tb // ch, dtype=F32), jnp.tril(jnp.ones((ch, ch), F32))).astype(BF16)
    o, sfin = pl.pallas_call(
        body,
        grid=(b, l // tb),
        in_specs=[
            pl.BlockSpec((1, tb, in_dim), lambda i, c: (i, c, 0)),
            pl.BlockSpec((1, n_h, hv, hk), lambda i, c: (i, 0, 0, 0)),
            _resident((rank, dk)), _resident((1, dk)), _resident((1, hv)), _resident((tb, tb)),
        ],
        out_specs=[
            pl.BlockSpec((1, tb, dv), lambda i, c: (i, c, 0)),
            pl.BlockSpec((1, n_h, hv, hk), lambda i, c: (i, 0, 0, 0)),
        ],
        out_shape=[
            jax.ShapeDtypeStruct((b, l, dv), BF16),
            jax.ShapeDtypeStruct((b, n_h, hv, hk), F32),
        ],
        scratch_shapes=[pltpu.VMEM((n_h, hv, hk), F32)],
        compiler_params=_cparams("parallel", "arbitrary"),
        name="gla_scan",
    )(proj, jnp.swapaxes(s0, -1, -2), w_gate.astype(BF16), gate_bias.reshape(1, -1),
      norm_w.reshape(1, -1), blocktri)
    return o, jnp.swapaxes(sfin, -1, -2)


def _attn_prompt_kernel(q_ref, kp_ref, kc_ref, vp_ref, vc_ref, o_ref, lse_ref, *, dilation, nb):
    not_first = pl.program_id(2) > 0
    blk = ATT_BLOCK
    dh = ATT_HEAD_DIM
    units = [(r, jb) for r in range(dilation) for jb in range(nb)]
    n_u = len(units)
    iu = lax.broadcasted_iota(jnp.int32, (n_u, blk, blk), 0)
    iq = lax.broadcasted_iota(jnp.int32, (n_u, blk, blk), 1)
    ik = lax.broadcasted_iota(jnp.int32, (n_u, blk, blk), 2)
    ok_prev = (ik >= iq) & (((iu & (nb - 1)) != 0) | not_first)
    ok_cur = ik <= iq
    scale = dh ** -0.5

    def rows(r, jb):
        start = r + dilation * jb * blk
        return pl.ds(start, blk) if dilation == 1 else pl.ds(start, blk, stride=dilation)

    def gather(cur_ref, prev_ref):
        cur = jnp.stack([cur_ref[0, rows(r, jb), :] for r, jb in units])
        prev = jnp.stack([prev_ref[0, rows(r, nb - 1), :] if jb == 0 else cur_ref[0, rows(r, jb - 1), :]
                          for r, jb in units])
        return cur.astype(BF16), prev.astype(BF16)

    q = (jnp.stack([q_ref[0, rows(r, jb), :] for r, jb in units]) * scale).astype(BF16)
    k_cur, k_prev = gather(kc_ref, kp_ref)
    v_cur, v_prev = gather(vc_ref, vp_ref)
    outs, lses = [], []
    for hh in range(LANES // dh):
        hs = slice(hh * dh, (hh + 1) * dh)
        qh = q[:, :, hs]
        s_p = jnp.einsum("uqd,ukd->uqk", qh, k_prev[:, :, hs], preferred_element_type=F32)
        s_c = jnp.einsum("uqd,ukd->uqk", qh, k_cur[:, :, hs], preferred_element_type=F32)
        s_p = jnp.where(ok_prev, s_p, -jnp.inf)
        s_c = jnp.where(ok_cur, s_c, -jnp.inf)
        mx = jnp.max(jnp.maximum(s_p, s_c), axis=-1, keepdims=True)
        p_p = jnp.exp(s_p - mx)
        p_c = jnp.exp(s_c - mx)
        den = jnp.sum(p_p + p_c, axis=-1, keepdims=True)
        outs.append(jnp.einsum("uqk,ukd->uqd", (p_p / den).astype(BF16), v_prev[:, :, hs],
                               preferred_element_type=F32)
                    + jnp.einsum("uqk,ukd->uqd", (p_c / den).astype(BF16), v_cur[:, :, hs],
                                 preferred_element_type=F32))
        lses.append(jnp.broadcast_to(mx + jnp.log(den), (n_u, blk, dh)))
    o = jnp.concatenate(outs, axis=-1)
    lse = jnp.concatenate(lses, axis=-1)
    for u, (r, jb) in enumerate(units):
        o_ref[0, rows(r, jb), :] = o[u]
        lse_ref[0, rows(r, jb), :] = lse[u]


def _attn_prompt(qkv, gi, *, b, l, window, dilation):
    n_slab = qkv.shape[0] // 3
    spg = ATT_HEADS_PER_GROUP * ATT_HEAD_DIM // LANES
    span = window // dilation
    blk = ATT_BLOCK
    nb = max(1, min(l, ATT_SUPER_ROWS) // (dilation * blk))
    r_rows = dilation * nb * blk
    n_sup = l // r_rows
    assert l % r_rows == 0 and span == blk and qkv.shape[1] == b * l and nb & (nb - 1) == 0

    def spec(part, prev):
        def index(bi, sp, i):
            i = jnp.maximum(i - 1, 0) if prev else i
            return (part * n_slab + gi * spg + sp, bi * n_sup + i, 0)
        return pl.BlockSpec((1, r_rows, LANES), index)

    out_spec = pl.BlockSpec((1, r_rows, LANES), lambda bi, sp, i: (sp, bi * n_sup + i, 0))
    return pl.pallas_call(
        functools.partial(_attn_prompt_kernel, dilation=dilation, nb=nb),
        grid=(b, spg, n_sup),
        in_specs=[spec(0, False), spec(1, True), spec(1, False), spec(2, True), spec(2, False)],
        out_specs=[out_spec, out_spec],
        out_shape=[jax.ShapeDtypeStruct((spg, b * l, LANES), F32)] * 2,
        compiler_params=_cparams("parallel", "parallel", "parallel"),
        name="attn_prompt",
    )(qkv, qkv, qkv, qkv, qkv)


def _attn_sample_kernel(q_ref, kn_ref, vn_ref, cache_ref, o_ref, lse_ref, cout_ref,
                        *, window, dilation, n_new, seqs):
    dh = ATT_HEAD_DIM
    n_h = ATT_HEADS_PER_GROUP
    rows = seqs * n_new
    scale = dh ** -0.5
    units = [(bb, h) for bb in range(seqs) for h in range(n_h)]
    n_u = len(units)

    def heads_of(ref):
        return jnp.stack([ref[h * dh // LANES][:, h * dh % LANES:h * dh % LANES + dh] for _, h in units])

    q = heads_of(q_ref).astype(BF16)
    k_new, v_new = heads_of(kn_ref), heads_of(vn_ref)
    k_t = cache_ref[:, 0].reshape(n_u, dh, window)
    v_t = cache_ref[:, 1].reshape(n_u, dh, window)

    row = lax.broadcasted_iota(jnp.int32, (n_u, rows, window), 1)
    col = lax.broadcasted_iota(jnp.int32, (n_u, rows, window), 2)
    sq = row & (n_new - 1)
    ok_cache = (col >= sq) & (((col - sq) & (dilation - 1)) == 0)
    row_n = lax.broadcasted_iota(jnp.int32, (n_u, rows, rows), 1)
    col_n = lax.broadcasted_iota(jnp.int32, (n_u, rows, rows), 2)
    unit = lax.broadcasted_iota(jnp.int32, (n_u, rows, rows), 0)
    first = sum(jnp.where(unit >= bb * n_h, n_new, 0) for bb in range(1, seqs))
    back = (row_n & (n_new - 1)) - (col_n & (n_new - 1))
    ok_new = (back >= 0) & ((back & (dilation - 1)) == 0) & (col_n >= first) & (col_n < first + n_new)

    s_c = jnp.einsum("urd,udw->urw", q, k_t.astype(BF16), preferred_element_type=F32) * scale
    s_n = jnp.einsum("urd,utd->urt", q, k_new.astype(BF16), preferred_element_type=F32) * scale
    s_c = jnp.where(ok_cache, s_c, -jnp.inf)
    s_n = jnp.where(ok_new, s_n, -jnp.inf)
    mx = jnp.maximum(jnp.max(s_c, axis=-1, keepdims=True), jnp.max(s_n, axis=-1, keepdims=True))
    p_c = jnp.exp(s_c - mx)
    p_n = jnp.exp(s_n - mx)
    den = jnp.sum(p_c, axis=-1, keepdims=True) + jnp.sum(p_n, axis=-1, keepdims=True)
    o = (jnp.einsum("urw,udw->urd", (p_c / den).astype(BF16), v_t.astype(BF16), preferred_element_type=F32)
         + jnp.einsum("urt,utd->urd", (p_n / den).astype(BF16), v_new.astype(BF16),
                      preferred_element_type=F32))
    lse = jnp.broadcast_to(mx + jnp.log(den), (n_u, rows, dh))

    lane_t = lax.broadcasted_iota(jnp.int32, (2 * rows, LANES), 1)
    tok_t = lax.broadcasted_iota(jnp.int32, (2 * rows, LANES), 0)
    lane_o = lax.broadcasted_iota(jnp.int32, (dh, LANES), 1)
    zpad = jnp.zeros((rows, dh), F32)
    for u, (bb, h) in enumerate(units):
        r0 = bb * n_new
        sl = h * dh // LANES
        hs = slice(h * dh % LANES, h * dh % LANES + dh)
        o_ref[sl, r0:r0 + n_new, hs] = o[u, r0:r0 + n_new]
        lse_ref[sl, r0:r0 + n_new, hs] = lse[u, r0:r0 + n_new]
        place = jnp.where(lane_t - (LANES - n_new) == tok_t - r0, 1.0, 0.0).astype(BF16)
        for kv, old, new in ((0, k_t[u], k_new[u]), (1, v_t[u], v_new[u])):
            tail = sum(lax.dot_general(part, place, _TN, preferred_element_type=F32)
                       for part in _split3(jnp.concatenate([new, zpad], axis=0)))
            shifted = pltpu.roll(old, window - n_new, 1)
            last = jnp.where(lane_o >= LANES - n_new, tail, shifted[:, window - LANES:])
            if window > LANES:
                cout_ref[bb, kv, h, :, :window - LANES] = shifted[:, :window - LANES]
            cout_ref[bb, kv, h, :, window - LANES:] = last


def _attn_sample(qkv, cache_t, gi, *, window, dilation, n_new):
    n_slab = qkv.shape[0] // 3
    spg = ATT_HEADS_PER_GROUP * ATT_HEAD_DIM // LANES
    b = cache_t.shape[0]
    seqs = SUBLANES // n_new
    rows = seqs * n_new
    assert rows == SUBLANES and b % seqs == 0 and qkv.shape[1] == b * n_new
    assert cache_t.shape[-1] == window and window % dilation == 0 and window % LANES == 0
    assert dilation & (dilation - 1) == 0 and n_new & (n_new - 1) == 0

    def slabs(part):
        return pl.BlockSpec((spg, rows, LANES), lambda i: (part * n_slab // spg + gi, i, 0))

    cache_spec = pl.BlockSpec((seqs,) + cache_t.shape[1:], lambda i: (i, 0, 0, 0, 0))
    out_spec = pl.BlockSpec((spg, rows, LANES), lambda i: (0, i, 0))
    return pl.pallas_call(
        functools.partial(_attn_sample_kernel, window=window, dilation=dilation, n_new=n_new, seqs=seqs),
        grid=(b // seqs,),
        in_specs=[slabs(0), slabs(1), slabs(2), cache_spec],
        out_specs=[out_spec, out_spec, cache_spec],
        out_shape=[jax.ShapeDtypeStruct((spg, b * n_new, LANES), F32)] * 2
        + [jax.ShapeDtypeStruct(cache_t.shape, F32)],
        compiler_params=_cparams("parallel"),
        name="attn_sample",
    )(qkv, qkv, qkv, cache_t)


def _rope_tables(pos):
    half = ATT_ROT_DIM // 2
    lane = lax.broadcasted_iota(jnp.int32, (pos.shape[0], LANES), 1) % ATT_HEAD_DIM
    x1, rot = lane < half, lane < ATT_ROT_DIM
    freq = jnp.where(x1, lane, lane - half)
    inv_freq = ROPE_THETA ** (-(2 * freq).astype(F32) / ATT_ROT_DIM)
    ang = pos.astype(F32)[:, None] * inv_freq
    cos, sin = jnp.cos(ang), jnp.sin(ang)
    c = jnp.where(rot, cos, 1.0)
    s1 = jnp.where(rot & jnp.logical_not(x1), sin, 0.0)
    s2 = jnp.where(x1, -sin, 0.0)
    return c, s1, s2


def _pad_rows(t, rows):
    return jnp.pad(t, ((0, 0), (0, rows - t.shape[1]), (0, 0)))


def kernel(x_prompt, x_sample, state_ssm, state_ssm_conv, state_gla, cache_kv_g0, cache_kv_g1, cache_kv_g2,
           norm_mix, norm_ffn, ffn_gate, ffn_up, ffn_down,
           ssm_w_in, ssm_conv_w, ssm_conv_b, ssm_dt_bias, ssm_a_log, ssm_d, ssm_norm, ssm_w_out,
           gla_w_in, gla_w_gate, gla_gate_bias, gla_norm, gla_w_out,
           att_w_qkv, att_w_out, norm_final):
    bp, lp, d = x_prompt.shape
    bs, ls, _ = x_sample.shape
    depth = norm_mix.shape[0]
    mp, ms = bp * lp, bs * ls
    tm_p = 512 if mp % 512 == 0 else mp
    tm_wide = 2 * tm_p if lp % (2 * tm_p) == 0 else tm_p
    tm_s = ms
    xp = x_prompt.reshape(mp, d)
    xs = x_sample.reshape(ms, d)
    att_caches = (cache_kv_g0, cache_kv_g1, cache_kv_g2)
    hg_w = ATT_HEADS_PER_GROUP * ATT_HEAD_DIM
    n_att = hg_w * len(ATT_GROUPS)
    rope_p = _rope_tables(jnp.arange(lp))
    rope_s = _rope_tables(jnp.tile(PAST_LEN + jnp.arange(ls), bs))
    ffn_w = tuple(w.astype(BF16) for w in (ffn_gate, ffn_up, ffn_down))
    ssm_w_in, ssm_w_out, gla_w_in, gla_w_out, att_w_qkv, att_w_out = (
        w.astype(BF16) for w in (ssm_w_in, ssm_w_out, gla_w_in, gla_w_out, att_w_qkv, att_w_out))

    ssm_p, conv_p, conv_s, gla_p, gla_s = [], [], [], [], []
    ssm_s = jnp.zeros(state_ssm.shape, F32)
    kv_p, kv_s = [[], [], []], [[], [], []]
    for i in range(depth):
        m, j = i % N_MIXERS, i // N_MIXERS
        if m == 0:
            w_in = (ssm_w_in, j)
            conv_dim = ssm_conv_w.shape[2]
            d_inner = ssm_w_out.shape[1]
            wts = (ssm_conv_w[j], ssm_conv_b[j], ssm_dt_bias[j], ssm_a_log[j], ssm_d[j], ssm_norm[j])
            conv0 = jnp.zeros((bp, SSM_CONV - 1, conv_dim), F32)
            zx_p, tails = _ssd_in_proj(xp, norm_mix[i], w_in, conv0, ssm_conv_w[j], ssm_conv_b[j],
                                       tm=tm_p, seq_len=lp, d_inner=d_inner)
            zx_p = zx_p.reshape(bp, lp, -1)
            zx_s = _norm_matmul(xs, norm_mix[i], w_in, tm=tm_s).reshape(bs, ls, -1)
            t_p = min(SSM_CHUNK, lp)
            y_p, h_p = _ssd(zx_p, conv0, jnp.zeros((bp,) + state_ssm.shape[2:], F32), *wts,
                            t=t_p, valid=t_p, conv_done=True)
            y_s, ssm_s = _ssd(_pad_rows(zx_s, SAMPLE_PAD), state_ssm_conv[j], state_ssm, *wts,
                              t=SAMPLE_PAD, valid=ls, stacked=(j, ssm_s))
            y_s = y_s[:, :ls]
            xbc_s = zx_s[:, :, d_inner:d_inner + conv_dim]
            conv_p.append(tails.reshape(bp, lp // tm_p, SUBLANES, conv_dim)[:, -1, -(SSM_CONV - 1):])
            conv_s.append(jnp.concatenate([state_ssm_conv[j], xbc_s], axis=1)[:, -(SSM_CONV - 1):])
            ssm_p.append(h_p)
            w_out = (ssm_w_out, j)
            mix_p, mix_s = y_p.reshape(mp, -1), y_s.reshape(ms, -1)
        elif m == 1:
            w_in = (gla_w_in, j)
            wts = (gla_w_gate[j], gla_gate_bias[j], gla_norm[j])
            pr_p = _norm_matmul(xp, norm_mix[i], w_in, tm=tm_wide).reshape(bp, lp, -1)
            pr_s = _norm_matmul(xs, norm_mix[i], w_in, tm=tm_s).reshape(bs, ls, -1)
            ch_p = min(GLA_CHUNK, lp)
            tb_p = 256 if lp % 256 == 0 else ch_p
            o_p, s_p = _gla(pr_p, jnp.zeros((bp,) + state_gla.shape[2:], F32), *wts,
                            tb=tb_p, ch=ch_p, valid=tb_p)
            o_s, s_s = _gla(_pad_rows(pr_s, SAMPLE_PAD), state_gla[j], *wts,
                            tb=SAMPLE_PAD, ch=SAMPLE_PAD, valid=ls)
            o_s = o_s[:, :ls]
            gla_p.append(s_p)
            gla_s.append(s_s)
            w_out = (gla_w_out, j)
            mix_p, mix_s = o_p.reshape(mp, -1), o_s.reshape(ms, -1)
        else:
            w_qkv = (att_w_qkv, j)
            qkv_p = _norm_matmul(xp, norm_mix[i], w_qkv, tm=tm_wide, rope=(2 * n_att, rope_p))
            qkv_s = _norm_matmul(xs, norm_mix[i], w_qkv, tm=tm_s, rope=(2 * n_att, rope_s))
            n_slab = n_att // LANES
            spg = hg_w // LANES
            qkv_p4 = qkv_p.reshape(3 * n_slab, bp, lp, LANES)
            os_p, ls_p, os_s, ls_s = [], [], [], []
            for gi, (window, dilation) in enumerate(ATT_GROUPS):
                o, lse = _attn_prompt(qkv_p, gi, b=bp, l=lp, window=window, dilation=dilation)
                os_p.append(o)
                ls_p.append(lse)
                keep = min(window, lp)
                kv = jnp.stack([qkv_p4[part * n_slab + gi * spg:part * n_slab + (gi + 1) * spg, :, lp - keep:]
                                for part in (1, 2)])
                kv = kv.reshape(2, spg, bp, keep, LANES // ATT_HEAD_DIM, ATT_HEAD_DIM)
                kv_p[gi].append(kv.transpose(2, 3, 0, 1, 4, 5)
                                .reshape(bp, keep, 2, ATT_HEADS_PER_GROUP, ATT_HEAD_DIM))
                cache_t = att_caches[gi][j].transpose(0, 2, 3, 4, 1)
                o, lse, cache_new = _attn_sample(qkv_s, cache_t, gi, window=window, dilation=dilation, n_new=ls)
                os_s.append(o)
                ls_s.append(lse)
                kv_s[gi].append(cache_new.transpose(0, 4, 1, 2, 3))
            w_out = (att_w_out, j)
            mix_p, mix_s = (os_p, ls_p), (os_s, ls_s)
        final_g = norm_final if i == depth - 1 else None
        xp = _block_tail(mix_p, w_out, xp, norm_ffn[i], *ffn_w, i, tm=tm_p, final_g=final_g)
        xs = _block_tail(mix_s, w_out, xs, norm_ffn[i], *ffn_w, i, tm=tm_s, final_g=final_g)
    y_prompt = xp.reshape(bp, lp, d)
    y_sample = xs.reshape(bs, ls, d)
    return (y_prompt, y_sample,
            jnp.stack(ssm_p), ssm_s, jnp.stack(conv_p), jnp.stack(conv_s),
            jnp.stack(gla_p), jnp.stack(gla_s),
            jnp.stack(kv_p[0]), jnp.stack(kv_s[0]), jnp.stack(kv_p[1]), jnp.stack(kv_s[1]),
            jnp.stack(kv_p[2]), jnp.stack(kv_s[2]))
```

```python
import functools

import jax
import jax.numpy as jnp
from jax import lax
from jax.experimental import pallas as pl
from jax.experimental.pallas import tpu as pltpu

F32 = jnp.float32
BF16 = jnp.bfloat16
LOG2_E = 1.4426950408889634

NORM_EPS = 1e-6
N_MIXERS = 3

SSM_HEAD_DIM = 64
SSM_N_GROUPS = 4
SSM_D_STATE = 128
SSM_CONV = 4
SSM_CHUNK = 128
SSM_IN_PROJ_CHUNKS = 4

GLA_N_HEADS = 4
GLA_GATE_NORM = 16.0
GLA_CHUNK = 32

ATT_GROUPS = ((128, 1), (512, 4), (2048, 16))
ATT_HEADS_PER_GROUP = 4
ATT_HEAD_DIM = 64
ATT_ROT_DIM = ATT_HEAD_DIM // 4
ROPE_THETA = 500000.0
ATT_BLOCK = 128
ATT_SUPER_ROWS = 2048
PAST_LEN = 8192

LANES = 128
SUBLANES = 8
SAMPLE_PAD = 16
VMEM_LIMIT = 56 * 1024 * 1024

_NT = (((1,), (1,)), ((), ()))
_TN = (((0,), (0,)), ((), ()))


def _cparams(*sem):
    return pltpu.CompilerParams(dimension_semantics=sem, vmem_limit_bytes=VMEM_LIMIT)


def _resident(shape):
    zeros = (0,) * len(shape)
    return pl.BlockSpec(shape, lambda *_: zeros, pipeline_mode=pl.Buffered(1))


def _layer(wl):
    w, layer = wl
    return pl.BlockSpec((None,) + w.shape[1:], lambda *_: (layer, 0, 0), pipeline_mode=pl.Buffered(1))


def _rms(x, g):
    inv = lax.rsqrt(jnp.mean(x * x, axis=-1, keepdims=True) + NORM_EPS)
    return x * inv * g


def _sigmoid(x):
    return 1.0 / (1.0 + jnp.exp(-x))


def _softplus(x):
    return jnp.maximum(x, 0.0) + jnp.log1p(jnp.exp(-jnp.abs(x)))


def _split3(v):
    hi = v.astype(BF16)
    r = v - hi.astype(F32)
    mid = r.astype(BF16)
    lo = (r - mid.astype(F32)).astype(BF16)
    return hi, mid, lo


def _dot(a, b):
    return jnp.dot(a, b, preferred_element_type=F32)


def _spread_matrix(n_blocks, width):
    k_dim = -(-3 * n_blocks // LANES) * LANES
    row = lax.broadcasted_iota(jnp.int32, (k_dim, n_blocks * width), 0)
    col = lax.broadcasted_iota(jnp.int32, (k_dim, n_blocks * width), 1)
    blk = row - jnp.where(row >= 2 * n_blocks, 2 * n_blocks, jnp.where(row >= n_blocks, n_blocks, 0))
    sel = (row < 3 * n_blocks) & (col >= blk * width) & (col < blk * width + width)
    return jnp.where(sel, 1.0, 0.0).astype(BF16)


def _spread(v, sel):
    rows, n_blocks = v.shape
    terms = [p.astype(F32) for p in _split3(v)]
    pad = jnp.zeros((rows, sel.shape[0] - 3 * n_blocks), F32)
    return _dot(jnp.concatenate(terms + [pad], axis=1).astype(BF16), sel)


def _sel_left(e, v):
    hi, mid, lo = _split3(v)
    return _dot(e, hi) + _dot(e, mid) + _dot(e, lo)


def _norm_matmul_kernel(x_ref, g_ref, w_ref, o_ref):
    h = _rms(x_ref[...], g_ref[...]).astype(BF16)
    o_ref[...] = _dot(h, w_ref[...])


def _norm_matmul_rope_kernel(x_ref, g_ref, w_ref, c_ref, s1_ref, s2_ref, o_ref, *, n_rot):
    h = _rms(x_ref[...], g_ref[...]).astype(BF16)
    res = _dot(h, w_ref[...])
    c, s1, s2 = c_ref[...], s1_ref[...], s2_ref[...]
    half = ATT_ROT_DIM // 2
    for j in range(o_ref.shape[0]):
        x = res[:, j * LANES:(j + 1) * LANES]
        if j < n_rot // LANES:
            x = x * c + pltpu.roll(x, half, 1) * s1 + pltpu.roll(x, LANES - half, 1) * s2
        o_ref[j] = x


def _norm_matmul(x, g, wl, *, tm, rope=None):
    m, d = x.shape
    n = wl[0].shape[2]
    assert m % tm == 0
    in_specs = [pl.BlockSpec((tm, d), lambda i: (i, 0)), _resident((1, d)), _layer(wl)]
    args = [x, g.reshape(1, d), wl[0]]
    if rope is None:
        body = _norm_matmul_kernel
        out_spec = pl.BlockSpec((tm, n), lambda i: (i, 0))
        out_shape = jax.ShapeDtypeStruct((m, n), F32)
    else:
        n_rot, tables = rope
        assert n % LANES == 0 and n_rot % LANES == 0
        body = functools.partial(_norm_matmul_rope_kernel, n_rot=n_rot)
        period = tables[0].shape[0] // tm
        assert tables[0].shape[0] % tm == 0 and (m // tm) % period == 0
        in_specs += [pl.BlockSpec((tm, LANES), lambda i: (i % period, 0))] * 3
        args += list(tables)
        out_spec = pl.BlockSpec((n // LANES, tm, LANES), lambda i: (0, i, 0))
        out_shape = jax.ShapeDtypeStruct((n // LANES, m, LANES), F32)
    return pl.pallas_call(
        body,
        grid=(m // tm,),
        in_specs=in_specs,
        out_specs=out_spec,
        out_shape=out_shape,
        compiler_params=_cparams("parallel"),
        name="norm_proj" if rope is None else "norm_proj_rope",
    )(*args)


def _conv_silu(raw, before, cw, cb):
    t, c = raw.shape
    groups = [before] + [raw[r:r + SUBLANES] for r in range(0, t, SUBLANES)]
    row8 = lax.broadcasted_iota(jnp.int32, (SUBLANES, c), 0)
    acc = cb
    for k in range(SSM_CONV - 1, 0, -1):
        shifted = jnp.concatenate([pltpu.roll(jnp.where(row8 >= SUBLANES - k, prev, here), k, 0)
                                   for prev, here in zip(groups[:-1], groups[1:])], axis=0)
        acc = acc + shifted * cw[SSM_CONV - 1 - k:SSM_CONV - k, :]
    acc = acc + raw * cw[SSM_CONV - 1:SSM_CONV, :]
    return acc * _sigmoid(acc), groups[-1]


def _ssd_in_proj_kernel(x_ref, g_ref, w_ref, cbuf_ref, cw_ref, cb_ref, o_ref, tail_ref, tail,
                        *, d_inner, conv_dim, tiles_per_seq, col_chunk):
    @pl.when(pl.program_id(0) % tiles_per_seq == 0)
    def _start_of_sequence():
        tail[...] = cbuf_ref[0]

    h = _rms(x_ref[...], g_ref[...]).astype(BF16)
    n_chunks = conv_dim // col_chunk
    z_chunk = d_inner // n_chunks
    for j in range(n_chunks):
        c0 = j * col_chunk
        cs = slice(c0, c0 + col_chunk)
        raw = _dot(h, w_ref[:, d_inner + c0:d_inner + c0 + col_chunk])
        zs = slice(j * z_chunk, (j + 1) * z_chunk)
        o_ref[:, zs] = _dot(h, w_ref[:, zs])
        act, last = _conv_silu(raw, tail[:, cs], cw_ref[:, cs], cb_ref[:, cs])
        o_ref[:, d_inner + c0:d_inner + c0 + col_chunk] = act
        tail[:, cs] = last
        tail_ref[0, :, cs] = last
    o_ref[:, d_inner + conv_dim:] = _dot(h, w_ref[:, d_inner + conv_dim:])


def _ssd_in_proj(x, g, wl, conv_buf, conv_w, conv_b, *, tm, seq_len, d_inner):
    m, d = x.shape
    n = wl[0].shape[2]
    conv_dim = conv_w.shape[1]
    assert m % tm == 0 and seq_len % tm == 0
    cbuf = jnp.pad(conv_buf, ((0, 0), (SUBLANES - (SSM_CONV - 1), 0), (0, 0)))
    tiles_per_seq = seq_len // tm
    body = functools.partial(_ssd_in_proj_kernel, d_inner=d_inner, conv_dim=conv_dim,
                             tiles_per_seq=tiles_per_seq, col_chunk=conv_dim // SSM_IN_PROJ_CHUNKS)
    assert conv_dim % (SSM_IN_PROJ_CHUNKS * LANES) == 0 and d_inner % (SSM_IN_PROJ_CHUNKS * LANES) == 0
    return pl.pallas_call(
        body,
        grid=(m // tm,),
        in_specs=[pl.BlockSpec((tm, d), lambda i: (i, 0)), _resident((1, d)), _layer(wl),
                  pl.BlockSpec((1, SUBLANES, conv_dim), lambda i: (i // tiles_per_seq, 0, 0)),
                  _resident((SSM_CONV, conv_dim)), _resident((1, conv_dim))],
        out_specs=[pl.BlockSpec((tm, n), lambda i: (i, 0)),
                   pl.BlockSpec((1, SUBLANES, conv_dim), lambda i: (i, 0, 0))],
        out_shape=[jax.ShapeDtypeStruct((m, n), F32),
                   jax.ShapeDtypeStruct((m // tm, SUBLANES, conv_dim), F32)],
        scratch_shapes=[pltpu.VMEM((SUBLANES, conv_dim), F32)],
        compiler_params=_cparams("arbitrary"),
        name="ssd_in_proj",
    )(x, g.reshape(1, d), wl[0], cbuf, conv_w, conv_b.reshape(1, -1))


def _mixed_heads(o_refs, l_refs):
    spg = o_refs[0].shape[0]
    pieces = [[None] * spg for _ in o_refs]
    for s in range(spg):
        ls = [l_ref[s] for l_ref in l_refs]
        mx = functools.reduce(jnp.maximum, ls)
        es = [jnp.exp(l - mx) for l in ls]
        den = functools.reduce(lambda a, b: a + b, es)
        for grp, (o_ref, e) in enumerate(zip(o_refs, es)):
            pieces[grp][s] = o_ref[s] * (e / den)
    return jnp.concatenate([p for grp in pieces for p in grp], axis=-1).astype(BF16)


def _block_tail_kernel(*refs, n_groups, final_norm):
    n_mix = 2 * n_groups if n_groups else 1
    wo_ref, x_ref, g_ref, wg_ref, wu_ref, wd_ref, gf_ref, o_ref = refs[n_mix:]
    y = _mixed_heads(refs[:n_groups], refs[n_groups:n_mix]) if n_groups else refs[0][...]
    x = x_ref[...] + _dot(y, wo_ref[...])
    h = _rms(x, g_ref[...]).astype(BF16)
    gate = _dot(h, wg_ref[...])
    up = _dot(h, wu_ref[...])
    act = (gate * _sigmoid(gate) * up).astype(BF16)
    x = x + _dot(act, wd_ref[...])
    o_ref[...] = _rms(x, gf_ref[...]) if final_norm else x


def _block_tail(mix, wol, x, g, wg, wu, wd, layer, *, tm, final_g=None):
    m, d = x.shape
    assert m % tm == 0
    gf = g if final_g is None else final_g
    if isinstance(mix, tuple):
        os_, lses = mix
        n_groups = len(os_)
        mix_args = [*os_, *lses]
        mix_specs = [pl.BlockSpec((os_[0].shape[0], tm, LANES), lambda i: (0, i, 0))] * (2 * n_groups)
    else:
        n_groups = 0
        mix_args = [mix]
        mix_specs = [pl.BlockSpec((tm, mix.shape[1]), lambda i: (i, 0))]
    return pl.pallas_call(
        functools.partial(_block_tail_kernel, n_groups=n_groups, final_norm=final_g is not None),
        grid=(m // tm,),
        in_specs=mix_specs + [_layer(wol), pl.BlockSpec((tm, d), lambda i: (i, 0)), _resident((1, d)),
                              _layer((wg, layer)), _layer((wu, layer)), _layer((wd, layer)),
                              _resident((1, d))],
        out_specs=pl.BlockSpec((tm, d), lambda i: (i, 0)),
        out_shape=jax.ShapeDtypeStruct((m, d), F32),
        compiler_params=_cparams("parallel"),
        name="proj_swiglu",
    )(*mix_args, wol[0], x, g.reshape(1, d), wg, wu, wd, gf.reshape(1, d))


def _ssd_kernel(zx_ref, cbuf_ref, h0_ref, cw_ref, cb_ref, dtb_ref, alog_ref, dsk_ref, nw_ref,
                tri_ref, triu_ref, sel_hp_ref, *rest,
                t, valid, d_inner, n_heads, conv_done, native):
    y_ref, hfin_ref, tail, state, yacc = rest[-5:]
    c = pl.program_id(1)
    n_grp = SSM_N_GROUPS
    d_st = SSM_D_STATE
    hp = d_inner // n_grp
    hpg = n_heads // n_grp
    pdim = SSM_HEAD_DIM
    bc_w = n_grp * d_st
    conv_dim = d_inner + 2 * bc_w

    @pl.when(c == 0)
    def _init():
        state[...] = h0_ref[0]
        tail[...] = cbuf_ref[0]

    xbc = zx_ref[0, :, d_inner:d_inner + conv_dim]
    if not conv_done:
        xbc, tail[...] = _conv_silu(xbc, tail[...], cw_ref[...], cb_ref[...])
    xs = xbc[:, :d_inner]
    bm = xbc[:, d_inner:d_inner + bc_w]
    cm = xbc[:, d_inner + bc_w:]

    dt = _softplus(zx_ref[0, :, d_inner + conv_dim:] + dtb_ref[...])
    if valid < t:
        dt = jnp.where(lax.broadcasted_iota(jnp.int32, dt.shape, 0) < valid, dt, 0.0)
    a = dt * (-jnp.exp(alog_ref[...]))

    iq = lax.broadcasted_iota(jnp.int32, (t, t), 0)
    ik = lax.broadcasted_iota(jnp.int32, (t, t), 1)
    low = iq >= ik
    a_cs = _sel_left(tri_ref[...], a)
    a_cs_t = sum(lax.dot_general(part, triu_ref[...], _TN, preferred_element_type=F32)
                 for part in _split3(a))

    a_last = a_cs[t - 1:t, :]
    pad_rows = 2 * SUBLANES
    stack = jnp.concatenate([dt, dt * jnp.exp(a_last - a_cs), jnp.exp(a_cs),
                             jnp.broadcast_to(jnp.exp(a_last), (pad_rows, n_heads))], axis=0)
    ex = _spread(stack, sel_hp_ref[...])
    xdt_b = (xs * ex[0:t]).astype(BF16)
    xdst_b = (xs * ex[t:2 * t]).astype(BF16)
    from_start = ex[2 * t:3 * t]
    chunk_decay = ex[3 * t:3 * t + pad_rows]
    if native:
        only0 = jnp.where(lax.broadcasted_iota(jnp.int32, chunk_decay.shape, 0) == 0, chunk_decay, 0.0)
        terms = jnp.concatenate(_split3(only0), axis=0)
        decay_rows = lax.dot_general(terms, jnp.ones((terms.shape[0], d_st), BF16), _TN,
                                     preferred_element_type=F32)

    a_cs2 = a_cs * LOG2_E
    a_row = a_cs_t * LOG2_E

    pair = LANES // pdim
    for g in range(n_grp):
        bg = bm[:, g * d_st:(g + 1) * d_st].astype(BF16)
        cg = cm[:, g * d_st:(g + 1) * d_st].astype(BF16)
        cb = lax.dot_general(cg, bg, _NT, preferred_element_type=F32)
        sg = state[g]
        sg_b = sg.astype(BF16)
        gl = slice(g * hp, (g + 1) * hp)
        carried = lax.dot_general(cg, sg_b, _NT, preferred_element_type=F32) if native else _dot(cg, sg_b)
        yacc[:, gl] = carried * from_start[:, gl]
        for h0 in range(0, hpg, pair):
            outs = []
            for hh in range(g * hpg + h0, g * hpg + h0 + pair):
                ps = slice(hh * pdim, (hh + 1) * pdim)
                decay = jnp.exp2(jnp.where(low, a_cs2[:, hh:hh + 1] - a_row[hh:hh + 1, :], -jnp.inf))
                outs.append(_dot((cb * decay).astype(BF16), xdt_b[:, ps]))
            lo = (g * hpg + h0) * pdim
            yacc[:, lo:lo + pair * pdim] += jnp.concatenate(outs, axis=1)
        if native:
            upd = lax.dot_general(xdst_b[:, gl], bg, _TN, preferred_element_type=F32)
            state[g] = sg * decay_rows[gl, :] + upd
        else:
            upd = lax.dot_general(bg, xdst_b[:, gl], _TN, preferred_element_type=F32)
            state[g] = sg * chunk_decay[0:1, gl] + upd

    z = zx_ref[0, :, :d_inner]
    y = (yacc[...] + xs * dsk_ref[...]) * (z * _sigmoid(z))
    nw = nw_ref[...]
    for g in range(n_grp):
        gl = slice(g * hp, (g + 1) * hp)
        y_ref[0, :, gl] = _rms(y[:, gl], nw[:, gl]).astype(y_ref.dtype)

    @pl.when(c == pl.num_programs(1) - 1)
    def _fin():
        hfin_ref[0] = state[...]


def _ssd(zx, conv_buf, h0, conv_w, conv_b, dt_bias, a_log, d_skip, norm_w, *, t, valid, conv_done=False,
         stacked=None):
    b, l, in_dim = zx.shape
    n_heads = a_log.shape[0]
    d_inner = n_heads * SSM_HEAD_DIM
    conv_dim = conv_w.shape[1]
    n_grp, d_st = SSM_N_GROUPS, SSM_D_STATE
    hp = d_inner // n_grp
    assert l % t == 0 and in_dim == d_inner + conv_dim + n_heads
    cbuf = jnp.pad(conv_buf, ((0, 0), (SUBLANES - (SSM_CONV - 1), 0), (0, 0)))
    native = stacked is not None
    extra_in, extra_specs, aliases = [], [], {}
    if native:
        layer, out_buf = stacked
        state_shape = (n_grp, hp, d_st)
        h0k = h0.reshape(h0.shape[0], b, *state_shape)
        state_spec = pl.BlockSpec((None, 1) + state_shape, lambda i, c: (layer, i, 0, 0, 0))
        state_out = jax.ShapeDtypeStruct(h0k.shape, F32)
        extra_in, extra_specs = [out_buf.reshape(h0k.shape)], [pl.BlockSpec(memory_space=pl.ANY)]
    else:
        state_shape = (n_grp, d_st, hp)
        h0k = h0.reshape(b, n_grp, n_heads // n_grp, SSM_HEAD_DIM, d_st)
        h0k = h0k.transpose(0, 1, 4, 2, 3).reshape(b, *state_shape)
        state_spec = pl.BlockSpec((1,) + state_shape, lambda i, c: (i, 0, 0, 0))
        state_out = jax.ShapeDtypeStruct(h0k.shape, F32)
    body = functools.partial(_ssd_kernel, t=t, valid=valid, d_inner=d_inner, n_heads=n_heads,
                             conv_done=conv_done, native=native)
    tri = jnp.tril(jnp.ones((t, t), BF16))
    consts = (tri, tri.T, _spread_matrix(n_heads, SSM_HEAD_DIM))
    operands = [zx, cbuf, h0k, conv_w, conv_b.reshape(1, -1), dt_bias.reshape(1, -1), a_log.reshape(1, -1),
                jnp.repeat(d_skip, SSM_HEAD_DIM).reshape(1, -1), norm_w.reshape(1, -1), *consts]
    if extra_in:
        aliases = {len(operands): 1}
    y, hfin = pl.pallas_call(
        body,
        grid=(b, l // t),
        in_specs=[
            pl.BlockSpec((1, t, in_dim), lambda i, c: (i, c, 0)),
            pl.BlockSpec((1, SUBLANES, conv_dim), lambda i, c: (i, 0, 0)),
            state_spec,
            _resident((SSM_CONV, conv_dim)), _resident((1, conv_dim)),
            _resident((1, n_heads)), _resident((1, n_heads)),
            _resident((1, d_inner)), _resident((1, d_inner)),
        ] + [_resident(cst.shape) for cst in consts] + extra_specs,
        out_specs=[pl.BlockSpec((1, t, d_inner), lambda i, c: (i, c, 0)), state_spec],
        out_shape=[jax.ShapeDtypeStruct((b, l, d_inner), BF16), state_out],
        scratch_shapes=[
            pltpu.VMEM((SUBLANES, conv_dim), F32),
            pltpu.VMEM(state_shape, F32),
            pltpu.VMEM((t, d_inner), F32),
        ],
        input_output_aliases=aliases,
        compiler_params=_cparams("parallel", "arbitrary"),
        name="ssd_scan",
    )(*operands, *extra_in)
    if native:
        return y, hfin.reshape(h0.shape)
    hfin = hfin.reshape(b, n_grp, d_st, n_heads // n_grp, SSM_HEAD_DIM)
    hfin = hfin.transpose(0, 1, 3, 4, 2).reshape(b, n_heads, SSM_HEAD_DIM, d_st)
    return y, hfin


def _gla_kernel(p_ref, s0_ref, wg_ref, gb_ref, nw_ref, tri_ref, o_ref, sfin_ref, state,
                *, tb, ch, valid, dk, dv):
    c = pl.program_id(1)
    n_h = GLA_N_HEADS
    hk, hv = dk // n_h, dv // n_h

    @pl.when(c == 0)
    def _init():
        state[...] = s0_ref[0]

    q = p_ref[0, :, 0:dk] * (hk ** -0.5)
    k = p_ref[0, :, dk:2 * dk]
    v = p_ref[0, :, 2 * dk:2 * dk + dv]
    r = p_ref[0, :, 2 * dk + dv:2 * dk + 2 * dv]
    g_low = p_ref[0, :, 2 * dk + 2 * dv:]
    x = _dot(g_low.astype(BF16), wg_ref[...]) + gb_ref[...]
    log_a = (jnp.minimum(x, 0.0) - jnp.log(1.0 + jnp.exp(-jnp.abs(x)))) / GLA_GATE_NORM
    if valid < tb:
        keep = lax.broadcasted_iota(jnp.int32, (tb, dk), 0) < valid
        log_a = jnp.where(keep, log_a, 0.0)
        k = jnp.where(keep, k, 0.0)

    bcum = _sel_left(tri_ref[...], log_a)
    nch = tb // ch

    def chunks(a):
        return a.reshape(nch, ch, a.shape[-1])

    bc3 = chunks(bcum)
    b_last = bc3[:, ch - 1:ch, :]
    q_t = chunks(q * jnp.exp(bcum)).astype(BF16)
    k_t = chunks(k * jnp.exp(-bcum)).astype(BF16)
    k_dec = (chunks(k) * jnp.exp(b_last - bc3)).astype(BF16)
    e_last = jnp.exp(b_last)
    v_b = chunks(v).astype(BF16)
    causal = (lax.broadcasted_iota(jnp.int32, (nch, ch, ch), 1)
              >= lax.broadcasted_iota(jnp.int32, (nch, ch, ch), 2))

    gate = r * _sigmoid(r)
    nw = nw_ref[...]
    for h in range(n_h):
        ks = slice(h * hk, (h + 1) * hk)
        vs = slice(h * hv, (h + 1) * hv)
        qh, kh, vh = q_t[:, :, ks], k_t[:, :, ks], v_b[:, :, vs]
        upd = jnp.einsum("jkv,jkd->jvd", vh, k_dec[:, :, ks], preferred_element_type=F32)
        s_run = state[h]
        entering = []
        for j in range(nch):
            entering.append(s_run.astype(BF16))
            s_run = s_run * e_last[j, :, ks] + upd[j]
        state[h] = s_run
        att = jnp.einsum("jqd,jkd->jqk", qh, kh, preferred_element_type=F32)
        att = jnp.where(causal, att, 0.0).astype(BF16)
        o = (jnp.einsum("jqk,jkv->jqv", att, vh, preferred_element_type=F32)
             + jnp.einsum("jqd,jvd->jqv", qh, jnp.stack(entering), preferred_element_type=F32))
        o_ref[0, :, vs] = (_rms(o.reshape(tb, hv), nw) * gate[:, vs]).astype(o_ref.dtype)

    @pl.when(c == pl.num_programs(1) - 1)
    def _fin():
        sfin_ref[0] = state[...]


def _gla(proj, s0, w_gate, gate_bias, norm_w, *, tb, ch, valid):
    b, l, in_dim = proj.shape
    n_h = GLA_N_HEADS
    rank, dk = w_gate.shape
    hv = norm_w.shape[0]
    dv = hv * n_h
    hk = dk // n_h
    assert l % tb == 0 and tb % ch == 0 and in_dim == 2 * dk + 2 * dv + rank
    body = functools.partial(_gla_kernel, tb=tb, ch=ch, valid=valid, dk=dk, dv=dv)
    blocktri = jnp.kron(jnp.eye(tb // ch, dtype=F32), jnp.tril(jnp.ones((ch, ch), F32))).astype(BF16)
    o, sfin = pl.pallas_call(
        body,
        grid=(b, l // tb),
        in_specs=[
            pl.BlockSpec((1, tb, in_dim), lambda i, c: (i, c, 0)),
            pl.BlockSpec((1, n_h, hv, hk), lambda i, c: (i, 0, 0, 0)),
            _resident((rank, dk)), _resident((1, dk)), _resident((1, hv)), _resident((tb, tb)),
        ],
        out_specs=[
            pl.BlockSpec((1, tb, dv), lambda i, c: (i, c, 0)),
            pl.BlockSpec((1, n_h, hv, hk), lambda i, c: (i, 0, 0, 0)),
        ],
        out_shape=[
            jax.ShapeDtypeStruct((b, l, dv), BF16),
            jax.ShapeDtypeStruct((b, n_h, hv, hk), F32),
        ],
        scratch_shapes=[pltpu.VMEM((n_h, hv, hk), F32)],
        compiler_params=_cparams("parallel", "arbitrary"),
        name="gla_scan",
    )(proj, jnp.swapaxes(s0, -1, -2), w_gate.astype(BF16), gate_bias.reshape(1, -1),
      norm_w.reshape(1, -1), blocktri)
    return o, jnp.swapaxes(sfin, -1, -2)


def _attn_prompt_kernel(q_ref, kp_ref, kc_ref, vp_ref, vc_ref, o_ref, lse_ref, *, dilation, nb):
    not_first = pl.program_id(2) > 0
    blk = ATT_BLOCK
    dh = ATT_HEAD_DIM
    units = [(r, jb) for r in range(dilation) for jb in range(nb)]
    n_u = len(units)
    iu = lax.broadcasted_iota(jnp.int32, (n_u, blk, blk), 0)
    iq = lax.broadcasted_iota(jnp.int32, (n_u, blk, blk), 1)
    ik = lax.broadcasted_iota(jnp.int32, (n_u, blk, blk), 2)
    ok_prev = (ik >= iq) & (((iu & (nb - 1)) != 0) | not_first)
    ok_cur = ik <= iq
    scale = dh ** -0.5

    def rows(r, jb):
        start = r + dilation * jb * blk
        return pl.ds(start, blk) if dilation == 1 else pl.ds(start, blk, stride=dilation)

    def gather(cur_ref, prev_ref):
        cur = jnp.stack([cur_ref[0, rows(r, jb), :] for r, jb in units])
        prev = jnp.stack([prev_ref[0, rows(r, nb - 1), :] if jb == 0 else cur_ref[0, rows(r, jb - 1), :]
                          for r, jb in units])
        return cur.astype(BF16), prev.astype(BF16)

    q = (jnp.stack([q_ref[0, rows(r, jb), :] for r, jb in units]) * scale).astype(BF16)
    k_cur, k_prev = gather(kc_ref, kp_ref)
    v_cur, v_prev = gather(vc_ref, vp_ref)
    outs, lses = [], []
    for hh in range(LANES // dh):
        hs = slice(hh * dh, (hh + 1) * dh)
        qh = q[:, :, hs]
        s_p = jnp.einsum("uqd,ukd->uqk", qh, k_prev[:, :, hs], preferred_element_type=F32)
        s_c = jnp.einsum("uqd,ukd->uqk", qh, k_cur[:, :, hs], preferred_element_type=F32)
        s_p = jnp.where(ok_prev, s_p, -jnp.inf)
        s_c = jnp.where(ok_cur, s_c, -jnp.inf)
        mx = jnp.max(jnp.maximum(s_p, s_c), axis=-1, keepdims=True)
        p_p = jnp.exp(s_p - mx)
        p_c = jnp.exp(s_c - mx)
        den = jnp.sum(p_p + p_c, axis=-1, keepdims=True)
        outs.append(jnp.einsum("uqk,ukd->uqd", (p_p / den).astype(BF16), v_prev[:, :, hs],
                               preferred_element_type=F32)
                    + jnp.einsum("uqk,ukd->uqd", (p_c / den).astype(BF16), v_cur[:, :, hs],
                                 preferred_element_type=F32))
        lses.append(jnp.broadcast_to(mx + jnp.log(den), (n_u, blk, dh)))
    o = jnp.concatenate(outs, axis=-1)
    lse = jnp.concatenate(lses, axis=-1)
    for u, (r, jb) in enumerate(units):
        o_ref[0, rows(r, jb), :] = o[u]
        lse_ref[0, rows(r, jb), :] = lse[u]


def _attn_prompt(qkv, gi, *, b, l, window, dilation):
    n_slab = qkv.shape[0] // 3
    spg = ATT_HEADS_PER_GROUP * ATT_HEAD_DIM // LANES
    span = window // dilation
    blk = ATT_BLOCK
    nb = max(1, min(l, ATT_SUPER_ROWS) // (dilation * blk))
    r_rows = dilation * nb * blk
    n_sup = l // r_rows
    assert l % r_rows == 0 and span == blk and qkv.shape[1] == b * l and nb & (nb - 1) == 0

    def spec(part, prev):
        def index(bi, sp, i):
            i = jnp.maximum(i - 1, 0) if prev else i
            return (part * n_slab + gi * spg + sp, bi * n_sup + i, 0)
        return pl.BlockSpec((1, r_rows, LANES), index)

    out_spec = pl.BlockSpec((1, r_rows, LANES), lambda bi, sp, i: (sp, bi * n_sup + i, 0))
    return pl.pallas_call(
        functools.partial(_attn_prompt_kernel, dilation=dilation, nb=nb),
        grid=(b, spg, n_sup),
        in_specs=[spec(0, False), spec(1, True), spec(1, False), spec(2, True), spec(2, False)],
        out_specs=[out_spec, out_spec],
        out_shape=[jax.ShapeDtypeStruct((spg, b * l, LANES), F32)] * 2,
        compiler_params=_cparams("parallel", "parallel", "parallel"),
        name="attn_prompt",
    )(qkv, qkv, qkv, qkv, qkv)


def _attn_sample_kernel(q_ref, kn_ref, vn_ref, cache_ref, o_ref, lse_ref, cout_ref,
                        *, window, dilation, n_new, seqs):
    dh = ATT_HEAD_DIM
    n_h = ATT_HEADS_PER_GROUP
    rows = seqs * n_new
    scale = dh ** -0.5
    units = [(bb, h) for bb in range(seqs) for h in range(n_h)]
    n_u = len(units)

    def heads_of(ref):
        return jnp.stack([ref[h * dh // LANES][:, h * dh % LANES:h * dh % LANES + dh] for _, h in units])

    q = heads_of(q_ref).astype(BF16)
    k_new, v_new = heads_of(kn_ref), heads_of(vn_ref)
    k_t = cache_ref[:, 0].reshape(n_u, dh, window)
    v_t = cache_ref[:, 1].reshape(n_u, dh, window)

    row = lax.broadcasted_iota(jnp.int32, (n_u, rows, window), 1)
    col = lax.broadcasted_iota(jnp.int32, (n_u, rows, window), 2)
    sq = row & (n_new - 1)
    ok_cache = (col >= sq) & (((col - sq) & (dilation - 1)) == 0)
    row_n = lax.broadcasted_iota(jnp.int32, (n_u, rows, rows), 1)
    col_n = lax.broadcasted_iota(jnp.int32, (n_u, rows, rows), 2)
    unit = lax.broadcasted_iota(jnp.int32, (n_u, rows, rows), 0)
    first = sum(jnp.where(unit >= bb * n_h, n_new, 0) for bb in range(1, seqs))
    back = (row_n & (n_new - 1)) - (col_n & (n_new - 1))
    ok_new = (back >= 0) & ((back & (dilation - 1)) == 0) & (col_n >= first) & (col_n < first + n_new)

    s_c = jnp.einsum("urd,udw->urw", q, k_t.astype(BF16), preferred_element_type=F32) * scale
    s_n = jnp.einsum("urd,utd->urt", q, k_new.astype(BF16), preferred_element_type=F32) * scale
    s_c = jnp.where(ok_cache, s_c, -jnp.inf)
    s_n = jnp.where(ok_new, s_n, -jnp.inf)
    mx = jnp.maximum(jnp.max(s_c, axis=-1, keepdims=True), jnp.max(s_n, axis=-1, keepdims=True))
    p_c = jnp.exp(s_c - mx)
    p_n = jnp.exp(s_n - mx)
    den = jnp.sum(p_c, axis=-1, keepdims=True) + jnp.sum(p_n, axis=-1, keepdims=True)
    o = (jnp.einsum("urw,udw->urd", (p_c / den).astype(BF16), v_t.astype(BF16), preferred_element_type=F32)
         + jnp.einsum("urt,utd->urd", (p_n / den).astype(BF16), v_new.astype(BF16),
                      preferred_element_type=F32))
    lse = jnp.broadcast_to(mx + jnp.log(den), (n_u, rows, dh))

    lane_t = lax.broadcasted_iota(jnp.int32, (2 * rows, LANES), 1)
    tok_t = lax.broadcasted_iota(jnp.int32, (2 * rows, LANES), 0)
    lane_o = lax.broadcasted_iota(jnp.int32, (dh, LANES), 1)
    zpad = jnp.zeros((rows, dh), F32)
    for u, (bb, h) in enumerate(units):
        r0 = bb * n_new
        sl = h * dh // LANES
        hs = slice(h * dh % LANES, h * dh % LANES + dh)
        o_ref[sl, r0:r0 + n_new, hs] = o[u, r0:r0 + n_new]
        lse_ref[sl, r0:r0 + n_new, hs] = lse[u, r0:r0 + n_new]
        place = jnp.where(lane_t - (LANES - n_new) == tok_t - r0, 1.0, 0.0).astype(BF16)
        for kv, old, new in ((0, k_t[u], k_new[u]), (1, v_t[u], v_new[u])):
            tail = sum(lax.dot_general(part, place, _TN, preferred_element_type=F32)
                       for part in _split3(jnp.concatenate([new, zpad], axis=0)))
            shifted = pltpu.roll(old, window - n_new, 1)
            last = jnp.where(lane_o >= LANES - n_new, tail, shifted[:, window - LANES:])
            if window > LANES:
                cout_ref[bb, kv, h, :, :window - LANES] = shifted[:, :window - LANES]
            cout_ref[bb, kv, h, :, window - LANES:] = last


def _attn_sample(qkv, cache_t, gi, *, window, dilation, n_new):
    n_slab = qkv.shape[0] // 3
    spg = ATT_HEADS_PER_GROUP * ATT_HEAD_DIM // LANES
    b = cache_t.shape[0]
    seqs = SUBLANES // n_new
    rows = seqs * n_new
    assert rows == SUBLANES and b % seqs == 0 and qkv.shape[1] == b * n_new
    assert cache_t.shape[-1] == window and window % dilation == 0 and window % LANES == 0
    assert dilation & (dilation - 1) == 0 and n_new & (n_new - 1) == 0

    def slabs(part):
        return pl.BlockSpec((spg, rows, LANES), lambda i: (part * n_slab // spg + gi, i, 0))

    cache_spec = pl.BlockSpec((seqs,) + cache_t.shape[1:], lambda i: (i, 0, 0, 0, 0))
    out_spec = pl.BlockSpec((spg, rows, LANES), lambda i: (0, i, 0))
    return pl.pallas_call(
        functools.partial(_attn_sample_kernel, window=window, dilation=dilation, n_new=n_new, seqs=seqs),
        grid=(b // seqs,),
        in_specs=[slabs(0), slabs(1), slabs(2), cache_spec],
        out_specs=[out_spec, out_spec, cache_spec],
        out_shape=[jax.ShapeDtypeStruct((spg, b * n_new, LANES), F32)] * 2
        + [jax.ShapeDtypeStruct(cache_t.shape, F32)],
        compiler_params=_cparams("parallel"),
        name="attn_sample",
    )(qkv, qkv, qkv, cache_t)


def _rope_tables(pos):
    half = ATT_ROT_DIM // 2
    lane = lax.broadcasted_iota(jnp.int32, (pos.shape[0], LANES), 1) % ATT_HEAD_DIM
    x1, rot = lane < half, lane < ATT_ROT_DIM
    freq = jnp.where(x1, lane, lane - half)
    inv_freq = ROPE_THETA ** (-(2 * freq).astype(F32) / ATT_ROT_DIM)
    ang = pos.astype(F32)[:, None] * inv_freq
    cos, sin = jnp.cos(ang), jnp.sin(ang)
    c = jnp.where(rot, cos, 1.0)
    s1 = jnp.where(rot & jnp.logical_not(x1), sin, 0.0)
    s2 = jnp.where(x1, -sin, 0.0)
    return c, s1, s2


def _pad_rows(t, rows):
    return jnp.pad(t, ((0, 0), (0, rows - t.shape[1]), (0, 0)))


def kernel(x_prompt, x_sample, state_ssm, state_ssm_conv, state_gla, cache_kv_g0, cache_kv_g1, cache_kv_g2,
           norm_mix, norm_ffn, ffn_gate, ffn_up, ffn_down,
           ssm_w_in, ssm_conv_w, ssm_conv_b, ssm_dt_bias, ssm_a_log, ssm_d, ssm_norm, ssm_w_out,
           gla_w_in, gla_w_gate, gla_gate_bias, gla_norm, gla_w_out,
           att_w_qkv, att_w_out, norm_final):
    bp, lp, d = x_prompt.shape
    bs, ls, _ = x_sample.shape
    depth = norm_mix.shape[0]
    mp, ms = bp * lp, bs * ls
    tm_p = 512 if mp % 512 == 0 else mp
    tm_wide = 2 * tm_p if lp % (2 * tm_p) == 0 else tm_p
    tm_s = ms
    xp = x_prompt.reshape(mp, d)
    xs = x_sample.reshape(ms, d)
    att_caches = (cache_kv_g0, cache_kv_g1, cache_kv_g2)
    hg_w = ATT_HEADS_PER_GROUP * ATT_HEAD_DIM
    n_att = hg_w * len(ATT_GROUPS)
    rope_p = _rope_tables(jnp.arange(lp))
    rope_s = _rope_tables(jnp.tile(PAST_LEN + jnp.arange(ls), bs))
    ffn_w = tuple(w.astype(BF16) for w in (ffn_gate, ffn_up, ffn_down))
    ssm_w_in, ssm_w_out, gla_w_in, gla_w_out, att_w_qkv, att_w_out = (
        w.astype(BF16) for w in (ssm_w_in, ssm_w_out, gla_w_in, gla_w_out, att_w_qkv, att_w_out))

    ssm_p, conv_p, conv_s, gla_p, gla_s = [], [], [], [], []
    ssm_s = jnp.zeros(state_ssm.shape, F32)
    kv_p, kv_s = [[], [], []], [[], [], []]
    for i in range(depth):
        m, j = i % N_MIXERS, i // N_MIXERS
        if m == 0:
            w_in = (ssm_w_in, j)
            conv_dim = ssm_conv_w.shape[2]
            d_inner = ssm_w_out.shape[1]
            wts = (ssm_conv_w[j], ssm_conv_b[j], ssm_dt_bias[j], ssm_a_log[j], ssm_d[j], ssm_norm[j])
            conv0 = jnp.zeros((bp, SSM_CONV - 1, conv_dim), F32)
            zx_p, tails = _ssd_in_proj(xp, norm_mix[i], w_in, conv0, ssm_conv_w[j], ssm_conv_b[j],
                                       tm=tm_p, seq_len=lp, d_inner=d_inner)
            zx_p = zx_p.reshape(bp, lp, -1)
            zx_s = _norm_matmul(xs, norm_mix[i], w_in, tm=tm_s).reshape(bs, ls, -1)
            t_p = min(SSM_CHUNK, lp)
            y_p, h_p = _ssd(zx_p, conv0, jnp.zeros((bp,) + state_ssm.shape[2:], F32), *wts,
                            t=t_p, valid=t_p, conv_done=True)
            y_s, ssm_s = _ssd(_pad_rows(zx_s, SAMPLE_PAD), state_ssm_conv[j], state_ssm, *wts,
                              t=SAMPLE_PAD, valid=ls, stacked=(j, ssm_s))
            y_s = y_s[:, :ls]
            xbc_s = zx_s[:, :, d_inner:d_inner + conv_dim]
            conv_p.append(tails.reshape(bp, lp // tm_p, SUBLANES, conv_dim)[:, -1, -(SSM_CONV - 1):])
            conv_s.append(jnp.concatenate([state_ssm_conv[j], xbc_s], axis=1)[:, -(SSM_CONV - 1):])
            ssm_p.append(h_p)
            w_out = (ssm_w_out, j)
            mix_p, mix_s = y_p.reshape(mp, -1), y_s.reshape(ms, -1)
        elif m == 1:
            w_in = (gla_w_in, j)
            wts = (gla_w_gate[j], gla_gate_bias[j], gla_norm[j])
            pr_p = _norm_matmul(xp, norm_mix[i], w_in, tm=tm_wide).reshape(bp, lp, -1)
            pr_s = _norm_matmul(xs, norm_mix[i], w_in, tm=tm_s).reshape(bs, ls, -1)
            ch_p = min(GLA_CHUNK, lp)
            tb_p = 256 if lp % 256 == 0 else ch_p
            o_p, s_p = _gla(pr_p, jnp.zeros((bp,) + state_gla.shape[2:], F32), *wts,
                            tb=tb_p, ch=ch_p, valid=tb_p)
            o_s, s_s = _gla(_pad_rows(pr_s, SAMPLE_PAD), state_gla[j], *wts,
                            tb=SAMPLE_PAD, ch=SAMPLE_PAD, valid=ls)
            o_s = o_s[:, :ls]
            gla_p.append(s_p)
            gla_s.append(s_s)
            w_out = (gla_w_out, j)
            mix_p, mix_s = o_p.reshape(mp, -1), o_s.reshape(ms, -1)
        else:
            w_qkv = (att_w_qkv, j)
            qkv_p = _norm_matmul(xp, norm_mix[i], w_qkv, tm=tm_wide, rope=(2 * n_att, rope_p))
            qkv_s = _norm_matmul(xs, norm_mix[i], w_qkv, tm=tm_s, rope=(2 * n_att, rope_s))
            n_slab = n_att // LANES
            spg = hg_w // LANES
            qkv_p4 = qkv_p.reshape(3 * n_slab, bp, lp, LANES)
            os_p, ls_p, os_s, ls_s = [], [], [], []
            for gi, (window, dilation) in enumerate(ATT_GROUPS):
                o, lse = _attn_prompt(qkv_p, gi, b=bp, l=lp, window=window, dilation=dilation)
                os_p.append(o)
                ls_p.append(lse)
                keep = min(window, lp)
                kv = jnp.stack([qkv_p4[part * n_slab + gi * spg:part * n_slab + (gi + 1) * spg, :, lp - keep:]
                                for part in (1, 2)])
                kv = kv.reshape(2, spg, bp, keep, LANES // ATT_HEAD_DIM, ATT_HEAD_DIM)
                kv_p[gi].append(kv.transpose(2, 3, 0, 1, 4, 5)
                                .reshape(bp, keep, 2, ATT_HEADS_PER_GROUP, ATT_HEAD_DIM))
                cache_t = att_caches[gi][j].transpose(0, 2, 3, 4, 1)
                o, lse, cache_new = _attn_sample(qkv_s, cache_t, gi, window=window, dilation=dilation, n_new=ls)
                os_s.append(o)
                ls_s.append(lse)
                kv_s[gi].append(cache_new.transpose(0, 4, 1, 2, 3))
            w_out = (att_w_out, j)
            mix_p, mix_s = (os_p, ls_p), (os_s, ls_s)
        final_g = norm_final if i == depth - 1 else None
        xp = _block_tail(mix_p, w_out, xp, norm_ffn[i], *ffn_w, i, tm=tm_p, final_g=final_g)
        xs = _block_tail(mix_s, w_out, xs, norm_ffn[i], *ffn_w, i, tm=tm_s, final_g=final_g)
    y_prompt = xp.reshape(bp, lp, d)
    y_sample = xs.reshape(bs, ls, d)
    return (y_prompt, y_sample,
            jnp.stack(ssm_p), ssm_s, jnp.stack(conv_p), jnp.stack(conv_s),
            jnp.stack(gla_p), jnp.stack(gla_s),
            jnp.stack(kv_p[0]), jnp.stack(kv_s[0]), jnp.stack(kv_p[1]), jnp.stack(kv_s[1]),
            jnp.stack(kv_p[2]), jnp.stack(kv_s[2]))
```

```python
import functools

import jax
import jax.numpy as jnp
from jax import lax
from jax.experimental import pallas as pl
from jax.experimental.pallas import tpu as pltpu

F32 = jnp.float32
BF16 = jnp.bfloat16
LOG2_E = 1.4426950408889634

NORM_EPS = 1e-6
N_MIXERS = 3

SSM_HEAD_DIM = 64
SSM_N_GROUPS = 4
SSM_D_STATE = 128
SSM_CONV = 4
SSM_CHUNK = 128
SSM_IN_PROJ_CHUNKS = 4

GLA_N_HEADS = 4
GLA_GATE_NORM = 16.0
GLA_CHUNK = 32

ATT_GROUPS = ((128, 1), (512, 4), (2048, 16))
ATT_HEADS_PER_GROUP = 4
ATT_HEAD_DIM = 64
ATT_ROT_DIM = ATT_HEAD_DIM // 4
ROPE_THETA = 500000.0
ATT_BLOCK = 128
ATT_SUPER_ROWS = 2048
PAST_LEN = 8192

LANES = 128
SUBLANES = 8
SAMPLE_PAD = 16
V7X_VMEM_BYTES = 64 * 1024 * 1024
VMEM_LIMIT = V7X_VMEM_BYTES * 7 // 8

_NT = (((1,), (1,)), ((), ()))
_TN = (((0,), (0,)), ((), ()))


def _cparams(*sem):
    return pltpu.CompilerParams(dimension_semantics=sem, vmem_limit_bytes=VMEM_LIMIT)


def _resident(shape):
    zeros = (0,) * len(shape)
    return pl.BlockSpec(shape, lambda *_: zeros, pipeline_mode=pl.Buffered(1))


def _layer(wl):
    w, layer = wl
    return pl.BlockSpec((None,) + w.shape[1:], lambda *_: (layer, 0, 0), pipeline_mode=pl.Buffered(1))


def _rms(x, g):
    inv = lax.rsqrt(jnp.mean(x * x, axis=-1, keepdims=True) + NORM_EPS)
    return x * inv * g


def _sigmoid(x):
    return 1.0 / (1.0 + jnp.exp(-x))


def _softplus(x):
    return jnp.maximum(x, 0.0) + jnp.log1p(jnp.exp(-jnp.abs(x)))


def _split3(v):
    hi = v.astype(BF16)
    r = v - hi.astype(F32)
    mid = r.astype(BF16)
    lo = (r - mid.astype(F32)).astype(BF16)
    return hi, mid, lo


def _dot(a, b):
    return jnp.dot(a, b, preferred_element_type=F32)


def _spread_matrix(n_blocks, width):
    k_dim = -(-3 * n_blocks // LANES) * LANES
    row = lax.broadcasted_iota(jnp.int32, (k_dim, n_blocks * width), 0)
    col = lax.broadcasted_iota(jnp.int32, (k_dim, n_blocks * width), 1)
    blk = row - jnp.where(row >= 2 * n_blocks, 2 * n_blocks, jnp.where(row >= n_blocks, n_blocks, 0))
    sel = (row < 3 * n_blocks) & (col >= blk * width) & (col < blk * width + width)
    return jnp.where(sel, 1.0, 0.0).astype(BF16)


def _spread(v, sel):
    rows, n_blocks = v.shape
    terms = [p.astype(F32) for p in _split3(v)]
    pad = jnp.zeros((rows, sel.shape[0] - 3 * n_blocks), F32)
    return _dot(jnp.concatenate(terms + [pad], axis=1).astype(BF16), sel)


def _sel_left(e, v):
    hi, mid, lo = _split3(v)
    return _dot(e, hi) + _dot(e, mid) + _dot(e, lo)


def _norm_matmul_kernel(x_ref, g_ref, w_ref, o_ref):
    h = _rms(x_ref[...], g_ref[...]).astype(BF16)
    o_ref[...] = _dot(h, w_ref[...])


def _norm_matmul_rope_kernel(x_ref, g_ref, w_ref, c_ref, s1_ref, s2_ref, o_ref, *, n_rot):
    h = _rms(x_ref[...], g_ref[...]).astype(BF16)
    res = _dot(h, w_ref[...])
    c, s1, s2 = c_ref[...], s1_ref[...], s2_ref[...]
    half = ATT_ROT_DIM // 2
    for j in range(o_ref.shape[0]):
        x = res[:, j * LANES:(j + 1) * LANES]
        if j < n_rot // LANES:
            x = x * c + pltpu.roll(x, half, 1) * s1 + pltpu.roll(x, LANES - half, 1) * s2
        o_ref[j] = x


def _norm_matmul(x, g, wl, *, tm, rope=None):
    m, d = x.shape
    n = wl[0].shape[2]
    assert m % tm == 0
    in_specs = [pl.BlockSpec((tm, d), lambda i: (i, 0)), _resident((1, d)), _layer(wl)]
    args = [x, g.reshape(1, d), wl[0]]
    if rope is None:
        body = _norm_matmul_kernel
        out_spec = pl.BlockSpec((tm, n), lambda i: (i, 0))
        out_shape = jax.ShapeDtypeStruct((m, n), F32)
    else:
        n_rot, tables = rope
        assert n % LANES == 0 and n_rot % LANES == 0
        body = functools.partial(_norm_matmul_rope_kernel, n_rot=n_rot)
        period = tables[0].shape[0] // tm
        assert tables[0].shape[0] % tm == 0 and (m // tm) % period == 0
        in_specs += [pl.BlockSpec((tm, LANES), lambda i: (i % period, 0))] * 3
        args += list(tables)
        out_spec = pl.BlockSpec((n // LANES, tm, LANES), lambda i: (0, i, 0))
        out_shape = jax.ShapeDtypeStruct((n // LANES, m, LANES), F32)
    return pl.pallas_call(
        body,
        grid=(m // tm,),
        in_specs=in_specs,
        out_specs=out_spec,
        out_shape=out_shape,
        compiler_params=_cparams("parallel"),
        name="norm_proj" if rope is None else "norm_proj_rope",
    )(*args)


def _conv_silu(raw, before, cw, cb):
    t, c = raw.shape
    groups = [before] + [raw[r:r + SUBLANES] for r in range(0, t, SUBLANES)]
    row8 = lax.broadcasted_iota(jnp.int32, (SUBLANES, c), 0)
    acc = cb
    for k in range(SSM_CONV - 1, 0, -1):
        shifted = jnp.concatenate([pltpu.roll(jnp.where(row8 >= SUBLANES - k, prev, here), k, 0)
                                   for prev, here in zip(groups[:-1], groups[1:])], axis=0)
        acc = acc + shifted * cw[SSM_CONV - 1 - k:SSM_CONV - k, :]
    acc = acc + raw * cw[SSM_CONV - 1:SSM_CONV, :]
    return acc * _sigmoid(acc), groups[-1]


def _ssd_in_proj_kernel(x_ref, g_ref, w_ref, cbuf_ref, cw_ref, cb_ref, o_ref, tail_ref, tail,
                        *, d_inner, conv_dim, tiles_per_seq, col_chunk):
    @pl.when(pl.program_id(0) % tiles_per_seq == 0)
    def _start_of_sequence():
        tail[...] = cbuf_ref[0]

    h = _rms(x_ref[...], g_ref[...]).astype(BF16)
    n_chunks = conv_dim // col_chunk
    z_chunk = d_inner // n_chunks
    for j in range(n_chunks):
        c0 = j * col_chunk
        cs = slice(c0, c0 + col_chunk)
        raw = _dot(h, w_ref[:, d_inner + c0:d_inner + c0 + col_chunk])
        zs = slice(j * z_chunk, (j + 1) * z_chunk)
        o_ref[:, zs] = _dot(h, w_ref[:, zs])
        act, last = _conv_silu(raw, tail[:, cs], cw_ref[:, cs], cb_ref[:, cs])
        o_ref[:, d_inner + c0:d_inner + c0 + col_chunk] = act
        tail[:, cs] = last
        tail_ref[0, :, cs] = last
    o_ref[:, d_inner + conv_dim:] = _dot(h, w_ref[:, d_inner + conv_dim:])


def _ssd_in_proj(x, g, wl, conv_buf, conv_w, conv_b, *, tm, seq_len, d_inner):
    m, d = x.shape
    n = wl[0].shape[2]
    conv_dim = conv_w.shape[1]
    assert m % tm == 0 and seq_len % tm == 0
    cbuf = jnp.pad(conv_buf, ((0, 0), (SUBLANES - (SSM_CONV - 1), 0), (0, 0)))
    tiles_per_seq = seq_len // tm
    body = functools.partial(_ssd_in_proj_kernel, d_inner=d_inner, conv_dim=conv_dim,
                             tiles_per_seq=tiles_per_seq, col_chunk=conv_dim // SSM_IN_PROJ_CHUNKS)
    assert conv_dim % (SSM_IN_PROJ_CHUNKS * LANES) == 0 and d_inner % (SSM_IN_PROJ_CHUNKS * LANES) == 0
    return pl.pallas_call(
        body,
        grid=(m // tm,),
        in_specs=[pl.BlockSpec((tm, d), lambda i: (i, 0)), _resident((1, d)), _layer(wl),
                  pl.BlockSpec((1, SUBLANES, conv_dim), lambda i: (i // tiles_per_seq, 0, 0)),
                  _resident((SSM_CONV, conv_dim)), _resident((1, conv_dim))],
        out_specs=[pl.BlockSpec((tm, n), lambda i: (i, 0)),
                   pl.BlockSpec((1, SUBLANES, conv_dim), lambda i: (i, 0, 0))],
        out_shape=[jax.ShapeDtypeStruct((m, n), F32),
                   jax.ShapeDtypeStruct((m // tm, SUBLANES, conv_dim), F32)],
        scratch_shapes=[pltpu.VMEM((SUBLANES, conv_dim), F32)],
        compiler_params=_cparams("arbitrary"),
        name="ssd_in_proj",
    )(x, g.reshape(1, d), wl[0], cbuf, conv_w, conv_b.reshape(1, -1))


def _mixed_heads(o_refs, l_refs):
    spg = o_refs[0].shape[0]
    pieces = [[None] * spg for _ in o_refs]
    for s in range(spg):
        ls = [l_ref[s] for l_ref in l_refs]
        mx = functools.reduce(jnp.maximum, ls)
        es = [jnp.exp(l - mx) for l in ls]
        den = functools.reduce(lambda a, b: a + b, es)
        for grp, (o_ref, e) in enumerate(zip(o_refs, es)):
            pieces[grp][s] = o_ref[s] * (e / den)
    return jnp.concatenate([p for grp in pieces for p in grp], axis=-1).astype(BF16)


def _block_tail_kernel(*refs, n_groups, final_norm):
    n_mix = 2 * n_groups if n_groups else 1
    wo_ref, x_ref, g_ref, wg_ref, wu_ref, wd_ref, gf_ref, o_ref = refs[n_mix:]
    y = _mixed_heads(refs[:n_groups], refs[n_groups:n_mix]) if n_groups else refs[0][...]
    x = x_ref[...] + _dot(y, wo_ref[...])
    h = _rms(x, g_ref[...]).astype(BF16)
    gate = _dot(h, wg_ref[...])
    up = _dot(h, wu_ref[...])
    act = (gate * _sigmoid(gate) * up).astype(BF16)
    x = x + _dot(act, wd_ref[...])
    o_ref[...] = _rms(x, gf_ref[...]) if final_norm else x


def _block_tail(mix, wol, x, g, wg, wu, wd, layer, *, tm, final_g=None):
    m, d = x.shape
    assert m % tm == 0
    gf = g if final_g is None else final_g
    if isinstance(mix, tuple):
        os_, lses = mix
        n_groups = len(os_)
        mix_args = [*os_, *lses]
        mix_specs = [pl.BlockSpec((os_[0].shape[0], tm, LANES), lambda i: (0, i, 0))] * (2 * n_groups)
    else:
        n_groups = 0
        mix_args = [mix]
        mix_specs = [pl.BlockSpec((tm, mix.shape[1]), lambda i: (i, 0))]
    return pl.pallas_call(
        functools.partial(_block_tail_kernel, n_groups=n_groups, final_norm=final_g is not None),
        grid=(m // tm,),
        in_specs=mix_specs + [_layer(wol), pl.BlockSpec((tm, d), lambda i: (i, 0)), _resident((1, d)),
                              _layer((wg, layer)), _layer((wu, layer)), _layer((wd, layer)),
                              _resident((1, d))],
        out_specs=pl.BlockSpec((tm, d), lambda i: (i, 0)),
        out_shape=jax.ShapeDtypeStruct((m, d), F32),
        compiler_params=_cparams("parallel"),
        name="proj_swiglu",
    )(*mix_args, wol[0], x, g.reshape(1, d), wg, wu, wd, gf.reshape(1, d))


def _ssd_kernel(zx_ref, cbuf_ref, h0_ref, cw_ref, cb_ref, dtb_ref, alog_ref, dsk_ref, nw_ref,
                tri_ref, triu_ref, sel_hp_ref, *rest,
                t, valid, d_inner, n_heads, conv_done, native):
    y_ref, hfin_ref, tail, yacc = rest[-4:]
    state = hfin_ref.at[0]
    c = pl.program_id(1)
    n_grp = SSM_N_GROUPS
    d_st = SSM_D_STATE
    hp = d_inner // n_grp
    hpg = n_heads // n_grp
    pdim = SSM_HEAD_DIM
    bc_w = n_grp * d_st
    conv_dim = d_inner + 2 * bc_w

    @pl.when(c == 0)
    def _init():
        state[...] = h0_ref[0]
        tail[...] = cbuf_ref[0]

    xbc = zx_ref[0, :, d_inner:d_inner + conv_dim]
    if not conv_done:
        xbc, tail[...] = _conv_silu(xbc, tail[...], cw_ref[...], cb_ref[...])
    xs = xbc[:, :d_inner]
    bm = xbc[:, d_inner:d_inner + bc_w]
    cm = xbc[:, d_inner + bc_w:]

    dt = _softplus(zx_ref[0, :, d_inner + conv_dim:] + dtb_ref[...])
    if valid < t:
        dt = jnp.where(lax.broadcasted_iota(jnp.int32, dt.shape, 0) < valid, dt, 0.0)
    a = dt * (-jnp.exp(alog_ref[...]))

    iq = lax.broadcasted_iota(jnp.int32, (t, t), 0)
    ik = lax.broadcasted_iota(jnp.int32, (t, t), 1)
    low = iq >= ik
    a_cs = _sel_left(tri_ref[...], a)
    a_cs_t = sum(lax.dot_general(part, triu_ref[...], _TN, preferred_element_type=F32)
                 for part in _split3(a))

    a_last = a_cs[t - 1:t, :]
    pad_rows = 2 * SUBLANES
    stack = jnp.concatenate([dt, dt * jnp.exp(a_last - a_cs), jnp.exp(a_cs),
                             jnp.broadcast_to(jnp.exp(a_last), (pad_rows, n_heads))], axis=0)
    ex = _spread(stack, sel_hp_ref[...])
    xdt_b = (xs * ex[0:t]).astype(BF16)
    xdst_b = (xs * ex[t:2 * t]).astype(BF16)
    from_start = ex[2 * t:3 * t]
    chunk_decay = ex[3 * t:3 * t + pad_rows]
    if native:
        only0 = jnp.where(lax.broadcasted_iota(jnp.int32, chunk_decay.shape, 0) == 0, chunk_decay, 0.0)
        terms = jnp.concatenate(_split3(only0), axis=0)
        decay_rows = lax.dot_general(terms, jnp.ones((terms.shape[0], d_st), BF16), _TN,
                                     preferred_element_type=F32)

    a_cs2 = a_cs * LOG2_E
    a_row = a_cs_t * LOG2_E

    pair = LANES // pdim
    for g in range(n_grp):
        bg = bm[:, g * d_st:(g + 1) * d_st].astype(BF16)
        cg = cm[:, g * d_st:(g + 1) * d_st].astype(BF16)
        cb = lax.dot_general(cg, bg, _NT, preferred_element_type=F32)
        sg = state[g]
        sg_b = sg.astype(BF16)
        gl = slice(g * hp, (g + 1) * hp)
        carried = lax.dot_general(cg, sg_b, _NT, preferred_element_type=F32) if native else _dot(cg, sg_b)
        yacc[:, gl] = carried * from_start[:, gl]
        for h0 in range(0, hpg, pair):
            outs = []
            for hh in range(g * hpg + h0, g * hpg + h0 + pair):
                ps = slice(hh * pdim, (hh + 1) * pdim)
                decay = jnp.exp2(jnp.where(low, a_cs2[:, hh:hh + 1] - a_row[hh:hh + 1, :], -jnp.inf))
                outs.append(_dot((cb * decay).astype(BF16), xdt_b[:, ps]))
            lo = (g * hpg + h0) * pdim
            yacc[:, lo:lo + pair * pdim] += jnp.concatenate(outs, axis=1)
        if native:
            upd = lax.dot_general(xdst_b[:, gl], bg, _TN, preferred_element_type=F32)
            state[g] = sg * decay_rows[gl, :] + upd
        else:
            upd = lax.dot_general(bg, xdst_b[:, gl], _TN, preferred_element_type=F32)
            state[g] = sg * chunk_decay[0:1, gl] + upd

    z = zx_ref[0, :, :d_inner]
    y = (yacc[...] + xs * dsk_ref[...]) * (z * _sigmoid(z))
    nw = nw_ref[...]
    for g in range(n_grp):
        gl = slice(g * hp, (g + 1) * hp)
        y_ref[0, :, gl] = _rms(y[:, gl], nw[:, gl]).astype(y_ref.dtype)


def _ssd(zx, conv_buf, h0, conv_w, conv_b, dt_bias, a_log, d_skip, norm_w, *, t, valid, conv_done=False,
         stacked=None):
    b, l, in_dim = zx.shape
    n_heads = a_log.shape[0]
    d_inner = n_heads * SSM_HEAD_DIM
    conv_dim = conv_w.shape[1]
    n_grp, d_st = SSM_N_GROUPS, SSM_D_STATE
    hp = d_inner // n_grp
    assert l % t == 0 and in_dim == d_inner + conv_dim + n_heads
    cbuf = jnp.pad(conv_buf, ((0, 0), (SUBLANES - (SSM_CONV - 1), 0), (0, 0)))
    native = stacked is not None
    extra_in, extra_specs, aliases = [], [], {}
    if native:
        layer, out_buf = stacked
        state_shape = (n_grp, hp, d_st)
        h0k = h0.reshape(h0.shape[0], b, *state_shape)
        state_spec = pl.BlockSpec((None, 1) + state_shape, lambda i, c: (layer, i, 0, 0, 0))
        state_out = jax.ShapeDtypeStruct(h0k.shape, F32)
        extra_in, extra_specs = [out_buf.reshape(h0k.shape)], [pl.BlockSpec(memory_space=pl.ANY)]
    else:
        state_shape = (n_grp, d_st, hp)
        h0k = h0.reshape(b, n_grp, n_heads // n_grp, SSM_HEAD_DIM, d_st)
        h0k = h0k.transpose(0, 1, 4, 2, 3).reshape(b, *state_shape)
        state_spec = pl.BlockSpec((1,) + state_shape, lambda i, c: (i, 0, 0, 0))
        state_out = jax.ShapeDtypeStruct(h0k.shape, F32)
    body = functools.partial(_ssd_kernel, t=t, valid=valid, d_inner=d_inner, n_heads=n_heads,
                             conv_done=conv_done, native=native)
    tri = jnp.tril(jnp.ones((t, t), BF16))
    consts = (tri, tri.T, _spread_matrix(n_heads, SSM_HEAD_DIM))
    operands = [zx, cbuf, h0k, conv_w, conv_b.reshape(1, -1), dt_bias.reshape(1, -1), a_log.reshape(1, -1),
                jnp.repeat(d_skip, SSM_HEAD_DIM).reshape(1, -1), norm_w.reshape(1, -1), *consts]
    if extra_in:
        aliases = {len(operands): 1}
    y, hfin = pl.pallas_call(
        body,
        grid=(b, l // t),
        in_specs=[
            pl.BlockSpec((1, t, in_dim), lambda i, c: (i, c, 0)),
            pl.BlockSpec((1, SUBLANES, conv_dim), lambda i, c: (i, 0, 0)),
            state_spec,
            _resident((SSM_CONV, conv_dim)), _resident((1, conv_dim)),
            _resident((1, n_heads)), _resident((1, n_heads)),
            _resident((1, d_inner)), _resident((1, d_inner)),
        ] + [_resident(cst.shape) for cst in consts] + extra_specs,
        out_specs=[pl.BlockSpec((1, t, d_inner), lambda i, c: (i, c, 0)), state_spec],
        out_shape=[jax.ShapeDtypeStruct((b, l, d_inner), BF16), state_out],
        scratch_shapes=[pltpu.VMEM((SUBLANES, conv_dim), F32), pltpu.VMEM((t, d_inner), F32)],
        input_output_aliases=aliases,
        compiler_params=_cparams("parallel", "arbitrary"),
        name="ssd_scan",
    )(*operands, *extra_in)
    if native:
        return y, hfin.reshape(h0.shape)
    hfin = hfin.reshape(b, n_grp, d_st, n_heads // n_grp, SSM_HEAD_DIM)
    hfin = hfin.transpose(0, 1, 3, 4, 2).reshape(b, n_heads, SSM_HEAD_DIM, d_st)
    return y, hfin


def _gla_kernel(p_ref, s0_ref, wg_ref, gb_ref, nw_ref, tri_ref, o_ref, sfin_ref, *, tb, ch, valid, dk, dv):
    c = pl.program_id(1)
    n_h = GLA_N_HEADS
    hk, hv = dk // n_h, dv // n_h
    state = sfin_ref.at[0]

    @pl.when(c == 0)
    def _init():
        state[...] = s0_ref[0]

    q = p_ref[0, :, 0:dk] * (hk ** -0.5)
    k = p_ref[0, :, dk:2 * dk]
    v = p_ref[0, :, 2 * dk:2 * dk + dv]
    r = p_ref[0, :, 2 * dk + dv:2 * dk + 2 * dv]
    g_low = p_ref[0, :, 2 * dk + 2 * dv:]
    x = _dot(g_low.astype(BF16), wg_ref[...]) + gb_ref[...]
    log_a = (jnp.minimum(x, 0.0) - jnp.log(1.0 + jnp.exp(-jnp.abs(x)))) / GLA_GATE_NORM
    if valid < tb:
        keep = lax.broadcasted_iota(jnp.int32, (tb, dk), 0) < valid
        log_a = jnp.where(keep, log_a, 0.0)
        k = jnp.where(keep, k, 0.0)

    bcum = _sel_left(tri_ref[...], log_a)
    nch = tb // ch

    def chunks(a):
        return a.reshape(nch, ch, a.shape[-1])

    bc3 = chunks(bcum)
    b_last = bc3[:, ch - 1:ch, :]
    q_t = chunks(q * jnp.exp(bcum)).astype(BF16)
    k_t = chunks(k * jnp.exp(-bcum)).astype(BF16)
    k_dec = (chunks(k) * jnp.exp(b_last - bc3)).astype(BF16)
    e_last = jnp.exp(b_last)
    v_b = chunks(v).astype(BF16)
    causal = (lax.broadcasted_iota(jnp.int32, (nch, ch, ch), 1)
              >= lax.broadcasted_iota(jnp.int32, (nch, ch, ch), 2))

    gate = r * _sigmoid(r)
    nw = nw_ref[...]
    for h in range(n_h):
        ks = slice(h * hk, (h + 1) * hk)
        vs = slice(h * hv, (h + 1) * hv)
        qh, kh, vh = q_t[:, :, ks], k_t[:, :, ks], v_b[:, :, vs]
        upd = jnp.einsum("jkv,jkd->jvd", vh, k_dec[:, :, ks], preferred_element_type=F32)
        s_run = state[h]
        entering = []
        for j in range(nch):
            entering.append(s_run.astype(BF16))
            s_run = s_run * e_last[j, :, ks] + upd[j]
        state[h] = s_run
        att = jnp.einsum("jqd,jkd->jqk", qh, kh, preferred_element_type=F32)
        att = jnp.where(causal, att, 0.0).astype(BF16)
        o = (jnp.einsum("jqk,jkv->jqv", att, vh, preferred_element_type=F32)
             + jnp.einsum("jqd,jvd->jqv", qh, jnp.stack(entering), preferred_element_type=F32))
        o_ref[0, :, vs] = (_rms(o.reshape(tb, hv), nw) * gate[:, vs]).astype(o_ref.dtype)


def _gla(proj, s0, w_gate, gate_bias, norm_w, *, tb, ch, valid):
    b, l, in_dim = proj.shape
    n_h = GLA_N_HEADS
    rank, dk = w_gate.shape
    hv = norm_w.shape[0]
    dv = hv * n_h
    hk = dk // n_h
    assert l % tb == 0 and tb % ch == 0 and in_dim == 2 * dk + 2 * dv + rank
    body = functools.partial(_gla_kernel, tb=tb, ch=ch, valid=valid, dk=dk, dv=dv)
    blocktri = jnp.kron(jnp.eye(tb // ch, dtype=F32), jnp.tril(jnp.ones((ch, ch), F32))).astype(BF16)
    o, sfin = pl.pallas_call(
        body,
        grid=(b, l // tb),
        in_specs=[
            pl.BlockSpec((1, tb, in_dim), lambda i, c: (i, c, 0)),
            pl.BlockSpec((1, n_h, hv, hk), lambda i, c: (i, 0, 0, 0)),
            _resident((rank, dk)), _resident((1, dk)), _resident((1, hv)), _resident((tb, tb)),
        ],
        out_specs=[
            pl.BlockSpec((1, tb, dv), lambda i, c: (i, c, 0)),
            pl.BlockSpec((1, n_h, hv, hk), lambda i, c: (i, 0, 0, 0)),
        ],
        out_shape=[
            jax.ShapeDtypeStruct((b, l, dv), BF16),
            jax.ShapeDtypeStruct((b, n_h, hv, hk), F32),
        ],
        compiler_params=_cparams("parallel", "arbitrary"),
        name="gla_scan",
    )(proj, jnp.swapaxes(s0, -1, -2), w_gate.astype(BF16), gate_bias.reshape(1, -1),
      norm_w.reshape(1, -1), blocktri)
    return o, jnp.swapaxes(sfin, -1, -2)


def _attn_prompt_kernel(q_ref, kp_ref, kc_ref, vp_ref, vc_ref, o_ref, lse_ref, *, dilation, nb):
    not_first = pl.program_id(2) > 0
    blk = ATT_BLOCK
    dh = ATT_HEAD_DIM
    units = [(r, jb) for r in range(dilation) for jb in range(nb)]
    n_u = len(units)
    iu = lax.broadcasted_iota(jnp.int32, (n_u, blk, blk), 0)
    iq = lax.broadcasted_iota(jnp.int32, (n_u, blk, blk), 1)
    ik = lax.broadcasted_iota(jnp.int32, (n_u, blk, blk), 2)
    ok_prev = (ik >= iq) & (((iu & (nb - 1)) != 0) | not_first)
    ok_cur = ik <= iq
    scale = dh ** -0.5

    def rows(r, jb):
        start = r + dilation * jb * blk
        return pl.ds(start, blk) if dilation == 1 else pl.ds(start, blk, stride=dilation)

    def gather(cur_ref, prev_ref):
        cur = jnp.stack([cur_ref[0, rows(r, jb), :] for r, jb in units])
        prev = jnp.stack([prev_ref[0, rows(r, nb - 1), :] if jb == 0 else cur_ref[0, rows(r, jb - 1), :]
                          for r, jb in units])
        return cur.astype(BF16), prev.astype(BF16)

    q = (jnp.stack([q_ref[0, rows(r, jb), :] for r, jb in units]) * scale).astype(BF16)
    k_cur, k_prev = gather(kc_ref, kp_ref)
    v_cur, v_prev = gather(vc_ref, vp_ref)
    outs, lses = [], []
    for hh in range(LANES // dh):
        hs = slice(hh * dh, (hh + 1) * dh)
        qh = q[:, :, hs]
        s_p = jnp.einsum("uqd,ukd->uqk", qh, k_prev[:, :, hs], preferred_element_type=F32)
        s_c = jnp.einsum("uqd,ukd->uqk", qh, k_cur[:, :, hs], preferred_element_type=F32)
        s_p = jnp.where(ok_prev, s_p, -jnp.inf)
        s_c = jnp.where(ok_cur, s_c, -jnp.inf)
        mx = jnp.max(jnp.maximum(s_p, s_c), axis=-1, keepdims=True)
        p_p = jnp.exp(s_p - mx)
        p_c = jnp.exp(s_c - mx)
        den = jnp.sum(p_p + p_c, axis=-1, keepdims=True)
        outs.append(jnp.einsum("uqk,ukd->uqd", (p_p / den).astype(BF16), v_prev[:, :, hs],
                               preferred_element_type=F32)
                    + jnp.einsum("uqk,ukd->uqd", (p_c / den).astype(BF16), v_cur[:, :, hs],
                                 preferred_element_type=F32))
        lses.append(jnp.broadcast_to(mx + jnp.log(den), (n_u, blk, dh)))
    o = jnp.concatenate(outs, axis=-1)
    lse = jnp.concatenate(lses, axis=-1)
    for u, (r, jb) in enumerate(units):
        o_ref[0, rows(r, jb), :] = o[u]
        lse_ref[0, rows(r, jb), :] = lse[u]


def _attn_prompt(qkv, gi, *, b, l, window, dilation):
    n_slab = qkv.shape[0] // 3
    spg = ATT_HEADS_PER_GROUP * ATT_HEAD_DIM // LANES
    span = window // dilation
    blk = ATT_BLOCK
    nb = max(1, min(l, ATT_SUPER_ROWS) // (dilation * blk))
    r_rows = dilation * nb * blk
    n_sup = l // r_rows
    assert l % r_rows == 0 and span == blk and qkv.shape[1] == b * l and nb & (nb - 1) == 0

    def spec(part, prev):
        def index(bi, sp, i):
            i = jnp.maximum(i - 1, 0) if prev else i
            return (part * n_slab + gi * spg + sp, bi * n_sup + i, 0)
        return pl.BlockSpec((1, r_rows, LANES), index)

    out_spec = pl.BlockSpec((1, r_rows, LANES), lambda bi, sp, i: (sp, bi * n_sup + i, 0))
    return pl.pallas_call(
        functools.partial(_attn_prompt_kernel, dilation=dilation, nb=nb),
        grid=(b, spg, n_sup),
        in_specs=[spec(0, False), spec(1, True), spec(1, False), spec(2, True), spec(2, False)],
        out_specs=[out_spec, out_spec],
        out_shape=[jax.ShapeDtypeStruct((spg, b * l, LANES), F32)] * 2,
        compiler_params=_cparams("parallel", "parallel", "parallel"),
        name="attn_prompt",
    )(qkv, qkv, qkv, qkv, qkv)


def _attn_sample_kernel(q_ref, kn_ref, vn_ref, cache_ref, o_ref, lse_ref, cout_ref,
                        *, window, dilation, n_new, seqs):
    dh = ATT_HEAD_DIM
    n_h = ATT_HEADS_PER_GROUP
    rows = seqs * n_new
    scale = dh ** -0.5
    units = [(bb, h) for bb in range(seqs) for h in range(n_h)]
    n_u = len(units)

    def heads_of(ref):
        return jnp.stack([ref[h * dh // LANES][:, h * dh % LANES:h * dh % LANES + dh] for _, h in units])

    q = heads_of(q_ref).astype(BF16)
    k_new, v_new = heads_of(kn_ref), heads_of(vn_ref)
    k_t = cache_ref[:, 0].reshape(n_u, dh, window)
    v_t = cache_ref[:, 1].reshape(n_u, dh, window)

    row = lax.broadcasted_iota(jnp.int32, (n_u, rows, window), 1)
    col = lax.broadcasted_iota(jnp.int32, (n_u, rows, window), 2)
    sq = row & (n_new - 1)
    ok_cache = (col >= sq) & (((col - sq) & (dilation - 1)) == 0)
    row_n = lax.broadcasted_iota(jnp.int32, (n_u, rows, rows), 1)
    col_n = lax.broadcasted_iota(jnp.int32, (n_u, rows, rows), 2)
    unit = lax.broadcasted_iota(jnp.int32, (n_u, rows, rows), 0)
    first = sum(jnp.where(unit >= bb * n_h, n_new, 0) for bb in range(1, seqs))
    back = (row_n & (n_new - 1)) - (col_n & (n_new - 1))
    ok_new = (back >= 0) & ((back & (dilation - 1)) == 0) & (col_n >= first) & (col_n < first + n_new)

    s_c = jnp.einsum("urd,udw->urw", q, k_t.astype(BF16), preferred_element_type=F32) * scale
    s_n = jnp.einsum("urd,utd->urt", q, k_new.astype(BF16), preferred_element_type=F32) * scale
    s_c = jnp.where(ok_cache, s_c, -jnp.inf)
    s_n = jnp.where(ok_new, s_n, -jnp.inf)
    mx = jnp.maximum(jnp.max(s_c, axis=-1, keepdims=True), jnp.max(s_n, axis=-1, keepdims=True))
    p_c = jnp.exp(s_c - mx)
    p_n = jnp.exp(s_n - mx)
    den = jnp.sum(p_c, axis=-1, keepdims=True) + jnp.sum(p_n, axis=-1, keepdims=True)
    o = (jnp.einsum("urw,udw->urd", (p_c / den).astype(BF16), v_t.astype(BF16), preferred_element_type=F32)
         + jnp.einsum("urt,utd->urd", (p_n / den).astype(BF16), v_new.astype(BF16),
                      preferred_element_type=F32))
    lse = jnp.broadcast_to(mx + jnp.log(den), (n_u, rows, dh))

    lane_t = lax.broadcasted_iota(jnp.int32, (2 * rows, LANES), 1)
    tok_t = lax.broadcasted_iota(jnp.int32, (2 * rows, LANES), 0)
    lane_o = lax.broadcasted_iota(jnp.int32, (dh, LANES), 1)
    zpad = jnp.zeros((rows, dh), F32)
    for u, (bb, h) in enumerate(units):
        r0 = bb * n_new
        sl = h * dh // LANES
        hs = slice(h * dh % LANES, h * dh % LANES + dh)
        o_ref[sl, r0:r0 + n_new, hs] = o[u, r0:r0 + n_new]
        lse_ref[sl, r0:r0 + n_new, hs] = lse[u, r0:r0 + n_new]
        place = jnp.where(lane_t - (LANES - n_new) == tok_t - r0, 1.0, 0.0).astype(BF16)
        for kv, old, new in ((0, k_t[u], k_new[u]), (1, v_t[u], v_new[u])):
            tail = sum(lax.dot_general(part, place, _TN, preferred_element_type=F32)
                       for part in _split3(jnp.concatenate([new, zpad], axis=0)))
            shifted = pltpu.roll(old, window - n_new, 1)
            last = jnp.where(lane_o >= LANES - n_new, tail, shifted[:, window - LANES:])
            if window > LANES:
                cout_ref[bb, kv, h, :, :window - LANES] = shifted[:, :window - LANES]
            cout_ref[bb, kv, h, :, window - LANES:] = last


def _attn_sample(qkv, cache_t, gi, *, window, dilation, n_new):
    n_slab = qkv.shape[0] // 3
    spg = ATT_HEADS_PER_GROUP * ATT_HEAD_DIM // LANES
    b = cache_t.shape[0]
    seqs = SUBLANES // n_new
    rows = seqs * n_new
    assert rows == SUBLANES and b % seqs == 0 and qkv.shape[1] == b * n_new
    assert cache_t.shape[-1] == window and window % dilation == 0 and window % LANES == 0
    assert dilation & (dilation - 1) == 0 and n_new & (n_new - 1) == 0

    def slabs(part):
        return pl.BlockSpec((spg, rows, LANES), lambda i: (part * n_slab // spg + gi, i, 0))

    cache_spec = pl.BlockSpec((seqs,) + cache_t.shape[1:], lambda i: (i, 0, 0, 0, 0))
    out_spec = pl.BlockSpec((spg, rows, LANES), lambda i: (0, i, 0))
    return pl.pallas_call(
        functools.partial(_attn_sample_kernel, window=window, dilation=dilation, n_new=n_new, seqs=seqs),
        grid=(b // seqs,),
        in_specs=[slabs(0), slabs(1), slabs(2), cache_spec],
        out_specs=[out_spec, out_spec, cache_spec],
        out_shape=[jax.ShapeDtypeStruct((spg, b * n_new, LANES), F32)] * 2
        + [jax.ShapeDtypeStruct(cache_t.shape, F32)],
        compiler_params=_cparams("parallel"),
        name="attn_sample",
    )(qkv, qkv, qkv, cache_t)


def _rope_tables(pos):
    half = ATT_ROT_DIM // 2
    lane = lax.broadcasted_iota(jnp.int32, (pos.shape[0], LANES), 1) % ATT_HEAD_DIM
    x1, rot = lane < half, lane < ATT_ROT_DIM
    freq = jnp.where(x1, lane, lane - half)
    inv_freq = ROPE_THETA ** (-(2 * freq).astype(F32) / ATT_ROT_DIM)
    ang = pos.astype(F32)[:, None] * inv_freq
    cos, sin = jnp.cos(ang), jnp.sin(ang)
    c = jnp.where(rot, cos, 1.0)
    s1 = jnp.where(rot & jnp.logical_not(x1), sin, 0.0)
    s2 = jnp.where(x1, -sin, 0.0)
    return c, s1, s2


def _pad_rows(t, rows):
    return jnp.pad(t, ((0, 0), (0, rows - t.shape[1]), (0, 0)))


def kernel(x_prompt, x_sample, state_ssm, state_ssm_conv, state_gla, cache_kv_g0, cache_kv_g1, cache_kv_g2,
           norm_mix, norm_ffn, ffn_gate, ffn_up, ffn_down,
           ssm_w_in, ssm_conv_w, ssm_conv_b, ssm_dt_bias, ssm_a_log, ssm_d, ssm_norm, ssm_w_out,
           gla_w_in, gla_w_gate, gla_gate_bias, gla_norm, gla_w_out,
           att_w_qkv, att_w_out, norm_final):
    bp, lp, d = x_prompt.shape
    bs, ls, _ = x_sample.shape
    depth = norm_mix.shape[0]
    mp, ms = bp * lp, bs * ls
    tm_p = 512 if mp % 512 == 0 else mp
    tm_wide = 2 * tm_p if lp % (2 * tm_p) == 0 else tm_p
    tm_s = ms
    xp = x_prompt.reshape(mp, d)
    xs = x_sample.reshape(ms, d)
    att_caches = (cache_kv_g0, cache_kv_g1, cache_kv_g2)
    hg_w = ATT_HEADS_PER_GROUP * ATT_HEAD_DIM
    n_att = hg_w * len(ATT_GROUPS)
    rope_p = _rope_tables(jnp.arange(lp))
    rope_s = _rope_tables(jnp.tile(PAST_LEN + jnp.arange(ls), bs))
    ffn_w = tuple(w.astype(BF16) for w in (ffn_gate, ffn_up, ffn_down))
    ssm_w_in, ssm_w_out, gla_w_in, gla_w_out, att_w_qkv, att_w_out = (
        w.astype(BF16) for w in (ssm_w_in, ssm_w_out, gla_w_in, gla_w_out, att_w_qkv, att_w_out))

    ssm_p, conv_p, conv_s, gla_p, gla_s = [], [], [], [], []
    ssm_s = jnp.zeros(state_ssm.shape, F32)
    kv_p, kv_s = [[], [], []], [[], [], []]
    for i in range(depth):
        m, j = i % N_MIXERS, i // N_MIXERS
        if m == 0:
            w_in = (ssm_w_in, j)
            conv_dim = ssm_conv_w.shape[2]
            d_inner = ssm_w_out.shape[1]
            wts = (ssm_conv_w[j], ssm_conv_b[j], ssm_dt_bias[j], ssm_a_log[j], ssm_d[j], ssm_norm[j])
            conv0 = jnp.zeros((bp, SSM_CONV - 1, conv_dim), F32)
            zx_p, tails = _ssd_in_proj(xp, norm_mix[i], w_in, conv0, ssm_conv_w[j], ssm_conv_b[j],
                                       tm=tm_p, seq_len=lp, d_inner=d_inner)
            zx_p = zx_p.reshape(bp, lp, -1)
            zx_s = _norm_matmul(xs, norm_mix[i], w_in, tm=tm_s).reshape(bs, ls, -1)
            t_p = min(SSM_CHUNK, lp)
            y_p, h_p = _ssd(zx_p, conv0, jnp.zeros((bp,) + state_ssm.shape[2:], F32), *wts,
                            t=t_p, valid=t_p, conv_done=True)
            y_s, ssm_s = _ssd(_pad_rows(zx_s, SAMPLE_PAD), state_ssm_conv[j], state_ssm, *wts,
                              t=SAMPLE_PAD, valid=ls, stacked=(j, ssm_s))
            y_s = y_s[:, :ls]
            xbc_s = zx_s[:, :, d_inner:d_inner + conv_dim]
            conv_p.append(tails.reshape(bp, lp // tm_p, SUBLANES, conv_dim)[:, -1, -(SSM_CONV - 1):])
            conv_s.append(jnp.concatenate([state_ssm_conv[j], xbc_s], axis=1)[:, -(SSM_CONV - 1):])
            ssm_p.append(h_p)
            w_out = (ssm_w_out, j)
            mix_p, mix_s = y_p.reshape(mp, -1), y_s.reshape(ms, -1)
        elif m == 1:
            w_in = (gla_w_in, j)
            wts = (gla_w_gate[j], gla_gate_bias[j], gla_norm[j])
            pr_p = _norm_matmul(xp, norm_mix[i], w_in, tm=tm_wide).reshape(bp, lp, -1)
            pr_s = _norm_matmul(xs, norm_mix[i], w_in, tm=tm_s).reshape(bs, ls, -1)
            ch_p = min(GLA_CHUNK, lp)
            tb_p = 256 if lp % 256 == 0 else ch_p
            o_p, s_p = _gla(pr_p, jnp.zeros((bp,) + state_gla.shape[2:], F32), *wts,
                            tb=tb_p, ch=ch_p, valid=tb_p)
            o_s, s_s = _gla(_pad_rows(pr_s, SAMPLE_PAD), state_gla[j], *wts,
                            tb=SAMPLE_PAD, ch=SAMPLE_PAD, valid=ls)
            o_s = o_s[:, :ls]
            gla_p.append(s_p)
            gla_s.append(s_s)
            w_out = (gla_w_out, j)
            mix_p, mix_s = o_p.reshape(mp, -1), o_s.reshape(ms, -1)
        else:
            w_qkv = (att_w_qkv, j)
            qkv_p = _norm_matmul(xp, norm_mix[i], w_qkv, tm=tm_wide, rope=(2 * n_att, rope_p))
            qkv_s = _norm_matmul(xs, norm_mix[i], w_qkv, tm=tm_s, rope=(2 * n_att, rope_s))
            n_slab = n_att // LANES
            spg = hg_w // LANES
            qkv_p4 = qkv_p.reshape(3 * n_slab, bp, lp, LANES)
            os_p, ls_p, os_s, ls_s = [], [], [], []
            for gi, (window, dilation) in enumerate(ATT_GROUPS):
                o, lse = _attn_prompt(qkv_p, gi, b=bp, l=lp, window=window, dilation=dilation)
                os_p.append(o)
                ls_p.append(lse)
                keep = min(window, lp)
                kv = jnp.stack([qkv_p4[part * n_slab + gi * spg:part * n_slab + (gi + 1) * spg, :, lp - keep:]
                                for part in (1, 2)])
                kv = kv.reshape(2, spg, bp, keep, LANES // ATT_HEAD_DIM, ATT_HEAD_DIM)
                kv_p[gi].append(kv.transpose(2, 3, 0, 1, 4, 5)
                                .reshape(bp, keep, 2, ATT_HEADS_PER_GROUP, ATT_HEAD_DIM))
                cache_t = att_caches[gi][j].transpose(0, 2, 3, 4, 1)
                o, lse, cache_new = _attn_sample(qkv_s, cache_t, gi, window=window, dilation=dilation, n_new=ls)
                os_s.append(o)
                ls_s.append(lse)
                kv_s[gi].append(cache_new.transpose(0, 4, 1, 2, 3))
            w_out = (att_w_out, j)
            mix_p, mix_s = (os_p, ls_p), (os_s, ls_s)
        final_g = norm_final if i == depth - 1 else None
        xp = _block_tail(mix_p, w_out, xp, norm_ffn[i], *ffn_w, i, tm=tm_p, final_g=final_g)
        xs = _block_tail(mix_s, w_out, xs, norm_ffn[i], *ffn_w, i, tm=tm_s, final_g=final_g)
    y_prompt = xp.reshape(bp, lp, d)
    y_sample = xs.reshape(bs, ls, d)
    return (y_prompt, y_sample,
            jnp.stack(ssm_p), ssm_s, jnp.stack(conv_p), jnp.stack(conv_s),
            jnp.stack(gla_p), jnp.stack(gla_s),
            jnp.stack(kv_p[0]), jnp.stack(kv_s[0]), jnp.stack(kv_p[1]), jnp.stack(kv_s[1]),
            jnp.stack(kv_p[2]), jnp.stack(kv_s[2]))
```

```python
import functools

import jax
import jax.numpy as jnp
from jax import lax
from jax.experimental import pallas as pl
from jax.experimental.pallas import tpu as pltpu

F32 = jnp.float32
BF16 = jnp.bfloat16
LOG2_E = 1.4426950408889634

NORM_EPS = 1e-6
N_MIXERS = 3

SSM_HEAD_DIM = 64
SSM_N_GROUPS = 4
SSM_D_STATE = 128
SSM_CONV = 4
SSM_CHUNK = 128
SSM_IN_PROJ_CHUNKS = 4

GLA_N_HEADS = 4
GLA_GATE_NORM = 16.0
GLA_CHUNK = 32

ATT_GROUPS = ((128, 1), (512, 4), (2048, 16))
ATT_HEADS_PER_GROUP = 4
ATT_HEAD_DIM = 64
ATT_ROT_DIM = ATT_HEAD_DIM // 4
ROPE_THETA = 500000.0
ATT_BLOCK = 128
ATT_SUPER_ROWS = 2048
PAST_LEN = 8192

LANES = 128
SUBLANES = 8
SAMPLE_PAD = 16
V7X_VMEM_BYTES = 64 * 1024 * 1024
VMEM_LIMIT = V7X_VMEM_BYTES * 7 // 8

_NT = (((1,), (1,)), ((), ()))
_TN = (((0,), (0,)), ((), ()))


def _cparams(*sem):
    return pltpu.CompilerParams(dimension_semantics=sem, vmem_limit_bytes=VMEM_LIMIT)


def _resident(shape):
    zeros = (0,) * len(shape)
    return pl.BlockSpec(shape, lambda *_: zeros, pipeline_mode=pl.Buffered(1))


def _layer(wl):
    w, layer = wl
    return pl.BlockSpec((None,) + w.shape[1:], lambda *_: (layer, 0, 0), pipeline_mode=pl.Buffered(1))


def _rms(x, g):
    inv = lax.rsqrt(jnp.mean(x * x, axis=-1, keepdims=True) + NORM_EPS)
    return x * inv * g


def _sigmoid(x):
    return 1.0 / (1.0 + jnp.exp(-x))


def _softplus(x):
    return jnp.maximum(x, 0.0) + jnp.log1p(jnp.exp(-jnp.abs(x)))


def _split3(v):
    hi = v.astype(BF16)
    r = v - hi.astype(F32)
    mid = r.astype(BF16)
    lo = (r - mid.astype(F32)).astype(BF16)
    return hi, mid, lo


def _dot(a, b):
    return jnp.dot(a, b, preferred_element_type=F32)


def _spread_matrix(n_blocks, width):
    k_dim = -(-3 * n_blocks // LANES) * LANES
    row = lax.broadcasted_iota(jnp.int32, (k_dim, n_blocks * width), 0)
    col = lax.broadcasted_iota(jnp.int32, (k_dim, n_blocks * width), 1)
    blk = row - jnp.where(row >= 2 * n_blocks, 2 * n_blocks, jnp.where(row >= n_blocks, n_blocks, 0))
    sel = (row < 3 * n_blocks) & (col >= blk * width) & (col < blk * width + width)
    return jnp.where(sel, 1.0, 0.0).astype(BF16)


def _spread(v, sel):
    rows, n_blocks = v.shape
    terms = [p.astype(F32) for p in _split3(v)]
    pad = jnp.zeros((rows, sel.shape[0] - 3 * n_blocks), F32)
    return _dot(jnp.concatenate(terms + [pad], axis=1).astype(BF16), sel)


def _sel_left(e, v):
    hi, mid, lo = _split3(v)
    return _dot(e, hi) + _dot(e, mid) + _dot(e, lo)


def _norm_matmul_kernel(x_ref, g_ref, w_ref, o_ref):
    h = _rms(x_ref[...], g_ref[...]).astype(BF16)
    o_ref[...] = _dot(h, w_ref[...])


def _norm_matmul_rope_kernel(x_ref, g_ref, w_ref, c_ref, s1_ref, s2_ref, o_ref, *, n_rot):
    h = _rms(x_ref[...], g_ref[...]).astype(BF16)
    res = _dot(h, w_ref[...])
    c, s1, s2 = c_ref[...], s1_ref[...], s2_ref[...]
    half = ATT_ROT_DIM // 2
    for j in range(o_ref.shape[0]):
        x = res[:, j * LANES:(j + 1) * LANES]
        if j < n_rot // LANES:
            x = x * c + pltpu.roll(x, half, 1) * s1 + pltpu.roll(x, LANES - half, 1) * s2
        o_ref[j] = x


def _norm_matmul(x, g, wl, *, tm, rope=None):
    m, d = x.shape
    n = wl[0].shape[2]
    assert m % tm == 0
    in_specs = [pl.BlockSpec((tm, d), lambda i: (i, 0)), _resident((1, d)), _layer(wl)]
    args = [x, g.reshape(1, d), wl[0]]
    if rope is None:
        body = _norm_matmul_kernel
        out_spec = pl.BlockSpec((tm, n), lambda i: (i, 0))
        out_shape = jax.ShapeDtypeStruct((m, n), F32)
    else:
        n_rot, tables = rope
        assert n % LANES == 0 and n_rot % LANES == 0
        body = functools.partial(_norm_matmul_rope_kernel, n_rot=n_rot)
        period = tables[0].shape[0] // tm
        assert tables[0].shape[0] % tm == 0 and (m // tm) % period == 0
        in_specs += [pl.BlockSpec((tm, LANES), lambda i: (i % period, 0))] * 3
        args += list(tables)
        out_spec = pl.BlockSpec((n // LANES, tm, LANES), lambda i: (0, i, 0))
        out_shape = jax.ShapeDtypeStruct((n // LANES, m, LANES), F32)
    return pl.pallas_call(
        body,
        grid=(m // tm,),
        in_specs=in_specs,
        out_specs=out_spec,
        out_shape=out_shape,
        compiler_params=_cparams("parallel"),
        name="norm_proj" if rope is None else "norm_proj_rope",
    )(*args)


def _conv_silu(raw, before, cw, cb, silu=True):
    t, c = raw.shape
    groups = [before] + [raw[r:r + SUBLANES] for r in range(0, t, SUBLANES)]
    row8 = lax.broadcasted_iota(jnp.int32, (SUBLANES, c), 0)
    acc = cb
    for k in range(SSM_CONV - 1, 0, -1):
        shifted = jnp.concatenate([pltpu.roll(jnp.where(row8 >= SUBLANES - k, prev, here), k, 0)
                                   for prev, here in zip(groups[:-1], groups[1:])], axis=0)
        acc = acc + shifted * cw[SSM_CONV - 1 - k:SSM_CONV - k, :]
    acc = acc + raw * cw[SSM_CONV - 1:SSM_CONV, :]
    return (acc * _sigmoid(acc) if silu else acc), groups[-1]


def _ssd_in_proj_kernel(x_ref, g_ref, w_ref, cbuf_ref, cw_ref, cb_ref, o_ref, tail_ref, tail,
                        *, d_inner, conv_dim, tiles_per_seq, col_chunk):
    @pl.when(pl.program_id(0) % tiles_per_seq == 0)
    def _start_of_sequence():
        tail[...] = cbuf_ref[0]

    h = _rms(x_ref[...], g_ref[...]).astype(BF16)
    n_chunks = conv_dim // col_chunk
    z_chunk = d_inner // n_chunks
    for j in range(n_chunks):
        c0 = j * col_chunk
        cs = slice(c0, c0 + col_chunk)
        raw = _dot(h, w_ref[:, d_inner + c0:d_inner + c0 + col_chunk])
        zs = slice(j * z_chunk, (j + 1) * z_chunk)
        o_ref[:, zs] = _dot(h, w_ref[:, zs])
        act, last = _conv_silu(raw, tail[:, cs], cw_ref[:, cs], cb_ref[:, cs], silu=False)
        o_ref[:, d_inner + c0:d_inner + c0 + col_chunk] = act
        tail[:, cs] = last
        tail_ref[0, :, cs] = last
    o_ref[:, d_inner + conv_dim:] = _dot(h, w_ref[:, d_inner + conv_dim:])


def _ssd_in_proj(x, g, wl, conv_buf, conv_w, conv_b, *, tm, seq_len, d_inner):
    m, d = x.shape
    n = wl[0].shape[2]
    conv_dim = conv_w.shape[1]
    assert m % tm == 0 and seq_len % tm == 0
    cbuf = jnp.pad(conv_buf, ((0, 0), (SUBLANES - (SSM_CONV - 1), 0), (0, 0)))
    tiles_per_seq = seq_len // tm
    body = functools.partial(_ssd_in_proj_kernel, d_inner=d_inner, conv_dim=conv_dim,
                             tiles_per_seq=tiles_per_seq, col_chunk=conv_dim // SSM_IN_PROJ_CHUNKS)
    assert conv_dim % (SSM_IN_PROJ_CHUNKS * LANES) == 0 and d_inner % (SSM_IN_PROJ_CHUNKS * LANES) == 0
    return pl.pallas_call(
        body,
        grid=(m // tm,),
        in_specs=[pl.BlockSpec((tm, d), lambda i: (i, 0)), _resident((1, d)), _layer(wl),
                  pl.BlockSpec((1, SUBLANES, conv_dim), lambda i: (i // tiles_per_seq, 0, 0)),
                  _resident((SSM_CONV, conv_dim)), _resident((1, conv_dim))],
        out_specs=[pl.BlockSpec((tm, n), lambda i: (i, 0)),
                   pl.BlockSpec((1, SUBLANES, conv_dim), lambda i: (i, 0, 0))],
        out_shape=[jax.ShapeDtypeStruct((m, n), F32),
                   jax.ShapeDtypeStruct((m // tm, SUBLANES, conv_dim), F32)],
        scratch_shapes=[pltpu.VMEM((SUBLANES, conv_dim), F32)],
        compiler_params=_cparams("arbitrary"),
        name="ssd_in_proj",
    )(x, g.reshape(1, d), wl[0], cbuf, conv_w, conv_b.reshape(1, -1))


def _mixed_heads(o_refs, l_refs):
    spg = o_refs[0].shape[0]
    pieces = [[None] * spg for _ in o_refs]
    for s in range(spg):
        ls = [l_ref[s] for l_ref in l_refs]
        mx = functools.reduce(jnp.maximum, ls)
        es = [jnp.exp(l - mx) for l in ls]
        den = functools.reduce(lambda a, b: a + b, es)
        for grp, (o_ref, e) in enumerate(zip(o_refs, es)):
            pieces[grp][s] = o_ref[s] * (e / den)
    return jnp.concatenate([p for grp in pieces for p in grp], axis=-1).astype(BF16)


def _block_tail_kernel(*refs, n_groups, final_norm):
    n_mix = 2 * n_groups if n_groups else 1
    wo_ref, x_ref, g_ref, wg_ref, wu_ref, wd_ref, gf_ref, o_ref = refs[n_mix:]
    y = _mixed_heads(refs[:n_groups], refs[n_groups:n_mix]) if n_groups else refs[0][...]
    x = x_ref[...] + _dot(y, wo_ref[...])
    h = _rms(x, g_ref[...]).astype(BF16)
    gate = _dot(h, wg_ref[...])
    up = _dot(h, wu_ref[...])
    act = (gate * _sigmoid(gate) * up).astype(BF16)
    x = x + _dot(act, wd_ref[...])
    o_ref[...] = _rms(x, gf_ref[...]) if final_norm else x


def _block_tail(mix, wol, x, g, wg, wu, wd, layer, *, tm, final_g=None):
    m, d = x.shape
    assert m % tm == 0
    gf = g if final_g is None else final_g
    if isinstance(mix, tuple):
        os_, lses = mix
        n_groups = len(os_)
        mix_args = [*os_, *lses]
        mix_specs = [pl.BlockSpec((os_[0].shape[0], tm, LANES), lambda i: (0, i, 0))] * (2 * n_groups)
    else:
        n_groups = 0
        mix_args = [mix]
        mix_specs = [pl.BlockSpec((tm, mix.shape[1]), lambda i: (i, 0))]
    return pl.pallas_call(
        functools.partial(_block_tail_kernel, n_groups=n_groups, final_norm=final_g is not None),
        grid=(m // tm,),
        in_specs=mix_specs + [_layer(wol), pl.BlockSpec((tm, d), lambda i: (i, 0)), _resident((1, d)),
                              _layer((wg, layer)), _layer((wu, layer)), _layer((wd, layer)),
                              _resident((1, d))],
        out_specs=pl.BlockSpec((tm, d), lambda i: (i, 0)),
        out_shape=jax.ShapeDtypeStruct((m, d), F32),
        compiler_params=_cparams("parallel"),
        name="proj_swiglu",
    )(*mix_args, wol[0], x, g.reshape(1, d), wg, wu, wd, gf.reshape(1, d))


def _ssd_kernel(zx_ref, cbuf_ref, h0_ref, cw_ref, cb_ref, dtb_ref, alog_ref, dsk_ref, nw_ref,
                tri_ref, triu_ref, sel_hp_ref, *rest,
                t, valid, d_inner, n_heads, conv_done, native):
    y_ref, hfin_ref, tail, yacc = rest[-4:]
    state = hfin_ref.at[0]
    c = pl.program_id(1)
    n_grp = SSM_N_GROUPS
    d_st = SSM_D_STATE
    hp = d_inner // n_grp
    hpg = n_heads // n_grp
    pdim = SSM_HEAD_DIM
    bc_w = n_grp * d_st
    conv_dim = d_inner + 2 * bc_w

    @pl.when(c == 0)
    def _init():
        state[...] = h0_ref[0]
        tail[...] = cbuf_ref[0]

    xbc = zx_ref[0, :, d_inner:d_inner + conv_dim]
    if conv_done:
        xbc = xbc * _sigmoid(xbc)
    else:
        xbc, tail[...] = _conv_silu(xbc, tail[...], cw_ref[...], cb_ref[...])
    xs = xbc[:, :d_inner]
    bm = xbc[:, d_inner:d_inner + bc_w]
    cm = xbc[:, d_inner + bc_w:]

    dt = _softplus(zx_ref[0, :, d_inner + conv_dim:] + dtb_ref[...])
    if valid < t:
        dt = jnp.where(lax.broadcasted_iota(jnp.int32, dt.shape, 0) < valid, dt, 0.0)
    a = dt * (-jnp.exp(alog_ref[...]))

    iq = lax.broadcasted_iota(jnp.int32, (t, t), 0)
    ik = lax.broadcasted_iota(jnp.int32, (t, t), 1)
    low = iq >= ik
    a_cs = _sel_left(tri_ref[...], a)
    a_cs_t = sum(lax.dot_general(part, triu_ref[...], _TN, preferred_element_type=F32)
                 for part in _split3(a))

    a_last = a_cs[t - 1:t, :]
    pad_rows = 2 * SUBLANES
    stack = jnp.concatenate([dt, dt * jnp.exp(a_last - a_cs), jnp.exp(a_cs),
                             jnp.broadcast_to(jnp.exp(a_last), (pad_rows, n_heads))], axis=0)
    ex = _spread(stack, sel_hp_ref[...])
    xdt_b = (xs * ex[0:t]).astype(BF16)
    xdst_b = (xs * ex[t:2 * t]).astype(BF16)
    from_start = ex[2 * t:3 * t]
    chunk_decay = ex[3 * t:3 * t + pad_rows]
    if native:
        only0 = jnp.where(lax.broadcasted_iota(jnp.int32, chunk_decay.shape, 0) == 0, chunk_decay, 0.0)
        terms = jnp.concatenate(_split3(only0), axis=0)
        decay_rows = lax.dot_general(terms, jnp.ones((terms.shape[0], d_st), BF16), _TN,
                                     preferred_element_type=F32)

    a_cs2 = a_cs * LOG2_E
    a_row = a_cs_t * LOG2_E

    pair = LANES // pdim
    for g in range(n_grp):
        bg = bm[:, g * d_st:(g + 1) * d_st].astype(BF16)
        cg = cm[:, g * d_st:(g + 1) * d_st].astype(BF16)
        cb = lax.dot_general(cg, bg, _NT, preferred_element_type=F32)
        sg = state[g]
        sg_b = sg.astype(BF16)
        gl = slice(g * hp, (g + 1) * hp)
        carried = lax.dot_general(cg, sg_b, _NT, preferred_element_type=F32) if native else _dot(cg, sg_b)
        yacc[:, gl] = carried * from_start[:, gl]
        for h0 in range(0, hpg, pair):
            outs = []
            for hh in range(g * hpg + h0, g * hpg + h0 + pair):
                ps = slice(hh * pdim, (hh + 1) * pdim)
                decay = jnp.exp2(jnp.where(low, a_cs2[:, hh:hh + 1] - a_row[hh:hh + 1, :], -jnp.inf))
                outs.append(_dot((cb * decay).astype(BF16), xdt_b[:, ps]))
            lo = (g * hpg + h0) * pdim
            yacc[:, lo:lo + pair * pdim] += jnp.concatenate(outs, axis=1)
        if native:
            upd = lax.dot_general(xdst_b[:, gl], bg, _TN, preferred_element_type=F32)
            state[g] = sg * decay_rows[gl, :] + upd
        else:
            upd = lax.dot_general(bg, xdst_b[:, gl], _TN, preferred_element_type=F32)
            state[g] = sg * chunk_decay[0:1, gl] + upd

    z = zx_ref[0, :, :d_inner]
    y = (yacc[...] + xs * dsk_ref[...]) * (z * _sigmoid(z))
    nw = nw_ref[...]
    for g in range(n_grp):
        gl = slice(g * hp, (g + 1) * hp)
        y_ref[0, :, gl] = _rms(y[:, gl], nw[:, gl]).astype(y_ref.dtype)


def _ssd(zx, conv_buf, h0, conv_w, conv_b, dt_bias, a_log, d_skip, norm_w, *, t, valid, conv_done=False,
         stacked=None):
    b, l, in_dim = zx.shape
    n_heads = a_log.shape[0]
    d_inner = n_heads * SSM_HEAD_DIM
    conv_dim = conv_w.shape[1]
    n_grp, d_st = SSM_N_GROUPS, SSM_D_STATE
    hp = d_inner // n_grp
    assert l % t == 0 and in_dim == d_inner + conv_dim + n_heads
    cbuf = jnp.pad(conv_buf, ((0, 0), (SUBLANES - (SSM_CONV - 1), 0), (0, 0)))
    native = stacked is not None
    extra_in, extra_specs, aliases = [], [], {}
    if native:
        layer, out_buf = stacked
        state_shape = (n_grp, hp, d_st)
        h0k = h0.reshape(h0.shape[0], b, *state_shape)
        state_spec = pl.BlockSpec((None, 1) + state_shape, lambda i, c: (layer, i, 0, 0, 0))
        state_out = jax.ShapeDtypeStruct(h0k.shape, F32)
        extra_in, extra_specs = [out_buf.reshape(h0k.shape)], [pl.BlockSpec(memory_space=pl.ANY)]
    else:
        state_shape = (n_grp, d_st, hp)
        h0k = h0.reshape(b, n_grp, n_heads // n_grp, SSM_HEAD_DIM, d_st)
        h0k = h0k.transpose(0, 1, 4, 2, 3).reshape(b, *state_shape)
        state_spec = pl.BlockSpec((1,) + state_shape, lambda i, c: (i, 0, 0, 0))
        state_out = jax.ShapeDtypeStruct(h0k.shape, F32)
    body = functools.partial(_ssd_kernel, t=t, valid=valid, d_inner=d_inner, n_heads=n_heads,
                             conv_done=conv_done, native=native)
    tri = jnp.tril(jnp.ones((t, t), BF16))
    consts = (tri, tri.T, _spread_matrix(n_heads, SSM_HEAD_DIM))
    operands = [zx, cbuf, h0k, conv_w, conv_b.reshape(1, -1), dt_bias.reshape(1, -1), a_log.reshape(1, -1),
                jnp.repeat(d_skip, SSM_HEAD_DIM).reshape(1, -1), norm_w.reshape(1, -1), *consts]
    if extra_in:
        aliases = {len(operands): 1}
    y, hfin = pl.pallas_call(
        body,
        grid=(b, l // t),
        in_specs=[
            pl.BlockSpec((1, t, in_dim), lambda i, c: (i, c, 0)),
            pl.BlockSpec((1, SUBLANES, conv_dim), lambda i, c: (i, 0, 0)),
            state_spec,
            _resident((SSM_CONV, conv_dim)), _resident((1, conv_dim)),
            _resident((1, n_heads)), _resident((1, n_heads)),
            _resident((1, d_inner)), _resident((1, d_inner)),
        ] + [_resident(cst.shape) for cst in consts] + extra_specs,
        out_specs=[pl.BlockSpec((1, t, d_inner), lambda i, c: (i, c, 0)), state_spec],
        out_shape=[jax.ShapeDtypeStruct((b, l, d_inner), BF16), state_out],
        scratch_shapes=[pltpu.VMEM((SUBLANES, conv_dim), F32), pltpu.VMEM((t, d_inner), F32)],
        input_output_aliases=aliases,
        compiler_params=_cparams("parallel", "arbitrary"),
        name="ssd_scan",
    )(*operands, *extra_in)
    if native:
        return y, hfin.reshape(h0.shape)
    hfin = hfin.reshape(b, n_grp, d_st, n_heads // n_grp, SSM_HEAD_DIM)
    hfin = hfin.transpose(0, 1, 3, 4, 2).reshape(b, n_heads, SSM_HEAD_DIM, d_st)
    return y, hfin


def _gla_kernel(p_ref, s0_ref, wg_ref, gb_ref, nw_ref, tri_ref, o_ref, sfin_ref, *, tb, ch, valid, dk, dv):
    c = pl.program_id(1)
    n_h = GLA_N_HEADS
    hk, hv = dk // n_h, dv // n_h
    state = sfin_ref.at[0]

    @pl.when(c == 0)
    def _init():
        state[...] = s0_ref[0]

    q = p_ref[0, :, 0:dk] * (hk ** -0.5)
    k = p_ref[0, :, dk:2 * dk]
    v = p_ref[0, :, 2 * dk:2 * dk + dv]
    r = p_ref[0, :, 2 * dk + dv:2 * dk + 2 * dv]
    g_low = p_ref[0, :, 2 * dk + 2 * dv:]
    x = _dot(g_low.astype(BF16), wg_ref[...]) + gb_ref[...]
    log_a = (jnp.minimum(x, 0.0) - jnp.log(1.0 + jnp.exp(-jnp.abs(x)))) / GLA_GATE_NORM
    if valid < tb:
        keep = lax.broadcasted_iota(jnp.int32, (tb, dk), 0) < valid
        log_a = jnp.where(keep, log_a, 0.0)
        k = jnp.where(keep, k, 0.0)

    bcum = _sel_left(tri_ref[...], log_a)
    nch = tb // ch

    def chunks(a):
        return a.reshape(nch, ch, a.shape[-1])

    bc3 = chunks(bcum)
    b_last = bc3[:, ch - 1:ch, :]
    q_t = chunks(q * jnp.exp(bcum)).astype(BF16)
    k_t = chunks(k * jnp.exp(-bcum)).astype(BF16)
    k_dec = (chunks(k) * jnp.exp(b_last - bc3)).astype(BF16)
    e_last = jnp.exp(b_last)
    v_b = chunks(v).astype(BF16)
    causal = (lax.broadcasted_iota(jnp.int32, (nch, ch, ch), 1)
              >= lax.broadcasted_iota(jnp.int32, (nch, ch, ch), 2))

    gate = r * _sigmoid(r)
    nw = nw_ref[...]
    for h in range(n_h):
        ks = slice(h * hk, (h + 1) * hk)
        vs = slice(h * hv, (h + 1) * hv)
        qh, kh, vh = q_t[:, :, ks], k_t[:, :, ks], v_b[:, :, vs]
        upd = jnp.einsum("jkv,jkd->jvd", vh, k_dec[:, :, ks], preferred_element_type=F32)
        s_run = state[h]
        entering = []
        for j in range(nch):
            entering.append(s_run.astype(BF16))
            s_run = s_run * e_last[j, :, ks] + upd[j]
        state[h] = s_run
        att = jnp.einsum("jqd,jkd->jqk", qh, kh, preferred_element_type=F32)
        att = jnp.where(causal, att, 0.0).astype(BF16)
        o = (jnp.einsum("jqk,jkv->jqv", att, vh, preferred_element_type=F32)
             + jnp.einsum("jqd,jvd->jqv", qh, jnp.stack(entering), preferred_element_type=F32))
        o_ref[0, :, vs] = (_rms(o.reshape(tb, hv), nw) * gate[:, vs]).astype(o_ref.dtype)


def _gla(proj, s0, w_gate, gate_bias, norm_w, *, tb, ch, valid):
    b, l, in_dim = proj.shape
    n_h = GLA_N_HEADS
    rank, dk = w_gate.shape
    hv = norm_w.shape[0]
    dv = hv * n_h
    hk = dk // n_h
    assert l % tb == 0 and tb % ch == 0 and in_dim == 2 * dk + 2 * dv + rank
    body = functools.partial(_gla_kernel, tb=tb, ch=ch, valid=valid, dk=dk, dv=dv)
    blocktri = jnp.kron(jnp.eye(tb // ch, dtype=F32), jnp.tril(jnp.ones((ch, ch), F32))).astype(BF16)
    o, sfin = pl.pallas_call(
        body,
        grid=(b, l // tb),
        in_specs=[
            pl.BlockSpec((1, tb, in_dim), lambda i, c: (i, c, 0)),
            pl.BlockSpec((1, n_h, hv, hk), lambda i, c: (i, 0, 0, 0)),
            _resident((rank, dk)), _resident((1, dk)), _resident((1, hv)), _resident((tb, tb)),
        ],
        out_specs=[
            pl.BlockSpec((1, tb, dv), lambda i, c: (i, c, 0)),
            pl.BlockSpec((1, n_h, hv, hk), lambda i, c: (i, 0, 0, 0)),
        ],
        out_shape=[
            jax.ShapeDtypeStruct((b, l, dv), BF16),
            jax.ShapeDtypeStruct((b, n_h, hv, hk), F32),
        ],
        compiler_params=_cparams("parallel", "arbitrary"),
        name="gla_scan",
    )(proj, jnp.swapaxes(s0, -1, -2), w_gate.astype(BF16), gate_bias.reshape(1, -1),
      norm_w.reshape(1, -1), blocktri)
    return o, jnp.swapaxes(sfin, -1, -2)


def _attn_prompt_kernel(q_ref, kp_ref, kc_ref, vp_ref, vc_ref, o_ref, lse_ref, *, dilation, nb):
    not_first = pl.program_id(2) > 0
    blk = ATT_BLOCK
    dh = ATT_HEAD_DIM
    units = [(r, jb) for r in range(dilation) for jb in range(nb)]
    n_u = len(units)
    iu = lax.broadcasted_iota(jnp.int32, (n_u, blk, blk), 0)
    iq = lax.broadcasted_iota(jnp.int32, (n_u, blk, blk), 1)
    ik = lax.broadcasted_iota(jnp.int32, (n_u, blk, blk), 2)
    ok_prev = (ik >= iq) & (((iu & (nb - 1)) != 0) | not_first)
    ok_cur = ik <= iq
    scale = dh ** -0.5

    def rows(r, jb):
        start = r + dilation * jb * blk
        return pl.ds(start, blk) if dilation == 1 else pl.ds(start, blk, stride=dilation)

    def gather(cur_ref, prev_ref):
        cur = jnp.stack([cur_ref[0, rows(r, jb), :] for r, jb in units])
        prev = jnp.stack([prev_ref[0, rows(r, nb - 1), :] if jb == 0 else cur_ref[0, rows(r, jb - 1), :]
                          for r, jb in units])
        return cur.astype(BF16), prev.astype(BF16)

    q = (jnp.stack([q_ref[0, rows(r, jb), :] for r, jb in units]) * scale).astype(BF16)
    k_cur, k_prev = gather(kc_ref, kp_ref)
    v_cur, v_prev = gather(vc_ref, vp_ref)
    outs, lses = [], []
    for hh in range(LANES // dh):
        hs = slice(hh * dh, (hh + 1) * dh)
        qh = q[:, :, hs]
        s_p = jnp.einsum("uqd,ukd->uqk", qh, k_prev[:, :, hs], preferred_element_type=F32)
        s_c = jnp.einsum("uqd,ukd->uqk", qh, k_cur[:, :, hs], preferred_element_type=F32)
        s_p = jnp.where(ok_prev, s_p, -jnp.inf)
        s_c = jnp.where(ok_cur, s_c, -jnp.inf)
        mx = jnp.max(jnp.maximum(s_p, s_c), axis=-1, keepdims=True)
        p_p = jnp.exp(s_p - mx)
        p_c = jnp.exp(s_c - mx)
        den = jnp.sum(p_p + p_c, axis=-1, keepdims=True)
        outs.append(jnp.einsum("uqk,ukd->uqd", (p_p / den).astype(BF16), v_prev[:, :, hs],
                               preferred_element_type=F32)
                    + jnp.einsum("uqk,ukd->uqd", (p_c / den).astype(BF16), v_cur[:, :, hs],
                                 preferred_element_type=F32))
        lses.append(jnp.broadcast_to(mx + jnp.log(den), (n_u, blk, dh)))
    o = jnp.concatenate(outs, axis=-1)
    lse = jnp.concatenate(lses, axis=-1)
    for u, (r, jb) in enumerate(units):
        o_ref[0, rows(r, jb), :] = o[u]
        lse_ref[0, rows(r, jb), :] = lse[u]


def _attn_prompt(qkv, gi, *, b, l, window, dilation):
    n_slab = qkv.shape[0] // 3
    spg = ATT_HEADS_PER_GROUP * ATT_HEAD_DIM // LANES
    span = window // dilation
    blk = ATT_BLOCK
    nb = max(1, min(l, ATT_SUPER_ROWS) // (dilation * blk))
    r_rows = dilation * nb * blk
    n_sup = l // r_rows
    assert l % r_rows == 0 and span == blk and qkv.shape[1] == b * l and nb & (nb - 1) == 0

    def spec(part, prev):
        def index(bi, sp, i):
            i = jnp.maximum(i - 1, 0) if prev else i
            return (part * n_slab + gi * spg + sp, bi * n_sup + i, 0)
        return pl.BlockSpec((1, r_rows, LANES), index)

    out_spec = pl.BlockSpec((1, r_rows, LANES), lambda bi, sp, i: (sp, bi * n_sup + i, 0))
    return pl.pallas_call(
        functools.partial(_attn_prompt_kernel, dilation=dilation, nb=nb),
        grid=(b, spg, n_sup),
        in_specs=[spec(0, False), spec(1, True), spec(1, False), spec(2, True), spec(2, False)],
        out_specs=[out_spec, out_spec],
        out_shape=[jax.ShapeDtypeStruct((spg, b * l, LANES), F32)] * 2,
        compiler_params=_cparams("parallel", "parallel", "parallel"),
        name="attn_prompt",
    )(qkv, qkv, qkv, qkv, qkv)


def _attn_sample_kernel(q_ref, kn_ref, vn_ref, cache_ref, o_ref, lse_ref, cout_ref,
                        *, window, dilation, n_new, seqs):
    dh = ATT_HEAD_DIM
    n_h = ATT_HEADS_PER_GROUP
    rows = seqs * n_new
    scale = dh ** -0.5
    units = [(bb, h) for bb in range(seqs) for h in range(n_h)]
    n_u = len(units)

    def heads_of(ref):
        return jnp.stack([ref[h * dh // LANES][:, h * dh % LANES:h * dh % LANES + dh] for _, h in units])

    q = heads_of(q_ref).astype(BF16)
    k_new, v_new = heads_of(kn_ref), heads_of(vn_ref)
    k_t = cache_ref[:, 0].reshape(n_u, dh, window)
    v_t = cache_ref[:, 1].reshape(n_u, dh, window)

    row = lax.broadcasted_iota(jnp.int32, (n_u, rows, window), 1)
    col = lax.broadcasted_iota(jnp.int32, (n_u, rows, window), 2)
    sq = row & (n_new - 1)
    ok_cache = (col >= sq) & (((col - sq) & (dilation - 1)) == 0)
    row_n = lax.broadcasted_iota(jnp.int32, (n_u, rows, rows), 1)
    col_n = lax.broadcasted_iota(jnp.int32, (n_u, rows, rows), 2)
    unit = lax.broadcasted_iota(jnp.int32, (n_u, rows, rows), 0)
    first = sum(jnp.where(unit >= bb * n_h, n_new, 0) for bb in range(1, seqs))
    back = (row_n & (n_new - 1)) - (col_n & (n_new - 1))
    ok_new = (back >= 0) & ((back & (dilation - 1)) == 0) & (col_n >= first) & (col_n < first + n_new)

    s_c = jnp.einsum("urd,udw->urw", q, k_t.astype(BF16), preferred_element_type=F32) * scale
    s_n = jnp.einsum("urd,utd->urt", q, k_new.astype(BF16), preferred_element_type=F32) * scale
    s_c = jnp.where(ok_cache, s_c, -jnp.inf)
    s_n = jnp.where(ok_new, s_n, -jnp.inf)
    mx = jnp.maximum(jnp.max(s_c, axis=-1, keepdims=True), jnp.max(s_n, axis=-1, keepdims=True))
    p_c = jnp.exp(s_c - mx)
    p_n = jnp.exp(s_n - mx)
    den = jnp.sum(p_c, axis=-1, keepdims=True) + jnp.sum(p_n, axis=-1, keepdims=True)
    o = (jnp.einsum("urw,udw->urd", (p_c / den).astype(BF16), v_t.astype(BF16), preferred_element_type=F32)
         + jnp.einsum("urt,utd->urd", (p_n / den).astype(BF16), v_new.astype(BF16),
                      preferred_element_type=F32))
    lse = jnp.broadcast_to(mx + jnp.log(den), (n_u, rows, dh))

    lane_t = lax.broadcasted_iota(jnp.int32, (2 * rows, LANES), 1)
    tok_t = lax.broadcasted_iota(jnp.int32, (2 * rows, LANES), 0)
    lane_o = lax.broadcasted_iota(jnp.int32, (dh, LANES), 1)
    zpad = jnp.zeros((rows, dh), F32)
    for u, (bb, h) in enumerate(units):
        r0 = bb * n_new
        sl = h * dh // LANES
        hs = slice(h * dh % LANES, h * dh % LANES + dh)
        o_ref[sl, r0:r0 + n_new, hs] = o[u, r0:r0 + n_new]
        lse_ref[sl, r0:r0 + n_new, hs] = lse[u, r0:r0 + n_new]
        place = jnp.where(lane_t - (LANES - n_new) == tok_t - r0, 1.0, 0.0).astype(BF16)
        for kv, old, new in ((0, k_t[u], k_new[u]), (1, v_t[u], v_new[u])):
            tail = sum(lax.dot_general(part, place, _TN, preferred_element_type=F32)
                       for part in _split3(jnp.concatenate([new, zpad], axis=0)))
            shifted = pltpu.roll(old, window - n_new, 1)
            last = jnp.where(lane_o >= LANES - n_new, tail, shifted[:, window - LANES:])
            if window > LANES:
                cout_ref[bb, kv, h, :, :window - LANES] = shifted[:, :window - LANES]
            cout_ref[bb, kv, h, :, window - LANES:] = last


def _attn_sample(qkv, cache_t, gi, *, window, dilation, n_new):
    n_slab = qkv.shape[0] // 3
    spg = ATT_HEADS_PER_GROUP * ATT_HEAD_DIM // LANES
    b = cache_t.shape[0]
    seqs = SUBLANES // n_new
    rows = seqs * n_new
    assert rows == SUBLANES and b % seqs == 0 and qkv.shape[1] == b * n_new
    assert cache_t.shape[-1] == window and window % dilation == 0 and window % LANES == 0
    assert dilation & (dilation - 1) == 0 and n_new & (n_new - 1) == 0

    def slabs(part):
        return pl.BlockSpec((spg, rows, LANES), lambda i: (part * n_slab // spg + gi, i, 0))

    cache_spec = pl.BlockSpec((seqs,) + cache_t.shape[1:], lambda i: (i, 0, 0, 0, 0))
    out_spec = pl.BlockSpec((spg, rows, LANES), lambda i: (0, i, 0))
    return pl.pallas_call(
        functools.partial(_attn_sample_kernel, window=window, dilation=dilation, n_new=n_new, seqs=seqs),
        grid=(b // seqs,),
        in_specs=[slabs(0), slabs(1), slabs(2), cache_spec],
        out_specs=[out_spec, out_spec, cache_spec],
        out_shape=[jax.ShapeDtypeStruct((spg, b * n_new, LANES), F32)] * 2
        + [jax.ShapeDtypeStruct(cache_t.shape, F32)],
        compiler_params=_cparams("parallel"),
        name="attn_sample",
    )(qkv, qkv, qkv, cache_t)


def _rope_tables(pos):
    half = ATT_ROT_DIM // 2
    lane = lax.broadcasted_iota(jnp.int32, (pos.shape[0], LANES), 1) % ATT_HEAD_DIM
    x1, rot = lane < half, lane < ATT_ROT_DIM
    freq = jnp.where(x1, lane, lane - half)
    inv_freq = ROPE_THETA ** (-(2 * freq).astype(F32) / ATT_ROT_DIM)
    ang = pos.astype(F32)[:, None] * inv_freq
    cos, sin = jnp.cos(ang), jnp.sin(ang)
    c = jnp.where(rot, cos, 1.0)
    s1 = jnp.where(rot & jnp.logical_not(x1), sin, 0.0)
    s2 = jnp.where(x1, -sin, 0.0)
    return c, s1, s2


def _pad_rows(t, rows):
    return jnp.pad(t, ((0, 0), (0, rows - t.shape[1]), (0, 0)))


def kernel(x_prompt, x_sample, state_ssm, state_ssm_conv, state_gla, cache_kv_g0, cache_kv_g1, cache_kv_g2,
           norm_mix, norm_ffn, ffn_gate, ffn_up, ffn_down,
           ssm_w_in, ssm_conv_w, ssm_conv_b, ssm_dt_bias, ssm_a_log, ssm_d, ssm_norm, ssm_w_out,
           gla_w_in, gla_w_gate, gla_gate_bias, gla_norm, gla_w_out,
           att_w_qkv, att_w_out, norm_final):
    bp, lp, d = x_prompt.shape
    bs, ls, _ = x_sample.shape
    depth = norm_mix.shape[0]
    mp, ms = bp * lp, bs * ls
    tm_p = 512 if mp % 512 == 0 else mp
    tm_wide = 2 * tm_p if lp % (2 * tm_p) == 0 else tm_p
    tm_s = ms
    xp = x_prompt.reshape(mp, d)
    xs = x_sample.reshape(ms, d)
    att_caches = (cache_kv_g0, cache_kv_g1, cache_kv_g2)
    hg_w = ATT_HEADS_PER_GROUP * ATT_HEAD_DIM
    n_att = hg_w * len(ATT_GROUPS)
    rope_p = _rope_tables(jnp.arange(lp))
    rope_s = _rope_tables(jnp.tile(PAST_LEN + jnp.arange(ls), bs))
    ffn_w = tuple(w.astype(BF16) for w in (ffn_gate, ffn_up, ffn_down))
    ssm_w_in, ssm_w_out, gla_w_in, gla_w_out, att_w_qkv, att_w_out = (
        w.astype(BF16) for w in (ssm_w_in, ssm_w_out, gla_w_in, gla_w_out, att_w_qkv, att_w_out))

    ssm_p, conv_p, conv_s, gla_p, gla_s = [], [], [], [], []
    ssm_s = jnp.zeros(state_ssm.shape, F32)
    kv_p, kv_s = [[], [], []], [[], [], []]
    for i in range(depth):
        m, j = i % N_MIXERS, i // N_MIXERS
        if m == 0:
            w_in = (ssm_w_in, j)
            conv_dim = ssm_conv_w.shape[2]
            d_inner = ssm_w_out.shape[1]
            wts = (ssm_conv_w[j], ssm_conv_b[j], ssm_dt_bias[j], ssm_a_log[j], ssm_d[j], ssm_norm[j])
            conv0 = jnp.zeros((bp, SSM_CONV - 1, conv_dim), F32)
            zx_p, tails = _ssd_in_proj(xp, norm_mix[i], w_in, conv0, ssm_conv_w[j], ssm_conv_b[j],
                                       tm=tm_p, seq_len=lp, d_inner=d_inner)
            zx_p = zx_p.reshape(bp, lp, -1)
            zx_s = _norm_matmul(xs, norm_mix[i], w_in, tm=tm_s).reshape(bs, ls, -1)
            t_p = min(SSM_CHUNK, lp)
            y_p, h_p = _ssd(zx_p, conv0, jnp.zeros((bp,) + state_ssm.shape[2:], F32), *wts,
                            t=t_p, valid=t_p, conv_done=True)
            y_s, ssm_s = _ssd(_pad_rows(zx_s, SAMPLE_PAD), state_ssm_conv[j], state_ssm, *wts,
                              t=SAMPLE_PAD, valid=ls, stacked=(j, ssm_s))
            y_s = y_s[:, :ls]
            xbc_s = zx_s[:, :, d_inner:d_inner + conv_dim]
            conv_p.append(tails.reshape(bp, lp // tm_p, SUBLANES, conv_dim)[:, -1, -(SSM_CONV - 1):])
            conv_s.append(jnp.concatenate([state_ssm_conv[j], xbc_s], axis=1)[:, -(SSM_CONV - 1):])
            ssm_p.append(h_p)
            w_out = (ssm_w_out, j)
            mix_p, mix_s = y_p.reshape(mp, -1), y_s.reshape(ms, -1)
        elif m == 1:
            w_in = (gla_w_in, j)
            wts = (gla_w_gate[j], gla_gate_bias[j], gla_norm[j])
            pr_p = _norm_matmul(xp, norm_mix[i], w_in, tm=tm_wide).reshape(bp, lp, -1)
            pr_s = _norm_matmul(xs, norm_mix[i], w_in, tm=tm_s).reshape(bs, ls, -1)
            ch_p = min(GLA_CHUNK, lp)
            tb_p = 256 if lp % 256 == 0 else ch_p
            o_p, s_p = _gla(pr_p, jnp.zeros((bp,) + state_gla.shape[2:], F32), *wts,
                            tb=tb_p, ch=ch_p, valid=tb_p)
            o_s, s_s = _gla(_pad_rows(pr_s, SAMPLE_PAD), state_gla[j], *wts,
                            tb=SAMPLE_PAD, ch=SAMPLE_PAD, valid=ls)
            o_s = o_s[:, :ls]
            gla_p.append(s_p)
            gla_s.append(s_s)
            w_out = (gla_w_out, j)
            mix_p, mix_s = o_p.reshape(mp, -1), o_s.reshape(ms, -1)
        else:
            w_qkv = (att_w_qkv, j)
            qkv_p = _norm_matmul(xp, norm_mix[i], w_qkv, tm=tm_wide, rope=(2 * n_att, rope_p))
            qkv_s = _norm_matmul(xs, norm_mix[i], w_qkv, tm=tm_s, rope=(2 * n_att, rope_s))
            n_slab = n_att // LANES
            spg = hg_w // LANES
            qkv_p4 = qkv_p.reshape(3 * n_slab, bp, lp, LANES)
            os_p, ls_p, os_s, ls_s = [], [], [], []
            for gi, (window, dilation) in enumerate(ATT_GROUPS):
                o, lse = _attn_prompt(qkv_p, gi, b=bp, l=lp, window=window, dilation=dilation)
                os_p.append(o)
                ls_p.append(lse)
                keep = min(window, lp)
                kv = jnp.stack([qkv_p4[part * n_slab + gi * spg:part * n_slab + (gi + 1) * spg, :, lp - keep:]
                                for part in (1, 2)])
                kv = kv.reshape(2, spg, bp, keep, LANES // ATT_HEAD_DIM, ATT_HEAD_DIM)
                kv_p[gi].append(kv.transpose(2, 3, 0, 1, 4, 5)
                                .reshape(bp, keep, 2, ATT_HEADS_PER_GROUP, ATT_HEAD_DIM))
                cache_t = att_caches[gi][j].transpose(0, 2, 3, 4, 1)
                o, lse, cache_new = _attn_sample(qkv_s, cache_t, gi, window=window, dilation=dilation, n_new=ls)
                os_s.append(o)
                ls_s.append(lse)
                kv_s[gi].append(cache_new.transpose(0, 4, 1, 2, 3))
            w_out = (att_w_out, j)
            mix_p, mix_s = (os_p, ls_p), (os_s, ls_s)
        final_g = norm_final if i == depth - 1 else None
        xp = _block_tail(mix_p, w_out, xp, norm_ffn[i], *ffn_w, i, tm=tm_p, final_g=final_g)
        xs = _block_tail(mix_s, w_out, xs, norm_ffn[i], *ffn_w, i, tm=tm_s, final_g=final_g)
    y_prompt = xp.reshape(bp, lp, d)
    y_sample = xs.reshape(bs, ls, d)
    return (y_prompt, y_sample,
            jnp.stack(ssm_p), ssm_s, jnp.stack(conv_p), jnp.stack(conv_s),
            jnp.stack(gla_p), jnp.stack(gla_s),
            jnp.stack(kv_p[0]), jnp.stack(kv_s[0]), jnp.stack(kv_p[1]), jnp.stack(kv_s[1]),
            jnp.stack(kv_p[2]), jnp.stack(kv_s[2]))
```

```python
import functools

import jax
import jax.numpy as jnp
from jax import lax
from jax.experimental import pallas as pl
from jax.experimental.pallas import tpu as pltpu

F32 = jnp.float32
BF16 = jnp.bfloat16
LOG2_E = 1.4426950408889634

NORM_EPS = 1e-6
N_MIXERS = 3

SSM_HEAD_DIM = 64
SSM_N_GROUPS = 4
SSM_D_STATE = 128
SSM_CONV = 4
SSM_CHUNK = 128
SSM_IN_PROJ_CHUNKS = 4

GLA_N_HEADS = 4
GLA_GATE_NORM = 16.0
GLA_CHUNK = 32

ATT_GROUPS = ((128, 1), (512, 4), (2048, 16))
ATT_HEADS_PER_GROUP = 4
ATT_HEAD_DIM = 64
ATT_ROT_DIM = ATT_HEAD_DIM // 4
ROPE_THETA = 500000.0
ATT_BLOCK = 128
ATT_SUPER_ROWS = 2048
PAST_LEN = 8192

LANES = 128
SUBLANES = 8
SAMPLE_PAD = 16
V7X_VMEM_BYTES = 64 * 1024 * 1024
VMEM_LIMIT = V7X_VMEM_BYTES * 7 // 8

_NT = (((1,), (1,)), ((), ()))
_TN = (((0,), (0,)), ((), ()))


def _cparams(*sem):
    return pltpu.CompilerParams(dimension_semantics=sem, vmem_limit_bytes=VMEM_LIMIT)


def _resident(shape):
    zeros = (0,) * len(shape)
    return pl.BlockSpec(shape, lambda *_: zeros, pipeline_mode=pl.Buffered(1))


def _layer(wl):
    w, layer = wl
    return pl.BlockSpec((None,) + w.shape[1:], lambda *_: (layer, 0, 0), pipeline_mode=pl.Buffered(1))


def _rms(x, g):
    inv = lax.rsqrt(jnp.mean(x * x, axis=-1, keepdims=True) + NORM_EPS)
    return x * inv * g


def _silu(x):
    h = 0.5 * x
    return h + h * jnp.tanh(h)


def _softplus(x):
    return jnp.maximum(x, 0.0) + jnp.log1p(jnp.exp(-jnp.abs(x)))


def _split3(v):
    hi = v.astype(BF16)
    r = v - hi.astype(F32)
    mid = r.astype(BF16)
    lo = (r - mid.astype(F32)).astype(BF16)
    return hi, mid, lo


def _dot(a, b):
    return jnp.dot(a, b, preferred_element_type=F32)


def _spread_matrix(n_blocks, width):
    k_dim = -(-3 * n_blocks // LANES) * LANES
    row = lax.broadcasted_iota(jnp.int32, (k_dim, n_blocks * width), 0)
    col = lax.broadcasted_iota(jnp.int32, (k_dim, n_blocks * width), 1)
    blk = row - jnp.where(row >= 2 * n_blocks, 2 * n_blocks, jnp.where(row >= n_blocks, n_blocks, 0))
    sel = (row < 3 * n_blocks) & (col >= blk * width) & (col < blk * width + width)
    return jnp.where(sel, 1.0, 0.0).astype(BF16)


def _spread(v, sel):
    rows, n_blocks = v.shape
    terms = [p.astype(F32) for p in _split3(v)]
    pad = jnp.zeros((rows, sel.shape[0] - 3 * n_blocks), F32)
    return _dot(jnp.concatenate(terms + [pad], axis=1).astype(BF16), sel)


def _sel_left(e, v):
    hi, mid, lo = _split3(v)
    return _dot(e, hi) + _dot(e, mid) + _dot(e, lo)


def _norm_matmul_kernel(x_ref, g_ref, w_ref, o_ref):
    h = _rms(x_ref[...], g_ref[...]).astype(BF16)
    o_ref[...] = _dot(h, w_ref[...])


def _norm_matmul_rope_kernel(x_ref, g_ref, w_ref, c_ref, s1_ref, s2_ref, o_ref, *, n_rot):
    h = _rms(x_ref[...], g_ref[...]).astype(BF16)
    res = _dot(h, w_ref[...])
    c, s1, s2 = c_ref[...], s1_ref[...], s2_ref[...]
    half = ATT_ROT_DIM // 2
    for j in range(o_ref.shape[0]):
        x = res[:, j * LANES:(j + 1) * LANES]
        if j < n_rot // LANES:
            x = x * c + pltpu.roll(x, half, 1) * s1 + pltpu.roll(x, LANES - half, 1) * s2
        o_ref[j] = x


def _norm_matmul(x, g, wl, *, tm, rope=None):
    m, d = x.shape
    n = wl[0].shape[2]
    assert m % tm == 0
    in_specs = [pl.BlockSpec((tm, d), lambda i: (i, 0)), _resident((1, d)), _layer(wl)]
    args = [x, g.reshape(1, d), wl[0]]
    if rope is None:
        body = _norm_matmul_kernel
        out_spec = pl.BlockSpec((tm, n), lambda i: (i, 0))
        out_shape = jax.ShapeDtypeStruct((m, n), F32)
    else:
        n_rot, tables = rope
        assert n % LANES == 0 and n_rot % LANES == 0
        body = functools.partial(_norm_matmul_rope_kernel, n_rot=n_rot)
        period = tables[0].shape[0] // tm
        assert tables[0].shape[0] % tm == 0 and (m // tm) % period == 0
        in_specs += [pl.BlockSpec((tm, LANES), lambda i: (i % period, 0))] * 3
        args += list(tables)
        out_spec = pl.BlockSpec((n // LANES, tm, LANES), lambda i: (0, i, 0))
        out_shape = jax.ShapeDtypeStruct((n // LANES, m, LANES), F32)
    return pl.pallas_call(
        body,
        grid=(m // tm,),
        in_specs=in_specs,
        out_specs=out_spec,
        out_shape=out_shape,
        compiler_params=_cparams("parallel"),
        name="norm_proj" if rope is None else "norm_proj_rope",
    )(*args)


def _conv_silu(raw, before, cw, cb, silu=True):
    t, c = raw.shape
    groups = [before] + [raw[r:r + SUBLANES] for r in range(0, t, SUBLANES)]
    row8 = lax.broadcasted_iota(jnp.int32, (SUBLANES, c), 0)
    acc = cb
    for k in range(SSM_CONV - 1, 0, -1):
        shifted = jnp.concatenate([pltpu.roll(jnp.where(row8 >= SUBLANES - k, prev, here), k, 0)
                                   for prev, here in zip(groups[:-1], groups[1:])], axis=0)
        acc = acc + shifted * cw[SSM_CONV - 1 - k:SSM_CONV - k, :]
    acc = acc + raw * cw[SSM_CONV - 1:SSM_CONV, :]
    return (_silu(acc) if silu else acc), groups[-1]


def _ssd_in_proj_kernel(x_ref, g_ref, w_ref, cbuf_ref, cw_ref, cb_ref, o_ref, tail_ref, tail,
                        *, d_inner, conv_dim, tiles_per_seq, col_chunk):
    @pl.when(pl.program_id(0) % tiles_per_seq == 0)
    def _start_of_sequence():
        tail[...] = cbuf_ref[0]

    h = _rms(x_ref[...], g_ref[...]).astype(BF16)
    n_chunks = conv_dim // col_chunk
    z_chunk = d_inner // n_chunks
    for j in range(n_chunks):
        c0 = j * col_chunk
        cs = slice(c0, c0 + col_chunk)
        raw = _dot(h, w_ref[:, d_inner + c0:d_inner + c0 + col_chunk])
        zs = slice(j * z_chunk, (j + 1) * z_chunk)
        o_ref[:, zs] = _dot(h, w_ref[:, zs])
        act, last = _conv_silu(raw, tail[:, cs], cw_ref[:, cs], cb_ref[:, cs], silu=False)
        o_ref[:, d_inner + c0:d_inner + c0 + col_chunk] = act
        tail[:, cs] = last
        tail_ref[0, :, cs] = last
    o_ref[:, d_inner + conv_dim:] = _dot(h, w_ref[:, d_inner + conv_dim:])


def _ssd_in_proj(x, g, wl, conv_buf, conv_w, conv_b, *, tm, seq_len, d_inner):
    m, d = x.shape
    n = wl[0].shape[2]
    conv_dim = conv_w.shape[1]
    assert m % tm == 0 and seq_len % tm == 0
    cbuf = jnp.pad(conv_buf, ((0, 0), (SUBLANES - (SSM_CONV - 1), 0), (0, 0)))
    tiles_per_seq = seq_len // tm
    body = functools.partial(_ssd_in_proj_kernel, d_inner=d_inner, conv_dim=conv_dim,
                             tiles_per_seq=tiles_per_seq, col_chunk=conv_dim // SSM_IN_PROJ_CHUNKS)
    assert conv_dim % (SSM_IN_PROJ_CHUNKS * LANES) == 0 and d_inner % (SSM_IN_PROJ_CHUNKS * LANES) == 0
    return pl.pallas_call(
        body,
        grid=(m // tm,),
        in_specs=[pl.BlockSpec((tm, d), lambda i: (i, 0)), _resident((1, d)), _layer(wl),
                  pl.BlockSpec((1, SUBLANES, conv_dim), lambda i: (i // tiles_per_seq, 0, 0)),
                  _resident((SSM_CONV, conv_dim)), _resident((1, conv_dim))],
        out_specs=[pl.BlockSpec((tm, n), lambda i: (i, 0)),
                   pl.BlockSpec((1, SUBLANES, conv_dim), lambda i: (i, 0, 0))],
        out_shape=[jax.ShapeDtypeStruct((m, n), F32),
                   jax.ShapeDtypeStruct((m // tm, SUBLANES, conv_dim), F32)],
        scratch_shapes=[pltpu.VMEM((SUBLANES, conv_dim), F32)],
        compiler_params=_cparams("arbitrary"),
        name="ssd_in_proj",
    )(x, g.reshape(1, d), wl[0], cbuf, conv_w, conv_b.reshape(1, -1))


def _mixed_heads(o_refs, l_refs):
    spg = o_refs[0].shape[0]
    pieces = [[None] * spg for _ in o_refs]
    for s in range(spg):
        ls = [l_ref[s] for l_ref in l_refs]
        mx = functools.reduce(jnp.maximum, ls)
        es = [jnp.exp(l - mx) for l in ls]
        den = functools.reduce(lambda a, b: a + b, es)
        for grp, (o_ref, e) in enumerate(zip(o_refs, es)):
            pieces[grp][s] = o_ref[s] * (e / den)
    return jnp.concatenate([p for grp in pieces for p in grp], axis=-1).astype(BF16)


def _block_tail_kernel(*refs, n_groups, final_norm):
    n_mix = 2 * n_groups if n_groups else 1
    wo_ref, x_ref, g_ref, wg_ref, wu_ref, wd_ref, gf_ref, o_ref = refs[n_mix:]
    y = _mixed_heads(refs[:n_groups], refs[n_groups:n_mix]) if n_groups else refs[0][...]
    x = x_ref[...] + _dot(y, wo_ref[...])
    h = _rms(x, g_ref[...]).astype(BF16)
    gate = _dot(h, wg_ref[...])
    up = _dot(h, wu_ref[...])
    act = (_silu(gate) * up).astype(BF16)
    x = x + _dot(act, wd_ref[...])
    o_ref[...] = _rms(x, gf_ref[...]) if final_norm else x


def _block_tail(mix, wol, x, g, wg, wu, wd, layer, *, tm, final_g=None):
    m, d = x.shape
    assert m % tm == 0
    gf = g if final_g is None else final_g
    if isinstance(mix, tuple):
        os_, lses = mix
        n_groups = len(os_)
        mix_args = [*os_, *lses]
        mix_specs = [pl.BlockSpec((os_[0].shape[0], tm, LANES), lambda i: (0, i, 0))] * (2 * n_groups)
    else:
        n_groups = 0
        mix_args = [mix]
        mix_specs = [pl.BlockSpec((tm, mix.shape[1]), lambda i: (i, 0))]
    return pl.pallas_call(
        functools.partial(_block_tail_kernel, n_groups=n_groups, final_norm=final_g is not None),
        grid=(m // tm,),
        in_specs=mix_specs + [_layer(wol), pl.BlockSpec((tm, d), lambda i: (i, 0)), _resident((1, d)),
                              _layer((wg, layer)), _layer((wu, layer)), _layer((wd, layer)),
                              _resident((1, d))],
        out_specs=pl.BlockSpec((tm, d), lambda i: (i, 0)),
        out_shape=jax.ShapeDtypeStruct((m, d), F32),
        compiler_params=_cparams("parallel"),
        name="proj_swiglu",
    )(*mix_args, wol[0], x, g.reshape(1, d), wg, wu, wd, gf.reshape(1, d))


def _ssd_kernel(zx_ref, cbuf_ref, h0_ref, cw_ref, cb_ref, dtb_ref, alog_ref, dsk_ref, nw_ref,
                tri_ref, triu_ref, sel_hp_ref, *rest,
                t, valid, d_inner, n_heads, conv_done, native):
    y_ref, hfin_ref, tail, yacc = rest[-4:]
    state = hfin_ref.at[0]
    c = pl.program_id(1)
    n_grp = SSM_N_GROUPS
    d_st = SSM_D_STATE
    hp = d_inner // n_grp
    hpg = n_heads // n_grp
    pdim = SSM_HEAD_DIM
    bc_w = n_grp * d_st
    conv_dim = d_inner + 2 * bc_w

    @pl.when(c == 0)
    def _init():
        state[...] = h0_ref[0]
        tail[...] = cbuf_ref[0]

    xbc = zx_ref[0, :, d_inner:d_inner + conv_dim]
    if conv_done:
        xbc = _silu(xbc)
    else:
        xbc, tail[...] = _conv_silu(xbc, tail[...], cw_ref[...], cb_ref[...])
    xs = xbc[:, :d_inner]
    bm = xbc[:, d_inner:d_inner + bc_w]
    cm = xbc[:, d_inner + bc_w:]

    dt = _softplus(zx_ref[0, :, d_inner + conv_dim:] + dtb_ref[...])
    if valid < t:
        dt = jnp.where(lax.broadcasted_iota(jnp.int32, dt.shape, 0) < valid, dt, 0.0)
    a = dt * (-jnp.exp(alog_ref[...]))

    iq = lax.broadcasted_iota(jnp.int32, (t, t), 0)
    ik = lax.broadcasted_iota(jnp.int32, (t, t), 1)
    low = iq >= ik
    a_cs = _sel_left(tri_ref[...], a)
    a_cs_t = sum(lax.dot_general(part, triu_ref[...], _TN, preferred_element_type=F32)
                 for part in _split3(a))

    a_last = a_cs[t - 1:t, :]
    pad_rows = 2 * SUBLANES
    stack = jnp.concatenate([dt, dt * jnp.exp(a_last - a_cs), jnp.exp(a_cs),
                             jnp.broadcast_to(jnp.exp(a_last), (pad_rows, n_heads))], axis=0)
    ex = _spread(stack, sel_hp_ref[...])
    xdt_b = (xs * ex[0:t]).astype(BF16)
    xdst_b = (xs * ex[t:2 * t]).astype(BF16)
    from_start = ex[2 * t:3 * t]
    chunk_decay = ex[3 * t:3 * t + pad_rows]
    if native:
        only0 = jnp.where(lax.broadcasted_iota(jnp.int32, chunk_decay.shape, 0) == 0, chunk_decay, 0.0)
        terms = jnp.concatenate(_split3(only0), axis=0)
        decay_rows = lax.dot_general(terms, jnp.ones((terms.shape[0], d_st), BF16), _TN,
                                     preferred_element_type=F32)

    a_cs2 = a_cs * LOG2_E
    a_row = a_cs_t * LOG2_E

    pair = LANES // pdim
    for g in range(n_grp):
        bg = bm[:, g * d_st:(g + 1) * d_st].astype(BF16)
        cg = cm[:, g * d_st:(g + 1) * d_st].astype(BF16)
        cb = lax.dot_general(cg, bg, _NT, preferred_element_type=F32)
        sg = state[g]
        sg_b = sg.astype(BF16)
        gl = slice(g * hp, (g + 1) * hp)
        carried = lax.dot_general(cg, sg_b, _NT, preferred_element_type=F32) if native else _dot(cg, sg_b)
        yacc[:, gl] = carried * from_start[:, gl]
        for h0 in range(0, hpg, pair):
            outs = []
            for hh in range(g * hpg + h0, g * hpg + h0 + pair):
                ps = slice(hh * pdim, (hh + 1) * pdim)
                decay = jnp.exp2(jnp.where(low, a_cs2[:, hh:hh + 1] - a_row[hh:hh + 1, :], -jnp.inf))
                outs.append(_dot((cb * decay).astype(BF16), xdt_b[:, ps]))
            lo = (g * hpg + h0) * pdim
            yacc[:, lo:lo + pair * pdim] += jnp.concatenate(outs, axis=1)
        if native:
            upd = lax.dot_general(xdst_b[:, gl], bg, _TN, preferred_element_type=F32)
            state[g] = sg * decay_rows[gl, :] + upd
        else:
            upd = lax.dot_general(bg, xdst_b[:, gl], _TN, preferred_element_type=F32)
            state[g] = sg * chunk_decay[0:1, gl] + upd

    z = zx_ref[0, :, :d_inner]
    y = (yacc[...] + xs * dsk_ref[...]) * _silu(z)
    nw = nw_ref[...]
    for g in range(n_grp):
        gl = slice(g * hp, (g + 1) * hp)
        y_ref[0, :, gl] = _rms(y[:, gl], nw[:, gl]).astype(y_ref.dtype)


def _ssd(zx, conv_buf, h0, conv_w, conv_b, dt_bias, a_log, d_skip, norm_w, *, t, valid, conv_done=False,
         stacked=None):
    b, l, in_dim = zx.shape
    n_heads = a_log.shape[0]
    d_inner = n_heads * SSM_HEAD_DIM
    conv_dim = conv_w.shape[1]
    n_grp, d_st = SSM_N_GROUPS, SSM_D_STATE
    hp = d_inner // n_grp
    assert l % t == 0 and in_dim == d_inner + conv_dim + n_heads
    cbuf = jnp.pad(conv_buf, ((0, 0), (SUBLANES - (SSM_CONV - 1), 0), (0, 0)))
    native = stacked is not None
    extra_in, extra_specs, aliases = [], [], {}
    if native:
        layer, out_buf = stacked
        state_shape = (n_grp, hp, d_st)
        h0k = h0.reshape(h0.shape[0], b, *state_shape)
        state_spec = pl.BlockSpec((None, 1) + state_shape, lambda i, c: (layer, i, 0, 0, 0))
        state_out = jax.ShapeDtypeStruct(h0k.shape, F32)
        extra_in, extra_specs = [out_buf.reshape(h0k.shape)], [pl.BlockSpec(memory_space=pl.ANY)]
    else:
        state_shape = (n_grp, d_st, hp)
        h0k = h0.reshape(b, n_grp, n_heads // n_grp, SSM_HEAD_DIM, d_st)
        h0k = h0k.transpose(0, 1, 4, 2, 3).reshape(b, *state_shape)
        state_spec = pl.BlockSpec((1,) + state_shape, lambda i, c: (i, 0, 0, 0))
        state_out = jax.ShapeDtypeStruct(h0k.shape, F32)
    body = functools.partial(_ssd_kernel, t=t, valid=valid, d_inner=d_inner, n_heads=n_heads,
                             conv_done=conv_done, native=native)
    tri = jnp.tril(jnp.ones((t, t), BF16))
    consts = (tri, tri.T, _spread_matrix(n_heads, SSM_HEAD_DIM))
    operands = [zx, cbuf, h0k, conv_w, conv_b.reshape(1, -1), dt_bias.reshape(1, -1), a_log.reshape(1, -1),
                jnp.repeat(d_skip, SSM_HEAD_DIM).reshape(1, -1), norm_w.reshape(1, -1), *consts]
    if extra_in:
        aliases = {len(operands): 1}
    y, hfin = pl.pallas_call(
        body,
        grid=(b, l // t),
        in_specs=[
            pl.BlockSpec((1, t, in_dim), lambda i, c: (i, c, 0)),
            pl.BlockSpec((1, SUBLANES, conv_dim), lambda i, c: (i, 0, 0)),
            state_spec,
            _resident((SSM_CONV, conv_dim)), _resident((1, conv_dim)),
            _resident((1, n_heads)), _resident((1, n_heads)),
            _resident((1, d_inner)), _resident((1, d_inner)),
        ] + [_resident(cst.shape) for cst in consts] + extra_specs,
        out_specs=[pl.BlockSpec((1, t, d_inner), lambda i, c: (i, c, 0)), state_spec],
        out_shape=[jax.ShapeDtypeStruct((b, l, d_inner), BF16), state_out],
        scratch_shapes=[pltpu.VMEM((SUBLANES, conv_dim), F32), pltpu.VMEM((t, d_inner), F32)],
        input_output_aliases=aliases,
        compiler_params=_cparams("parallel", "arbitrary"),
        name="ssd_scan",
    )(*operands, *extra_in)
    if native:
        return y, hfin.reshape(h0.shape)
    hfin = hfin.reshape(b, n_grp, d_st, n_heads // n_grp, SSM_HEAD_DIM)
    hfin = hfin.transpose(0, 1, 3, 4, 2).reshape(b, n_heads, SSM_HEAD_DIM, d_st)
    return y, hfin


def _gla_kernel(p_ref, s0_ref, wg_ref, gb_ref, nw_ref, tri_ref, o_ref, sfin_ref, *, tb, ch, valid, dk, dv):
    c = pl.program_id(1)
    n_h = GLA_N_HEADS
    hk, hv = dk // n_h, dv // n_h
    state = sfin_ref.at[0]

    @pl.when(c == 0)
    def _init():
        state[...] = s0_ref[0]

    q = p_ref[0, :, 0:dk] * (hk ** -0.5)
    k = p_ref[0, :, dk:2 * dk]
    v = p_ref[0, :, 2 * dk:2 * dk + dv]
    r = p_ref[0, :, 2 * dk + dv:2 * dk + 2 * dv]
    g_low = p_ref[0, :, 2 * dk + 2 * dv:]
    x = _dot(g_low.astype(BF16), wg_ref[...]) + gb_ref[...]
    log_a = (jnp.minimum(x, 0.0) - jnp.log(1.0 + jnp.exp(-jnp.abs(x)))) / GLA_GATE_NORM
    if valid < tb:
        keep = lax.broadcasted_iota(jnp.int32, (tb, dk), 0) < valid
        log_a = jnp.where(keep, log_a, 0.0)
        k = jnp.where(keep, k, 0.0)

    bcum = _sel_left(tri_ref[...], log_a)
    nch = tb // ch

    def chunks(a):
        return a.reshape(nch, ch, a.shape[-1])

    bc3 = chunks(bcum)
    b_last = bc3[:, ch - 1:ch, :]
    q_t = chunks(q * jnp.exp(bcum)).astype(BF16)
    k_t = chunks(k * jnp.exp(-bcum)).astype(BF16)
    k_dec = (chunks(k) * jnp.exp(b_last - bc3)).astype(BF16)
    e_last = jnp.exp(b_last)
    v_b = chunks(v).astype(BF16)
    causal = (lax.broadcasted_iota(jnp.int32, (nch, ch, ch), 1)
              >= lax.broadcasted_iota(jnp.int32, (nch, ch, ch), 2))

    gate = _silu(r)
    nw = nw_ref[...]
    for h in range(n_h):
        ks = slice(h * hk, (h + 1) * hk)
        vs = slice(h * hv, (h + 1) * hv)
        qh, kh, vh = q_t[:, :, ks], k_t[:, :, ks], v_b[:, :, vs]
        upd = jnp.einsum("jkv,jkd->jvd", vh, k_dec[:, :, ks], preferred_element_type=F32)
        s_run = state[h]
        entering = []
        for j in range(nch):
            entering.append(s_run.astype(BF16))
            s_run = s_run * e_last[j, :, ks] + upd[j]
        state[h] = s_run
        att = jnp.einsum("jqd,jkd->jqk", qh, kh, preferred_element_type=F32)
        att = jnp.where(causal, att, 0.0).astype(BF16)
        o = (jnp.einsum("jqk,jkv->jqv", att, vh, preferred_element_type=F32)
             + jnp.einsum("jqd,jvd->jqv", qh, jnp.stack(entering), preferred_element_type=F32))
        o_ref[0, :, vs] = (_rms(o.reshape(tb, hv), nw) * gate[:, vs]).astype(o_ref.dtype)


def _gla(proj, s0, w_gate, gate_bias, norm_w, *, tb, ch, valid):
    b, l, in_dim = proj.shape
    n_h = GLA_N_HEADS
    rank, dk = w_gate.shape
    hv = norm_w.shape[0]
    dv = hv * n_h
    hk = dk // n_h
    assert l % tb == 0 and tb % ch == 0 and in_dim == 2 * dk + 2 * dv + rank
    body = functools.partial(_gla_kernel, tb=tb, ch=ch, valid=valid, dk=dk, dv=dv)
    blocktri = jnp.kron(jnp.eye(tb // ch, dtype=F32), jnp.tril(jnp.ones((ch, ch), F32))).astype(BF16)
    o, sfin = pl.pallas_call(
        body,
        grid=(b, l // tb),
        in_specs=[
            pl.BlockSpec((1, tb, in_dim), lambda i, c: (i, c, 0)),
            pl.BlockSpec((1, n_h, hv, hk), lambda i, c: (i, 0, 0, 0)),
            _resident((rank, dk)), _resident((1, dk)), _resident((1, hv)), _resident((tb, tb)),
        ],
        out_specs=[
            pl.BlockSpec((1, tb, dv), lambda i, c: (i, c, 0)),
            pl.BlockSpec((1, n_h, hv, hk), lambda i, c: (i, 0, 0, 0)),
        ],
        out_shape=[
            jax.ShapeDtypeStruct((b, l, dv), BF16),
            jax.ShapeDtypeStruct((b, n_h, hv, hk), F32),
        ],
        compiler_params=_cparams("parallel", "arbitrary"),
        name="gla_scan",
    )(proj, jnp.swapaxes(s0, -1, -2), w_gate.astype(BF16), gate_bias.reshape(1, -1),
      norm_w.reshape(1, -1), blocktri)
    return o, jnp.swapaxes(sfin, -1, -2)


def _attn_prompt_kernel(q_ref, kp_ref, kc_ref, vp_ref, vc_ref, o_ref, lse_ref, *, dilation, nb):
    not_first = pl.program_id(2) > 0
    blk = ATT_BLOCK
    dh = ATT_HEAD_DIM
    units = [(r, jb) for r in range(dilation) for jb in range(nb)]
    n_u = len(units)
    iu = lax.broadcasted_iota(jnp.int32, (n_u, blk, blk), 0)
    iq = lax.broadcasted_iota(jnp.int32, (n_u, blk, blk), 1)
    ik = lax.broadcasted_iota(jnp.int32, (n_u, blk, blk), 2)
    ok_prev = (ik >= iq) & (((iu & (nb - 1)) != 0) | not_first)
    ok_cur = ik <= iq
    scale = dh ** -0.5

    def rows(r, jb):
        start = r + dilation * jb * blk
        return pl.ds(start, blk) if dilation == 1 else pl.ds(start, blk, stride=dilation)

    def gather(cur_ref, prev_ref):
        cur = jnp.stack([cur_ref[0, rows(r, jb), :] for r, jb in units])
        prev = jnp.stack([prev_ref[0, rows(r, nb - 1), :] if jb == 0 else cur_ref[0, rows(r, jb - 1), :]
                          for r, jb in units])
        return cur.astype(BF16), prev.astype(BF16)

    q = (jnp.stack([q_ref[0, rows(r, jb), :] for r, jb in units]) * scale).astype(BF16)
    k_cur, k_prev = gather(kc_ref, kp_ref)
    v_cur, v_prev = gather(vc_ref, vp_ref)
    outs, lses = [], []
    for hh in range(LANES // dh):
        hs = slice(hh * dh, (hh + 1) * dh)
        qh = q[:, :, hs]
        s_p = jnp.einsum("uqd,ukd->uqk", qh, k_prev[:, :, hs], preferred_element_type=F32)
        s_c = jnp.einsum("uqd,ukd->uqk", qh, k_cur[:, :, hs], preferred_element_type=F32)
        s_p = jnp.where(ok_prev, s_p, -jnp.inf)
        s_c = jnp.where(ok_cur, s_c, -jnp.inf)
        mx = jnp.max(jnp.maximum(s_p, s_c), axis=-1, keepdims=True)
        p_p = jnp.exp(s_p - mx)
        p_c = jnp.exp(s_c - mx)
        den = jnp.sum(p_p + p_c, axis=-1, keepdims=True)
        outs.append(jnp.einsum("uqk,ukd->uqd", (p_p / den).astype(BF16), v_prev[:, :, hs],
                               preferred_element_type=F32)
                    + jnp.einsum("uqk,ukd->uqd", (p_c / den).astype(BF16), v_cur[:, :, hs],
                                 preferred_element_type=F32))
        lses.append(jnp.broadcast_to(mx + jnp.log(den), (n_u, blk, dh)))
    o = jnp.concatenate(outs, axis=-1)
    lse = jnp.concatenate(lses, axis=-1)
    for u, (r, jb) in enumerate(units):
        o_ref[0, rows(r, jb), :] = o[u]
        lse_ref[0, rows(r, jb), :] = lse[u]


def _attn_prompt(qkv, gi, *, b, l, window, dilation):
    n_slab = qkv.shape[0] // 3
    spg = ATT_HEADS_PER_GROUP * ATT_HEAD_DIM // LANES
    span = window // dilation
    blk = ATT_BLOCK
    nb = max(1, min(l, ATT_SUPER_ROWS) // (dilation * blk))
    r_rows = dilation * nb * blk
    n_sup = l // r_rows
    assert l % r_rows == 0 and span == blk and qkv.shape[1] == b * l and nb & (nb - 1) == 0

    def spec(part, prev):
        def index(bi, sp, i):
            i = jnp.maximum(i - 1, 0) if prev else i
            return (part * n_slab + gi * spg + sp, bi * n_sup + i, 0)
        return pl.BlockSpec((1, r_rows, LANES), index)

    out_spec = pl.BlockSpec((1, r_rows, LANES), lambda bi, sp, i: (sp, bi * n_sup + i, 0))
    return pl.pallas_call(
        functools.partial(_attn_prompt_kernel, dilation=dilation, nb=nb),
        grid=(b, spg, n_sup),
        in_specs=[spec(0, False), spec(1, True), spec(1, False), spec(2, True), spec(2, False)],
        out_specs=[out_spec, out_spec],
        out_shape=[jax.ShapeDtypeStruct((spg, b * l, LANES), F32)] * 2,
        compiler_params=_cparams("parallel", "parallel", "parallel"),
        name="attn_prompt",
    )(qkv, qkv, qkv, qkv, qkv)


def _attn_sample_kernel(q_ref, kn_ref, vn_ref, cache_ref, o_ref, lse_ref, cout_ref,
                        *, window, dilation, n_new, seqs):
    dh = ATT_HEAD_DIM
    n_h = ATT_HEADS_PER_GROUP
    rows = seqs * n_new
    scale = dh ** -0.5
    units = [(bb, h) for bb in range(seqs) for h in range(n_h)]
    n_u = len(units)

    def heads_of(ref):
        return jnp.stack([ref[h * dh // LANES][:, h * dh % LANES:h * dh % LANES + dh] for _, h in units])

    q = heads_of(q_ref).astype(BF16)
    k_new, v_new = heads_of(kn_ref), heads_of(vn_ref)
    k_t = cache_ref[:, 0].reshape(n_u, dh, window)
    v_t = cache_ref[:, 1].reshape(n_u, dh, window)

    row = lax.broadcasted_iota(jnp.int32, (n_u, rows, window), 1)
    col = lax.broadcasted_iota(jnp.int32, (n_u, rows, window), 2)
    sq = row & (n_new - 1)
    ok_cache = (col >= sq) & (((col - sq) & (dilation - 1)) == 0)
    row_n = lax.broadcasted_iota(jnp.int32, (n_u, rows, rows), 1)
    col_n = lax.broadcasted_iota(jnp.int32, (n_u, rows, rows), 2)
    unit = lax.broadcasted_iota(jnp.int32, (n_u, rows, rows), 0)
    first = sum(jnp.where(unit >= bb * n_h, n_new, 0) for bb in range(1, seqs))
    back = (row_n & (n_new - 1)) - (col_n & (n_new - 1))
    ok_new = (back >= 0) & ((back & (dilation - 1)) == 0) & (col_n >= first) & (col_n < first + n_new)

    s_c = jnp.einsum("urd,udw->urw", q, k_t.astype(BF16), preferred_element_type=F32) * scale
    s_n = jnp.einsum("urd,utd->urt", q, k_new.astype(BF16), preferred_element_type=F32) * scale
    s_c = jnp.where(ok_cache, s_c, -jnp.inf)
    s_n = jnp.where(ok_new, s_n, -jnp.inf)
    mx = jnp.maximum(jnp.max(s_c, axis=-1, keepdims=True), jnp.max(s_n, axis=-1, keepdims=True))
    p_c = jnp.exp(s_c - mx)
    p_n = jnp.exp(s_n - mx)
    den = jnp.sum(p_c, axis=-1, keepdims=True) + jnp.sum(p_n, axis=-1, keepdims=True)
    o = (jnp.einsum("urw,udw->urd", (p_c / den).astype(BF16), v_t.astype(BF16), preferred_element_type=F32)
         + jnp.einsum("urt,utd->urd", (p_n / den).astype(BF16), v_new.astype(BF16),
                      preferred_element_type=F32))
    lse = jnp.broadcast_to(mx + jnp.log(den), (n_u, rows, dh))

    lane_t = lax.broadcasted_iota(jnp.int32, (2 * rows, LANES), 1)
    tok_t = lax.broadcasted_iota(jnp.int32, (2 * rows, LANES), 0)
    lane_o = lax.broadcasted_iota(jnp.int32, (dh, LANES), 1)
    zpad = jnp.zeros((rows, dh), F32)
    for u, (bb, h) in enumerate(units):
        r0 = bb * n_new
        sl = h * dh // LANES
        hs = slice(h * dh % LANES, h * dh % LANES + dh)
        o_ref[sl, r0:r0 + n_new, hs] = o[u, r0:r0 + n_new]
        lse_ref[sl, r0:r0 + n_new, hs] = lse[u, r0:r0 + n_new]
        place = jnp.where(lane_t - (LANES - n_new) == tok_t - r0, 1.0, 0.0).astype(BF16)
        for kv, old, new in ((0, k_t[u], k_new[u]), (1, v_t[u], v_new[u])):
            tail = sum(lax.dot_general(part, place, _TN, preferred_element_type=F32)
                       for part in _split3(jnp.concatenate([new, zpad], axis=0)))
            shifted = pltpu.roll(old, window - n_new, 1)
            last = jnp.where(lane_o >= LANES - n_new, tail, shifted[:, window - LANES:])
            if window > LANES:
                cout_ref[bb, kv, h, :, :window - LANES] = shifted[:, :window - LANES]
            cout_ref[bb, kv, h, :, window - LANES:] = last


def _attn_sample(qkv, cache_t, gi, *, window, dilation, n_new):
    n_slab = qkv.shape[0] // 3
    spg = ATT_HEADS_PER_GROUP * ATT_HEAD_DIM // LANES
    b = cache_t.shape[0]
    seqs = SUBLANES // n_new
    rows = seqs * n_new
    assert rows == SUBLANES and b % seqs == 0 and qkv.shape[1] == b * n_new
    assert cache_t.shape[-1] == window and window % dilation == 0 and window % LANES == 0
    assert dilation & (dilation - 1) == 0 and n_new & (n_new - 1) == 0

    def slabs(part):
        return pl.BlockSpec((spg, rows, LANES), lambda i: (part * n_slab // spg + gi, i, 0))

    cache_spec = pl.BlockSpec((seqs,) + cache_t.shape[1:], lambda i: (i, 0, 0, 0, 0))
    out_spec = pl.BlockSpec((spg, rows, LANES), lambda i: (0, i, 0))
    return pl.pallas_call(
        functools.partial(_attn_sample_kernel, window=window, dilation=dilation, n_new=n_new, seqs=seqs),
        grid=(b // seqs,),
        in_specs=[slabs(0), slabs(1), slabs(2), cache_spec],
        out_specs=[out_spec, out_spec, cache_spec],
        out_shape=[jax.ShapeDtypeStruct((spg, b * n_new, LANES), F32)] * 2
        + [jax.ShapeDtypeStruct(cache_t.shape, F32)],
        compiler_params=_cparams("parallel"),
        name="attn_sample",
    )(qkv, qkv, qkv, cache_t)


def _rope_tables(pos):
    half = ATT_ROT_DIM // 2
    lane = lax.broadcasted_iota(jnp.int32, (pos.shape[0], LANES), 1) % ATT_HEAD_DIM
    x1, rot = lane < half, lane < ATT_ROT_DIM
    freq = jnp.where(x1, lane, lane - half)
    inv_freq = ROPE_THETA ** (-(2 * freq).astype(F32) / ATT_ROT_DIM)
    ang = pos.astype(F32)[:, None] * inv_freq
    cos, sin = jnp.cos(ang), jnp.sin(ang)
    c = jnp.where(rot, cos, 1.0)
    s1 = jnp.where(rot & jnp.logical_not(x1), sin, 0.0)
    s2 = jnp.where(x1, -sin, 0.0)
    return c, s1, s2


def _pad_rows(t, rows):
    return jnp.pad(t, ((0, 0), (0, rows - t.shape[1]), (0, 0)))


def kernel(x_prompt, x_sample, state_ssm, state_ssm_conv, state_gla, cache_kv_g0, cache_kv_g1, cache_kv_g2,
           norm_mix, norm_ffn, ffn_gate, ffn_up, ffn_down,
           ssm_w_in, ssm_conv_w, ssm_conv_b, ssm_dt_bias, ssm_a_log, ssm_d, ssm_norm, ssm_w_out,
           gla_w_in, gla_w_gate, gla_gate_bias, gla_norm, gla_w_out,
           att_w_qkv, att_w_out, norm_final):
    bp, lp, d = x_prompt.shape
    bs, ls, _ = x_sample.shape
    depth = norm_mix.shape[0]
    mp, ms = bp * lp, bs * ls
    tm_p = 512 if mp % 512 == 0 else mp
    tm_wide = 2 * tm_p if lp % (2 * tm_p) == 0 else tm_p
    tm_s = ms
    xp = x_prompt.reshape(mp, d)
    xs = x_sample.reshape(ms, d)
    att_caches = (cache_kv_g0, cache_kv_g1, cache_kv_g2)
    hg_w = ATT_HEADS_PER_GROUP * ATT_HEAD_DIM
    n_att = hg_w * len(ATT_GROUPS)
    rope_p = _rope_tables(jnp.arange(lp))
    rope_s = _rope_tables(jnp.tile(PAST_LEN + jnp.arange(ls), bs))
    ffn_w = tuple(w.astype(BF16) for w in (ffn_gate, ffn_up, ffn_down))
    ssm_w_in, ssm_w_out, gla_w_in, gla_w_out, att_w_qkv, att_w_out = (
        w.astype(BF16) for w in (ssm_w_in, ssm_w_out, gla_w_in, gla_w_out, att_w_qkv, att_w_out))

    ssm_p, conv_p, conv_s, gla_p, gla_s = [], [], [], [], []
    ssm_s = jnp.zeros(state_ssm.shape, F32)
    kv_p, kv_s = [[], [], []], [[], [], []]
    for i in range(depth):
        m, j = i % N_MIXERS, i // N_MIXERS
        if m == 0:
            w_in = (ssm_w_in, j)
            conv_dim = ssm_conv_w.shape[2]
            d_inner = ssm_w_out.shape[1]
            wts = (ssm_conv_w[j], ssm_conv_b[j], ssm_dt_bias[j], ssm_a_log[j], ssm_d[j], ssm_norm[j])
            conv0 = jnp.zeros((bp, SSM_CONV - 1, conv_dim), F32)
            zx_p, tails = _ssd_in_proj(xp, norm_mix[i], w_in, conv0, ssm_conv_w[j], ssm_conv_b[j],
                                       tm=tm_p, seq_len=lp, d_inner=d_inner)
            zx_p = zx_p.reshape(bp, lp, -1)
            zx_s = _norm_matmul(xs, norm_mix[i], w_in, tm=tm_s).reshape(bs, ls, -1)
            t_p = min(SSM_CHUNK, lp)
            y_p, h_p = _ssd(zx_p, conv0, jnp.zeros((bp,) + state_ssm.shape[2:], F32), *wts,
                            t=t_p, valid=t_p, conv_done=True)
            y_s, ssm_s = _ssd(_pad_rows(zx_s, SAMPLE_PAD), state_ssm_conv[j], state_ssm, *wts,
                              t=SAMPLE_PAD, valid=ls, stacked=(j, ssm_s))
            y_s = y_s[:, :ls]
            xbc_s = zx_s[:, :, d_inner:d_inner + conv_dim]
            conv_p.append(tails.reshape(bp, lp // tm_p, SUBLANES, conv_dim)[:, -1, -(SSM_CONV - 1):])
            conv_s.append(jnp.concatenate([state_ssm_conv[j], xbc_s], axis=1)[:, -(SSM_CONV - 1):])
            ssm_p.append(h_p)
            w_out = (ssm_w_out, j)
            mix_p, mix_s = y_p.reshape(mp, -1), y_s.reshape(ms, -1)
        elif m == 1:
            w_in = (gla_w_in, j)
            wts = (gla_w_gate[j], gla_gate_bias[j], gla_norm[j])
            pr_p = _norm_matmul(xp, norm_mix[i], w_in, tm=tm_wide).reshape(bp, lp, -1)
            pr_s = _norm_matmul(xs, norm_mix[i], w_in, tm=tm_s).reshape(bs, ls, -1)
            ch_p = min(GLA_CHUNK, lp)
            tb_p = 256 if lp % 256 == 0 else ch_p
            o_p, s_p = _gla(pr_p, jnp.zeros((bp,) + state_gla.shape[2:], F32), *wts,
                            tb=tb_p, ch=ch_p, valid=tb_p)
            o_s, s_s = _gla(_pad_rows(pr_s, SAMPLE_PAD), state_gla[j], *wts,
                            tb=SAMPLE_PAD, ch=SAMPLE_PAD, valid=ls)
            o_s = o_s[:, :ls]
            gla_p.append(s_p)
            gla_s.append(s_s)
            w_out = (gla_w_out, j)
            mix_p, mix_s = o_p.reshape(mp, -1), o_s.reshape(ms, -1)
        else:
            w_qkv = (att_w_qkv, j)
            qkv_p = _norm_matmul(xp, norm_mix[i], w_qkv, tm=tm_wide, rope=(2 * n_att, rope_p))
            qkv_s = _norm_matmul(xs, norm_mix[i], w_qkv, tm=tm_s, rope=(2 * n_att, rope_s))
            n_slab = n_att // LANES
            spg = hg_w // LANES
            qkv_p4 = qkv_p.reshape(3 * n_slab, bp, lp, LANES)
            os_p, ls_p, os_s, ls_s = [], [], [], []
            for gi, (window, dilation) in enumerate(ATT_GROUPS):
                o, lse = _attn_prompt(qkv_p, gi, b=bp, l=lp, window=window, dilation=dilation)
                os_p.append(o)
                ls_p.append(lse)
                keep = min(window, lp)
                kv = jnp.stack([qkv_p4[part * n_slab + gi * spg:part * n_slab + (gi + 1) * spg, :, lp - keep:]
                                for part in (1, 2)])
                kv = kv.reshape(2, spg, bp, keep, LANES // ATT_HEAD_DIM, ATT_HEAD_DIM)
                kv_p[gi].append(kv.transpose(2, 3, 0, 1, 4, 5)
                                .reshape(bp, keep, 2, ATT_HEADS_PER_GROUP, ATT_HEAD_DIM))
                cache_t = att_caches[gi][j].transpose(0, 2, 3, 4, 1)
                o, lse, cache_new = _attn_sample(qkv_s, cache_t, gi, window=window, dilation=dilation, n_new=ls)
                os_s.append(o)
                ls_s.append(lse)
                kv_s[gi].append(cache_new.transpose(0, 4, 1, 2, 3))
            w_out = (att_w_out, j)
            mix_p, mix_s = (os_p, ls_p), (os_s, ls_s)
        final_g = norm_final if i == depth - 1 else None
        xp = _block_tail(mix_p, w_out, xp, norm_ffn[i], *ffn_w, i, tm=tm_p, final_g=final_g)
        xs = _block_tail(mix_s, w_out, xs, norm_ffn[i], *ffn_w, i, tm=tm_s, final_g=final_g)
    y_prompt = xp.reshape(bp, lp, d)
    y_sample = xs.reshape(bs, ls, d)
    return (y_prompt, y_sample,
            jnp.stack(ssm_p), ssm_s, jnp.stack(conv_p), jnp.stack(conv_s),
            jnp.stack(gla_p), jnp.stack(gla_s),
            jnp.stack(kv_p[0]), jnp.stack(kv_s[0]), jnp.stack(kv_p[1]), jnp.stack(kv_s[1]),
            jnp.stack(kv_p[2]), jnp.stack(kv_s[2]))
```
